```python
import math
import jax, jax.numpy as jnp
from jax import lax
import numpy as np

D_MODEL = 4096
BATCH = 1
SEQ = 16384
DEPTH = 1

HEAD_DIM = 128
DSA_HEADS = 16
DSA_KV_HEADS = 4
IDX_HEADS = 32
IDX_DIM = 64
DSA_TOPK = 256
NSA_HEADS = 16
NSA_KV_HEADS = 4
CMP_LEN = 32
CMP_STRIDE = 16
CMP_HIDDEN = 256
SLC_LEN = 64
SLC_TOP = 16
WIN = 512
FORCE_SCORE = 1e9
PEER_HEADS = 8
PEER_NKEYS = 128
PEER_QDIM = 256
PEER_TOPK = 16
N_EXPERTS = PEER_NKEYS * PEER_NKEYS
REL_BUCKETS = 32
REL_MAX_DIST = 2048
N_ATTN_HEADS = DSA_HEADS + NSA_HEADS
QBLK = 128
EPS = 1e-6
IN_SPLITS = (DSA_HEADS * HEAD_DIM, DSA_KV_HEADS * HEAD_DIM, DSA_KV_HEADS * HEAD_DIM,
             IDX_HEADS * IDX_DIM, IDX_DIM, IDX_HEADS,
             NSA_HEADS * HEAD_DIM, 6 * NSA_KV_HEADS * HEAD_DIM, 3 * NSA_HEADS,
             2 * D_MODEL)
IN_COLS = sum(IN_SPLITS)

kernel_name = 'hybrid_dsa_nsa_peer_block'


def rms_norm(x, g):
    xf = x.astype(jnp.float32)
    y = xf * lax.rsqrt(jnp.mean(xf * xf, axis=-1, keepdims=True) + EPS)
    return (y * g.astype(jnp.float32)).astype(x.dtype)


def rel_bucket(dist):
    n = jnp.maximum(dist, 0)
    exact = REL_BUCKETS // 2
    nf = jnp.maximum(n, 1).astype(jnp.float32)
    log_b = exact + (jnp.log(nf / exact) / math.log(REL_MAX_DIST / exact) * (REL_BUCKETS - exact)).astype(jnp.int32)
    return jnp.where(n < exact, n, jnp.minimum(log_b, REL_BUCKETS - 1))


def masked_softmax(logits, mask):
    s = jnp.where(mask, logits.astype(jnp.float32), -jnp.inf)
    m = jnp.max(s, axis=-1, keepdims=True)
    m = jnp.where(jnp.isfinite(m), m, 0.0)
    e = jnp.exp(s - m)
    return e / jnp.maximum(jnp.sum(e, axis=-1, keepdims=True), 1e-30)


def map_blocks(fn, n_tokens):
    nb = n_tokens // QBLK
    out = lax.map(fn, jnp.arange(nb) * QBLK)
    out = jnp.moveaxis(out, 0, 1)
    return out.reshape(out.shape[0], nb * QBLK, *out.shape[3:])


def dsa_mixer(q, k, v, qi, ki, wi, table):
    B, T, H, dh = q.shape
    G = k.shape[2]
    R = H // G
    n_top = min(DSA_TOPK, T // 4)
    key_pos = jnp.arange(T)
    scale = dh ** -0.5
    wi = wi * IDX_HEADS ** -0.5

    def block(q0):
        tq = q0 + jnp.arange(QBLK)
        qi_b = lax.dynamic_slice_in_dim(qi, q0, QBLK, axis=1)
        wi_b = lax.dynamic_slice_in_dim(wi, q0, QBLK, axis=1)
        s = jnp.einsum('bqhd,bkd->bqhk', qi_b, ki) * IDX_DIM ** -0.5
        score = jnp.einsum('bqhk,bqh->bqk', jax.nn.relu(s), wi_b).astype(jnp.float32)
        score = jnp.where((key_pos[None, :] <= tq[:, None])[None], score, -jnp.inf)
        _, idx = lax.top_k(score, n_top)
        valid = idx <= tq[None, :, None]
        k_sel = jax.vmap(lambda a, i: a[i])(k, idx)
        v_sel = jax.vmap(lambda a, i: a[i])(v, idx)
        q_b = lax.dynamic_slice_in_dim(q, q0, QBLK, axis=1).reshape(B, QBLK, G, R, dh)
        logits = jnp.einsum('bqgrd,bqkgd->bqgrk', q_b, k_sel).astype(jnp.float32) * scale
        bias = table[rel_bucket(tq[None, :, None] - idx)]
        bias = jnp.moveaxis(bias.reshape(B, QBLK, n_top, G, R), 2, -1)
        p = masked_softmax(logits + bias, valid[:, :, None, None, :])
        o = jnp.einsum('bqgrk,bqkgd->bqgrd', p.astype(v.dtype), v_sel)
        return o.reshape(B, QBLK, H * dh)

    return map_blocks(block, T)


def compress_blocks(x, pe, w1, w2):
    T = x.shape[1]
    n_cmp = (T - CMP_LEN) // CMP_STRIDE + 1
    pos = (jnp.arange(n_cmp) * CMP_STRIDE)[:, None] + jnp.arange(CMP_LEN)[None, :]
    blk = x[:, pos] + pe[None, None, :, None, :]
    hid = jax.nn.gelu(jnp.einsum('bnlgd,lde->bnge', blk, w1))
    return jnp.einsum('bnge,ed->bngd', hid, w2)


def nsa_mixer(q, k_cmp, v_cmp, ks, vs, kw, vw, gates, table):
    B, T, H, dh = q.shape
    G = ks.shape[2]
    R = H // G
    n_cmp = k_cmp.shape[1]
    n_slc = T // SLC_LEN
    n_sel = min(SLC_TOP, n_slc)
    scale = dh ** -0.5
    cmp_end = jnp.arange(n_cmp) * CMP_STRIDE + CMP_LEN - 1
    c_start = jnp.arange(n_cmp)[:, None] * CMP_STRIDE
    s_start = jnp.arange(n_slc)[None, :] * SLC_LEN
    overlap = ((c_start < s_start + SLC_LEN) & (c_start + CMP_LEN > s_start)).astype(jnp.float32)
    tbl = table.reshape(REL_BUCKETS, G, R)
    ks_blk = jnp.moveaxis(ks.reshape(B, n_slc, SLC_LEN, G, dh), 3, 1)
    vs_blk = jnp.moveaxis(vs.reshape(B, n_slc, SLC_LEN, G, dh), 3, 1)
    kw_pad = jnp.pad(kw, ((0, 0), (WIN, 0), (0, 0), (0, 0)))
    vw_pad = jnp.pad(vw, ((0, 0), (WIN, 0), (0, 0), (0, 0)))
    b_idx = jnp.arange(B)[:, None, None, None]
    g_idx = jnp.arange(G)[None, None, :, None]
    blk_id = jnp.arange(n_slc)

    def block(q0):
        tq = q0 + jnp.arange(QBLK)
        q_b = lax.dynamic_slice_in_dim(q, q0, QBLK, axis=1).reshape(B, QBLK, G, R, dh)
        lc = jnp.einsum('bqgrd,bngd->bqgrn', q_b, k_cmp).astype(jnp.float32) * scale
        dist_c = tq[:, None] - cmp_end[None, :]
        lc = lc + jnp.transpose(tbl[rel_bucket(dist_c)], (0, 2, 3, 1))
        pc = masked_softmax(lc, (dist_c >= 0)[:, None, None, :])
        o_c = jnp.einsum('bqgrn,bngd->bqgrd', pc.astype(v_cmp.dtype), v_cmp)
        imp = jnp.einsum('bqgn,nm->bqgm', pc.sum(axis=3), overlap)
        cur = tq // SLC_LEN
        forced = (blk_id[None, :] == 0) | (blk_id[None, :] == cur[:, None]) | (blk_id[None, :] == cur[:, None] - 1)
        admissible = blk_id[None, :] * SLC_LEN <= tq[:, None]
        imp = jnp.where(forced[None, :, None, :], FORCE_SCORE, imp)
        imp = jnp.where(admissible[None, :, None, :], imp, -jnp.inf)
        _, sel = lax.top_k(imp, n_sel)
        k_sel = ks_blk[b_idx, g_idx, sel]
        v_sel = vs_blk[b_idx, g_idx, sel]
        pos = sel[..., None] * SLC_LEN + jnp.arange(SLC_LEN)
        dist_s = tq[None, :, None, None, None] - pos
        ls = jnp.einsum('bqgrd,bqgnld->bqgrnl', q_b, k_sel).astype(jnp.float32) * scale
        bias_s = tbl[rel_bucket(dist_s), g_idx[..., None]]
        ls = (ls + jnp.moveaxis(bias_s, -1, 3)).reshape(B, QBLK, G, R, n_sel * SLC_LEN)
        ps = masked_softmax(ls, (dist_s >= 0).reshape(B, QBLK, G, 1, n_sel * SLC_LEN))
        o_s = jnp.einsum('bqgrk,bqgkd->bqgrd', ps.astype(vs.dtype), v_sel.reshape(B, QBLK, G, n_sel * SLC_LEN, dh))
        kw_b = lax.dynamic_slice_in_dim(kw_pad, q0, WIN + QBLK, axis=1)
        vw_b = lax.dynamic_slice_in_dim(vw_pad, q0, WIN + QBLK, axis=1)
        kpos = q0 - WIN + jnp.arange(WIN + QBLK)
        dist_w = tq[:, None] - kpos[None, :]
        mask_w = (dist_w >= 0) & (dist_w < WIN) & (kpos[None, :] >= 0)
        lw = jnp.einsum('bqgrd,bkgd->bqgrk', q_b, kw_b).astype(jnp.float32) * scale
        lw = lw + jnp.transpose(tbl[rel_bucket(dist_w)], (0, 2, 3, 1))
        pw = masked_softmax(lw, mask_w[:, None, None, :])
        o_w = jnp.einsum('bqgrk,bkgd->bqgrd', pw.astype(vw.dtype), vw_b)
        g_b = jax.nn.sigmoid(lax.dynamic_slice_in_dim(gates, q0, QBLK, axis=1).astype(jnp.float32)).reshape(B, QBLK, G, R, 3)
        o = g_b[..., 0:1] * o_c + g_b[..., 1:2] * o_s + g_b[..., 2:3] * o_w
        return o.astype(q.dtype).reshape(B, QBLK, H * dh)

    return map_blocks(block, T)


def token_mixer(h, table, w_in, gq_a, gk_a, gq_b, gk_cmp, gk_slc, gk_win, cmp_pe_k, cmp_w1_k, cmp_w2_k,
                cmp_pe_v, cmp_w1_v, cmp_w2_v, w_branch_a, w_branch_b, w_out):
    B, T, _ = h.shape
    proj = h @ w_in
    offs = [int(o) for o in np.cumsum(IN_SPLITS)[:-1]]
    qa, ka, va, qi, ki, wi, qb, kvb, gb, gm = jnp.split(proj, offs, axis=-1)
    qa = rms_norm(qa.reshape(B, T, DSA_HEADS, HEAD_DIM), gq_a)
    ka = rms_norm(ka.reshape(B, T, DSA_KV_HEADS, HEAD_DIM), gk_a)
    va = va.reshape(B, T, DSA_KV_HEADS, HEAD_DIM)
    y_a = dsa_mixer(qa, ka, va, qi.reshape(B, T, IDX_HEADS, IDX_DIM), ki, wi, table[:, :DSA_HEADS])
    qb = rms_norm(qb.reshape(B, T, NSA_HEADS, HEAD_DIM), gq_b)
    kvb = kvb.reshape(B, T, 6, NSA_KV_HEADS, HEAD_DIM)
    k_cmp = rms_norm(compress_blocks(kvb[:, :, 0], cmp_pe_k, cmp_w1_k, cmp_w2_k), gk_cmp)
    v_cmp = compress_blocks(kvb[:, :, 1], cmp_pe_v, cmp_w1_v, cmp_w2_v)
    y_b = nsa_mixer(qb, k_cmp, v_cmp, rms_norm(kvb[:, :, 2], gk_slc), kvb[:, :, 3],
                    rms_norm(kvb[:, :, 4], gk_win), kvb[:, :, 5], gb, table[:, DSA_HEADS:])
    g_a, g_b = jnp.split(jax.nn.sigmoid(gm), 2, axis=-1)
    merged = g_a * (y_a @ w_branch_a) + g_b * (y_b @ w_branch_b)
    return merged @ w_out


def peer_ffn(h, w_q, sub_keys, u, v):
    B, T, _ = h.shape
    q = (h @ w_q).reshape(B, T, PEER_HEADS, 2, PEER_QDIM // 2)
    s = jnp.einsum('bthcd,hcnd->bthcn', q, sub_keys).astype(jnp.float32)
    s1, i1 = lax.top_k(s[..., 0, :], PEER_TOPK)
    s2, i2 = lax.top_k(s[..., 1, :], PEER_TOPK)
    cand = (s1[..., :, None] + s2[..., None, :]).reshape(B, T, PEER_HEADS, PEER_TOPK * PEER_TOPK)
    cand_id = (i1[..., :, None] * PEER_NKEYS + i2[..., None, :]).reshape(B, T, PEER_HEADS, PEER_TOPK * PEER_TOPK)
    top_s, top_pos = lax.top_k(cand, PEER_TOPK)
    expert = jnp.take_along_axis(cand_id, top_pos, axis=-1).reshape(B, T, PEER_HEADS * PEER_TOPK)
    gate = jax.nn.softmax(top_s, axis=-1).reshape(B, T, PEER_HEADS * PEER_TOPK)

    def block(t0):
        h_b = lax.dynamic_slice_in_dim(h, t0, QBLK, axis=1)
        e_b = lax.dynamic_slice_in_dim(expert, t0, QBLK, axis=1)
        g_b = lax.dynamic_slice_in_dim(gate, t0, QBLK, axis=1)
        a = jnp.einsum('bqkd,bqd->bqk', u[e_b], h_b).astype(jnp.float32)
        act = (jax.nn.gelu(a) * g_b).astype(h.dtype)
        return jnp.einsum('bqk,bqkd->bqd', act, v[e_b])

    return map_blocks(block, T)


def setup_inputs(seed: int = 0) -> dict:
    key = jax.random.key(seed)
    ks = jax.random.split(key, 32)
    D, dh, L = D_MODEL, HEAD_DIM, DEPTH

    def nrm(k, shape, scale):
        return jax.random.normal(k, shape, jnp.float32) * scale

    return {
        'x': nrm(ks[0], (BATCH, SEQ, D), 1.0),
        'c': nrm(ks[1], (BATCH, D), 1.0),
        'rel_bias': nrm(ks[2], (REL_BUCKETS, N_ATTN_HEADS), 0.5),
        'w_ada': nrm(ks[3], (L, D, 6 * D), 0.5 * D ** -0.5),
        'b_ada': nrm(ks[4], (L, 6 * D), 0.02),
        'g_mix': 1.0 + nrm(ks[5], (L, D), 0.02),
        'w_in': nrm(ks[6], (L, D, IN_COLS), D ** -0.5),
        'gq_a': 1.0 + nrm(ks[7], (L, dh), 0.02),
        'gk_a': 1.0 + nrm(ks[8], (L, dh), 0.02),
        'gq_b': 1.0 + nrm(ks[9], (L, dh), 0.02),
        'gk_cmp': 1.0 + nrm(ks[10], (L, dh), 0.02),
        'gk_slc': 1.0 + nrm(ks[11], (L, dh), 0.02),
        'gk_win': 1.0 + nrm(ks[12], (L, dh), 0.02),
        'cmp_pe_k': nrm(ks[13], (L, CMP_LEN, dh), 0.1),
        'cmp_w1_k': nrm(ks[14], (L, CMP_LEN, dh, CMP_HIDDEN), (CMP_LEN * dh) ** -0.5),
        'cmp_w2_k': nrm(ks[15], (L, CMP_HIDDEN, dh), CMP_HIDDEN ** -0.5),
        'cmp_pe_v': nrm(ks[16], (L, CMP_LEN, dh), 0.1),
        'cmp_w1_v': nrm(ks[17], (L, CMP_LEN, dh, CMP_HIDDEN), (CMP_LEN * dh) ** -0.5),
        'cmp_w2_v': nrm(ks[18], (L, CMP_HIDDEN, dh), CMP_HIDDEN ** -0.5),
        'w_branch_a': nrm(ks[19], (L, DSA_HEADS * dh, D), (DSA_HEADS * dh) ** -0.5),
        'w_branch_b': nrm(ks[20], (L, NSA_HEADS * dh, D), (NSA_HEADS * dh) ** -0.5),
        'w_out': nrm(ks[21], (L, D, D), D ** -0.5),
        'g_ffn': 1.0 + nrm(ks[22], (L, D), 0.02),
        'w_peer_q': nrm(ks[23], (L, D, PEER_HEADS * PEER_QDIM), D ** -0.5),
        'peer_sub_keys': nrm(ks[24], (L, PEER_HEADS, 2, PEER_NKEYS, PEER_QDIM // 2), (PEER_QDIM // 2) ** -0.5),
        'peer_u': nrm(ks[25], (L, N_EXPERTS, D), D ** -0.5),
        'peer_v': nrm(ks[26], (L, N_EXPERTS, D), 0.2),
    }


def reference(x, c, rel_bias, w_ada, b_ada, g_mix, w_in, gq_a, gk_a, gq_b, gk_cmp, gk_slc, gk_win,
              cmp_pe_k, cmp_w1_k, cmp_w2_k, cmp_pe_v, cmp_w1_v, cmp_w2_v, w_branch_a, w_branch_b, w_out,
              g_ffn, w_peer_q, peer_sub_keys, peer_u, peer_v):
    for i in range(DEPTH):
        mod = (c @ w_ada[i] + b_ada[i])[:, None, :]
        sh1, sc1, gt1, sh2, sc2, gt2 = jnp.split(mod, 6, axis=-1)
        h = rms_norm(x, g_mix[i]) * (1 + sc1) + sh1
        x = x + gt1 * token_mixer(h, rel_bias, w_in[i], gq_a[i], gk_a[i], gq_b[i], gk_cmp[i], gk_slc[i], gk_win[i],
                                  cmp_pe_k[i], cmp_w1_k[i], cmp_w2_k[i], cmp_pe_v[i], cmp_w1_v[i], cmp_w2_v[i],
                                  w_branch_a[i], w_branch_b[i], w_out[i])
        h = rms_norm(x, g_ffn[i]) * (1 + sc2) + sh2
        x = x + gt2 * peer_ffn(h, w_peer_q[i], peer_sub_keys[i], peer_u[i], peer_v[i])
    return x
```

```python
import functools
import math

import numpy as np
import jax
import jax.numpy as jnp
from jax import lax
from jax.experimental import pallas as pl
from jax.experimental.pallas import tpu as pltpu

HEAD_DIM = 128
DSA_HEADS = 16
DSA_KV_HEADS = 4
IDX_HEADS = 32
IDX_DIM = 64
DSA_TOPK = 256
NSA_HEADS = 16
NSA_KV_HEADS = 4
CMP_LEN = 32
CMP_STRIDE = 16
CMP_HIDDEN = 256
SLC_LEN = 64
SLC_TOP = 16
WIN = 512
FORCE_SCORE = 1e9
PEER_HEADS = 8
PEER_NKEYS = 128
PEER_QDIM = 256
PEER_TOPK = 16
REL_BUCKETS = 32
REL_MAX_DIST = 2048
EPS = 1e-6

LANE = 128
VMEM_LIMIT = 56 * 1024 * 1024
NEG = -1e30
INT_MIN = -2 ** 31
KEY_NEG_INF = -2 ** 31 + 0x007FFFFF

OFF_QA, OFF_QI, OFF_QB, OFF_KVB, OFF_KA, OFF_VA, OFF_GM = 0, 2048, 4096, 6144, 9216, 9728, 10240

F32 = jnp.float32
BF16 = jnp.bfloat16


def _bucket_starts():
    n = np.arange(2 * REL_MAX_DIST)
    exact = REL_BUCKETS // 2
    nf = np.maximum(n, 1).astype(np.float32)
    lb = exact + (np.log(nf / np.float32(exact)) / np.float32(math.log(REL_MAX_DIST / exact))
                  * np.float32(REL_BUCKETS - exact)).astype(np.int32)
    bucket = np.where(n < exact, n, np.minimum(lb, REL_BUCKETS - 1))
    return [int(np.argmax(bucket >= b)) for b in range(REL_BUCKETS)]


BUCKET_STARTS = _bucket_starts()


def _cparams(sem):
    return pltpu.CompilerParams(dimension_semantics=sem, vmem_limit_bytes=VMEM_LIMIT)


def _pick(n, pref):
    t = pref
    while n % t:
        t //= 2
    return t


def _to_key(x):
    b = lax.bitcast_convert_type(x, jnp.int32)
    return b ^ ((b >> 31) & 0x7FFFFFFF)


def _from_key(k):
    return lax.bitcast_convert_type(k ^ ((k >> 31) & 0x7FFFFFFF), F32)


def _kth_key(count_ge, rows, kk):
    def bit_pass(b, x):
        cand = x + lax.shift_left(jnp.int32(1), 31 - b)
        return jnp.where(count_ge(cand) >= kk, cand, x)
    x = lax.fori_loop(0, 32, bit_pass, jnp.full((rows, 1), INT_MIN, jnp.int32))
    return jnp.maximum(x, KEY_NEG_INF)


def _kth_key_val(keys, kk):
    def count_ge(cand):
        return jnp.sum(jnp.where(keys >= cand, 1.0, 0.0), axis=-1, keepdims=True)
    return _kth_key(count_ge, keys.shape[0], float(kk))


def _gelu(x):
    return 0.5 * x * (1.0 + jnp.tanh(math.sqrt(2.0 / math.pi) * (x + 0.044715 * (x * x * x))))


def _dot_nt(a, b):
    return lax.dot_general(a, b, (((1,), (1,)), ((), ())), preferred_element_type=F32)


def _head_norm(x, g, scale):
    ms = jnp.mean(x * x, axis=-1, keepdims=True)
    return x * lax.rsqrt(ms + EPS) * g * scale


def _mm_kernel(a_ref, b_ref, o_ref):
    o_ref[...] = jnp.dot(a_ref[...], b_ref[...], preferred_element_type=F32).astype(o_ref.dtype)


def _matmul(a, b, out_dtype=F32, tm=1024, tn=512, name="matmul"):
    m, k = a.shape
    n = b.shape[1]
    tm, tn = _pick(m, tm), _pick(n, tn)
    return pl.pallas_call(
        _mm_kernel, grid=(m // tm, n // tn),
        in_specs=[pl.BlockSpec((tm, k), lambda i, j: (i, 0)), pl.BlockSpec((k, tn), lambda i, j: (0, j))],
        out_specs=pl.BlockSpec((tm, tn), lambda i, j: (i, j)),
        out_shape=jax.ShapeDtypeStruct((m, n), out_dtype),
        compiler_params=_cparams(("parallel", "arbitrary")), name=name)(a, b)


def _ada_kernel(c_ref, w_ref, b_ref, o_ref):
    o_ref[...] = jnp.dot(c_ref[...], w_ref[...].astype(BF16), preferred_element_type=F32) + b_ref[...]


def _ada(c, w, b):
    d, n = w.shape
    tn = _pick(n, 512)
    c8 = jnp.broadcast_to(c.astype(BF16), (8, d))
    out = pl.pallas_call(
        _ada_kernel, grid=(n // tn,),
        in_specs=[pl.BlockSpec((8, d), lambda j: (0, 0)), pl.BlockSpec((d, tn), lambda j: (0, j)),
                  pl.BlockSpec((1, tn), lambda j: (0, j))],
        out_specs=pl.BlockSpec((8, tn), lambda j: (0, j)),
        out_shape=jax.ShapeDtypeStruct((8, n), F32),
        compiler_params=_cparams(("arbitrary",)), name="ada")(c8, w, b.reshape(1, n))
    return out[0:1]


def _normmod_kernel(x_ref, g_ref, sc_ref, sh_ref, o_ref):
    x = x_ref[...]
    y = x * lax.rsqrt(jnp.mean(x * x, axis=-1, keepdims=True) + EPS) * g_ref[...]
    o_ref[...] = (y * (1.0 + sc_ref[...]) + sh_ref[...]).astype(o_ref.dtype)


def _normmod(x, g, sc, sh):
    t, d = x.shape
    tm = _pick(t, 256)
    row = pl.BlockSpec((1, d), lambda i: (0, 0))
    return pl.pallas_call(
        _normmod_kernel, grid=(t // tm,),
        in_specs=[pl.BlockSpec((tm, d), lambda i: (i, 0)), row, row, row],
        out_specs=pl.BlockSpec((tm, d), lambda i: (i, 0)),
        out_shape=jax.ShapeDtypeStruct((t, d), BF16),
        compiler_params=_cparams(("parallel",)), name="normmod")(x, g.reshape(1, d), sc, sh)


def _resid_normmod_kernel(x_ref, o_ref, gt_ref, g_ref, sc_ref, sh_ref, x1_ref, h_ref):
    x1 = x_ref[...] + gt_ref[...] * o_ref[...]
    x1_ref[...] = x1
    y = x1 * lax.rsqrt(jnp.mean(x1 * x1, axis=-1, keepdims=True) + EPS) * g_ref[...]
    h_ref[...] = (y * (1.0 + sc_ref[...]) + sh_ref[...]).astype(h_ref.dtype)


def _resid_normmod(x, o, gt, g, sc, sh):
    t, d = x.shape
    tm = _pick(t, 256)
    row = pl.BlockSpec((1, d), lambda i: (0, 0))
    blk = pl.BlockSpec((tm, d), lambda i: (i, 0))
    return pl.pallas_call(
        _resid_normmod_kernel, grid=(t // tm,),
        in_specs=[blk, blk, row, row, row, row],
        out_specs=[blk, blk],
        out_shape=[jax.ShapeDtypeStruct((t, d), F32), jax.ShapeDtypeStruct((t, d), BF16)],
        compiler_params=_cparams(("parallel",)), name="resid_normmod")(x, o, gt, g.reshape(1, d), sc, sh)


def _resid_kernel(x_ref, o_ref, gt_ref, y_ref):
    y_ref[...] = x_ref[...] + gt_ref[...] * o_ref[...]


def _resid(x, o, gt):
    t, d = x.shape
    tm = _pick(t, 256)
    blk = pl.BlockSpec((tm, d), lambda i: (i, 0))
    return pl.pallas_call(
        _resid_kernel, grid=(t // tm,),
        in_specs=[blk, blk, pl.BlockSpec((1, d), lambda i: (0, 0))],
        out_specs=blk, out_shape=jax.ShapeDtypeStruct((t, d), F32),
        compiler_params=_cparams(("parallel",)), name="resid")(x, o, gt)


def _prep_kernel(qa_ref, qi_ref, qb_ref, kvb_ref, ka_ref, va_ref, tail_ref,
                 gqa_ref, gka_ref, gqb_ref, gslc_ref, gwin_ref,
                 qa_o, qi_o, qb_o, kvb_o, ka_o, va_o, ki_o, wi_o, gs_o):
    qscale = HEAD_DIM ** -0.5
    for h in range(DSA_HEADS):
        qa_o[h] = _head_norm(qa_ref[:, h * 128:(h + 1) * 128], gqa_ref[...], qscale).astype(BF16)
    for h in range(NSA_HEADS):
        qb_o[h] = _head_norm(qb_ref[:, h * 128:(h + 1) * 128], gqb_ref[...], qscale).astype(BF16)
    for h in range(IDX_HEADS):
        qi_o[h] = (qi_ref[:, h * IDX_DIM:(h + 1) * IDX_DIM] * IDX_DIM ** -0.5).astype(BF16)
    for g in range(DSA_KV_HEADS):
        ka_o[g] = _head_norm(ka_ref[:, g * 128:(g + 1) * 128], gka_ref[...], 1.0).astype(BF16)
        va_o[g] = va_ref[:, g * 128:(g + 1) * 128].astype(BF16)
    for s in range(6 * NSA_KV_HEADS):
        xs = kvb_ref[:, s * 128:(s + 1) * 128]
        part = s // NSA_KV_HEADS
        if part == 2:
            xs = _head_norm(xs, gslc_ref[...], 1.0)
        elif part == 4:
            xs = _head_norm(xs, gwin_ref[...], 1.0)
        kvb_o[s] = xs.astype(BF16)
    tail = tail_ref[...]
    ki_o[...] = tail[:, 0:IDX_DIM].astype(BF16)
    wi_o[...] = tail[:, 0:128] * IDX_HEADS ** -0.5
    gs_o[...] = jax.nn.sigmoid(tail[:, 128:256])


def _prep(pm, pt, gq_a, gk_a, gq_b, gk_slc, gk_win):
    t = pm.shape[0]
    tm = _pick(t, 128)
    g128 = pl.BlockSpec((1, 128), lambda i: (0, 0))

    def col(w, off):
        assert off % w == 0
        return pl.BlockSpec((tm, w), lambda i: (i, off // w))

    def hm(n, w):
        return pl.BlockSpec((n, tm, w), lambda i: (0, i, 0))

    row = lambda w: pl.BlockSpec((tm, w), lambda i: (i, 0))
    outs = pl.pallas_call(
        _prep_kernel, grid=(t // tm,),
        in_specs=[col(2048, OFF_QA), col(2048, OFF_QI), col(2048, OFF_QB), col(3072, OFF_KVB),
                  col(512, OFF_KA), col(512, OFF_VA), row(256), g128, g128, g128, g128, g128],
        out_specs=[hm(DSA_HEADS, 128), hm(IDX_HEADS, IDX_DIM), hm(NSA_HEADS, 128), hm(24, 128),
                   hm(DSA_KV_HEADS, 128), hm(DSA_KV_HEADS, 128), row(IDX_DIM), row(128), row(128)],
        out_shape=[jax.ShapeDtypeStruct((DSA_HEADS, t, 128), BF16),
                   jax.ShapeDtypeStruct((IDX_HEADS, t, IDX_DIM), BF16),
                   jax.ShapeDtypeStruct((NSA_HEADS, t, 128), BF16),
                   jax.ShapeDtypeStruct((24, t, 128), BF16),
                   jax.ShapeDtypeStruct((DSA_KV_HEADS, t, 128), BF16),
                   jax.ShapeDtypeStruct((DSA_KV_HEADS, t, 128), BF16),
                   jax.ShapeDtypeStruct((t, IDX_DIM), BF16),
                   jax.ShapeDtypeStruct((t, 128), F32),
                   jax.ShapeDtypeStruct((t, 128), F32)],
        compiler_params=_cparams(("parallel",)), name="prep")(
            pm, pm, pm, pm, pm, pm, pt,
            gq_a.reshape(1, 128), gk_a.reshape(1, 128), gq_b.reshape(1, 128),
            gk_slc.reshape(1, 128), gk_win.reshape(1, 128))
    return outs


def _bias_from_dist(dist, tbl_ref, h, nheads):
    val = jnp.full(dist.shape, tbl_ref[h], F32)
    for b in range(1, REL_BUCKETS):
        val = jnp.where(dist >= BUCKET_STARTS[b], tbl_ref[b * nheads + h], val)
    return val


def _bias_tiles_kernel(tbl_ref, o_ref, *, tq, tk, nheads):
    h, d = pl.program_id(0), pl.program_id(1)
    dist = d * tq + lax.broadcasted_iota(jnp.int32, (tq, tk), 0) - lax.broadcasted_iota(jnp.int32, (tq, tk), 1)
    o_ref[0, 0] = _bias_from_dist(dist, tbl_ref, h, nheads)


def _n_near(tq, tk):
    return -(-(BUCKET_STARTS[REL_BUCKETS - 1] + tk - 1) // tq)


def _bias_tiles(table, tq, tk):
    nheads = table.shape[1]
    nd = _n_near(tq, tk) + 1
    return pl.pallas_call(
        functools.partial(_bias_tiles_kernel, tq=tq, tk=tk, nheads=nheads), grid=(nheads, nd),
        in_specs=[pl.BlockSpec(memory_space=pltpu.SMEM)],
        out_specs=pl.BlockSpec((1, 1, tq, tk), lambda h, d: (h, d, 0, 0)),
        out_shape=jax.ShapeDtypeStruct((nheads, nd, tq, tk), F32),
        compiler_params=_cparams(("parallel", "parallel")), name="bias_tiles")(table.reshape(-1))


def _idx_kernel(q_ref, kt_ref, w_ref, key_ref, thr_ref, s_ref, wb_ref, *, tq, tk, t, kk):
    i = pl.program_id(0)
    nkt = (i * tq + tq + tk - 1) // tk
    w = w_ref[...]
    for h in range(IDX_HEADS):
        wb_ref[h] = jnp.broadcast_to(w[:, IDX_DIM + h:IDX_DIM + h + 1], (tq, LANE))
    qall = q_ref[...].reshape(IDX_HEADS * tq, IDX_DIM)
    rows = lax.broadcasted_iota(jnp.int32, (tq, tk), 0)
    cols = lax.broadcasted_iota(jnp.int32, (tq, tk), 1)

    def tile(j, carry):
        off = pl.multiple_of(j * tk, tk)
        s_ref[...] = jnp.dot(qall, kt_ref[:, pl.ds(off, tk)], preferred_element_type=F32)
        acc = jnp.zeros((tq, tk), F32)
        for h in range(IDX_HEADS):
            wbh = jnp.concatenate([wb_ref[h]] * (tk // LANE), axis=1)
            acc = acc + jnp.maximum(s_ref[h * tq:(h + 1) * tq, :], 0.0) * wbh
        acc = jnp.where(off + cols <= i * tq + rows, acc, -jnp.inf)
        key_ref[:, pl.ds(off, tk)] = _to_key(acc)
        return carry

    lax.fori_loop(0, nkt, tile, 0)

    def fill(j, carry):
        key_ref[:, pl.ds(pl.multiple_of(j * tk, tk), tk)] = jnp.full((tq, tk), KEY_NEG_INF, jnp.int32)
        return carry

    lax.fori_loop(nkt, t // tk, fill, 0)

    def count_ge(cand):
        def body(j, c):
            kt = key_ref[:, pl.ds(pl.multiple_of(j * tk, tk), tk)]
            ge = jnp.where(kt >= cand, 1.0, 0.0)
            for a in range(tk // LANE):
                c = c + ge[:, a * LANE:(a + 1) * LANE]
            return c
        c = lax.fori_loop(0, nkt, body, jnp.zeros((tq, LANE), F32))
        return jnp.sum(c, axis=-1, keepdims=True)

    thr_ref[...] = _kth_key(count_ge, tq, float(kk))


def _dsa_index(qi_h, ki_t, wi, kk):
    t = ki_t.shape[1]
    tq, tk = _pick(t, 128), _pick(t, 256)
    return pl.pallas_call(
        functools.partial(_idx_kernel, tq=tq, tk=tk, t=t, kk=kk), grid=(t // tq,),
        in_specs=[pl.BlockSpec((IDX_HEADS, tq, IDX_DIM), lambda i: (0, i, 0)),
                  pl.BlockSpec((IDX_DIM, t), lambda i: (0, 0)),
                  pl.BlockSpec((tq, 128), lambda i: (i, 0))],
        out_specs=[pl.BlockSpec((tq, t), lambda i: (i, 0)), pl.BlockSpec((tq, 1), lambda i: (i, 0))],
        out_shape=[jax.ShapeDtypeStruct((t, t), jnp.int32), jax.ShapeDtypeStruct((t, 1), jnp.int32)],
        scratch_shapes=[pltpu.VMEM((IDX_HEADS * tq, tk), F32), pltpu.VMEM((IDX_HEADS, tq, LANE), F32)],
        compiler_params=_cparams(("parallel",)), name="dsa_index")(qi_h, ki_t, wi)


def _flash_kernel(*refs, mode, tq, tk, nheads, ngroups, nd):
    if mode == "dsa":
        q_ref, k_ref, v_ref, b_ref, key_ref, thr_ref, o_ref, m_ref, l_ref, acc_ref = refs
    elif mode == "slc":
        q_ref, k_ref, v_ref, b_ref, sel_ref, e_ref, o_ref, m_ref, l_ref, acc_ref = refs
    else:
        q_ref, k_ref, v_ref, b_ref, o_ref, m_ref, l_ref, acc_ref = refs
    rep = nheads // ngroups
    i, s = pl.program_id(0), pl.program_id(1)
    j = _kv_tile(mode, i, s, tq, tk)

    @pl.when(s == 0)
    def _():
        m_ref[...] = jnp.full(m_ref.shape, NEG, F32)
        l_ref[...] = jnp.zeros(l_ref.shape, F32)
        acc_ref[...] = jnp.zeros(acc_ref.shape, F32)

    @pl.when(j <= i)
    def _():
        dist = ((i * tq - j * tk) + lax.broadcasted_iota(jnp.int32, (tq, tk), 0)
                - lax.broadcasted_iota(jnp.int32, (tq, tk), 1))
        mask = dist >= 0
        if mode == "dsa":
            mask = mask & (key_ref[...] >= thr_ref[...])
        elif mode == "win":
            mask = mask & (dist < WIN)
        for g in range(ngroups):
            qg = q_ref[g * rep:(g + 1) * rep].reshape(rep * tq, HEAD_DIM)
            s_all = _dot_nt(qg, k_ref[g])
            mask_g = mask
            if mode == "slc":
                mask_g = mask & (jnp.dot(sel_ref[g], e_ref[...], preferred_element_type=F32) > 0.5)
            ps = []
            for r in range(rep):
                h = g * rep + r
                sh = jnp.where(mask_g, s_all[r * tq:(r + 1) * tq] + b_ref[h, 0], NEG)
                m_old = m_ref[h]
                m_new = jnp.maximum(m_old, jnp.max(sh, axis=-1, keepdims=True))
                alpha = jnp.exp(m_old - m_new)
                p = jnp.exp(sh - m_new)
                l_ref[h] = alpha * l_ref[h] + jnp.sum(p, axis=-1, keepdims=True)
                m_ref[h] = m_new
                acc_ref[h] = acc_ref[h] * alpha
                ps.append(p.astype(BF16))
            pv = jnp.dot(jnp.concatenate(ps, axis=0), v_ref[g], preferred_element_type=F32)
            for r in range(rep):
                h = g * rep + r
                acc_ref[h] = acc_ref[h] + pv[r * tq:(r + 1) * tq]

    @pl.when(j == i)
    def _():
        for h in range(nheads):
            o_ref[:, h * HEAD_DIM:(h + 1) * HEAD_DIM] = (acc_ref[h] / l_ref[h]).astype(o_ref.dtype)


def _win_tiles(tk):
    return -(-(WIN - 1) // tk)


def _kv_tile(mode, i, s, tq, tk):
    if mode == "win":
        return jnp.maximum(i - _win_tiles(tk), 0) + s
    return s


def _flash(mode, q_h, k_h, v_h, bias, extra, out_dtype):
    nheads, t, _ = q_h.shape
    ngroups = k_h.shape[0]
    tq = tk = _pick(t, 256)
    nq = t // tq
    nd = bias.shape[1] - 1
    nsteps = nq if mode != "win" else min(nq, _win_tiles(tk) + 1)

    def jmap(i, s):
        return jnp.minimum(_kv_tile(mode, i, s, tq, tk), i)

    in_specs = [pl.BlockSpec((nheads, tq, HEAD_DIM), lambda i, s: (0, i, 0)),
                pl.BlockSpec((ngroups, tk, HEAD_DIM), lambda i, s: (0, jmap(i, s), 0)),
                pl.BlockSpec((ngroups, tk, HEAD_DIM), lambda i, s: (0, jmap(i, s), 0)),
                pl.BlockSpec((nheads, 1, tq, tk), lambda i, s: (0, jnp.minimum(i - jmap(i, s), nd), 0, 0))]
    if mode == "dsa":
        in_specs += [pl.BlockSpec((tq, tk), lambda i, s: (i, jmap(i, s))),
                     pl.BlockSpec((tq, 1), lambda i, s: (i, 0))]
    elif mode == "slc":
        ns = extra[0].shape[2]
        in_specs += [pl.BlockSpec((ngroups, tq, ns), lambda i, s: (0, i, 0)),
                     pl.BlockSpec((ns, tk), lambda i, s: (0, jmap(i, s)))]
    return pl.pallas_call(
        functools.partial(_flash_kernel, mode=mode, tq=tq, tk=tk, nheads=nheads, ngroups=ngroups, nd=nd),
        grid=(nq, nsteps), in_specs=in_specs,
        out_specs=pl.BlockSpec((tq, nheads * HEAD_DIM), lambda i, s: (i, 0)),
        out_shape=jax.ShapeDtypeStruct((t, nheads * HEAD_DIM), out_dtype),
        scratch_shapes=[pltpu.VMEM((nheads, tq, 1), F32), pltpu.VMEM((nheads, tq, 1), F32),
                        pltpu.VMEM((nheads, tq, HEAD_DIM), F32)],
        compiler_params=_cparams(("parallel", "arbitrary")), name="flash_" + mode)(q_h, k_h, v_h, bias, *extra)


def _compress_kernel(c_ref, w1_ref, pe_ref, w2_ref, g_ref, o_ref, *, norm):
    c = c_ref[0]
    half = CMP_STRIDE * HEAD_DIM
    ncp = c.shape[0]
    a = jnp.dot(c, w1_ref[0:half], preferred_element_type=F32)
    b = jnp.dot(c, w1_ref[half:2 * half], preferred_element_type=F32)
    pet = jnp.dot(pe_ref[...], w1_ref[...], preferred_element_type=F32)[0:1]
    hid = _gelu(a + pltpu.roll(b, ncp - 1, 0) + pet)
    y = jnp.dot(hid.astype(BF16), w2_ref[...], preferred_element_type=F32)
    if norm:
        y = _head_norm(y, g_ref[...], 1.0)
    o_ref[0] = y.astype(BF16)


def _compress(slabs, base, pe, w1, w2, gain, norm):
    _, t, _ = slabs.shape
    ncp = t // CMP_STRIDE
    width = CMP_STRIDE * HEAD_DIM
    chunks = slabs.reshape(slabs.shape[0], ncp, width)
    w1f = w1.reshape(CMP_LEN * HEAD_DIM, CMP_HIDDEN).astype(BF16)
    pe8 = jnp.broadcast_to(pe.reshape(1, CMP_LEN * HEAD_DIM).astype(BF16), (8, CMP_LEN * HEAD_DIM))
    full = lambda shp: pl.BlockSpec(shp, lambda g: (0,) * len(shp))
    return pl.pallas_call(
        functools.partial(_compress_kernel, norm=norm), grid=(NSA_KV_HEADS,),
        in_specs=[pl.BlockSpec((1, ncp, width), lambda g: (base + g, 0, 0)),
                  full((CMP_LEN * HEAD_DIM, CMP_HIDDEN)), full((8, CMP_LEN * HEAD_DIM)),
                  full((CMP_HIDDEN, HEAD_DIM)), full((1, HEAD_DIM))],
        out_specs=pl.BlockSpec((1, ncp, HEAD_DIM), lambda g: (g, 0, 0)),
        out_shape=jax.ShapeDtypeStruct((NSA_KV_HEADS, ncp, HEAD_DIM), BF16),
        compiler_params=_cparams(("parallel",)), name="compress")(
            chunks, w1f, pe8, w2.astype(BF16), gain.reshape(1, HEAD_DIM))


def _cmp_attn_kernel(tbl_ref, q_ref, kc_ref, vc_ref, ov_ref, oc_ref, sel_ref, *, tq, ncp, ns, nsel):
    i = pl.program_id(0)
    rep = NSA_HEADS // NSA_KV_HEADS
    qpos = i * tq + lax.broadcasted_iota(jnp.int32, (tq, ncp), 0)
    dist = qpos - (lax.broadcasted_iota(jnp.int32, (tq, ncp), 1) * CMP_STRIDE + CMP_LEN - 1)
    mask = dist >= 0
    blk = lax.broadcasted_iota(jnp.int32, (tq, ns), 1)
    qp = i * tq + lax.broadcasted_iota(jnp.int32, (tq, ns), 0)
    cur = qp // SLC_LEN
    forced = (blk == 0) | (blk == cur) | (blk == cur - 1)
    admissible = blk * SLC_LEN <= qp
    for g in range(NSA_KV_HEADS):
        s_all = _dot_nt(q_ref[g * rep:(g + 1) * rep].reshape(rep * tq, HEAD_DIM), kc_ref[g])
        imp = jnp.zeros((tq, ncp), F32)
        ps = []
        for r in range(rep):
            h = g * rep + r
            sh = jnp.where(mask, s_all[r * tq:(r + 1) * tq] + _bias_from_dist(dist, tbl_ref, h, NSA_HEADS), NEG)
            m = jnp.max(sh, axis=-1, keepdims=True)
            p = jnp.where(mask, jnp.exp(sh - m), 0.0)
            pc = p / jnp.maximum(jnp.sum(p, axis=-1, keepdims=True), 1e-30)
            imp = imp + pc
            ps.append(pc.astype(BF16))
        o = jnp.dot(jnp.concatenate(ps, axis=0), vc_ref[g], preferred_element_type=F32)
        for r in range(rep):
            h = g * rep + r
            oc_ref[:, h * HEAD_DIM:(h + 1) * HEAD_DIM] = o[r * tq:(r + 1) * tq]
        hi = imp.astype(BF16)
        lo = (imp - hi.astype(F32)).astype(BF16)
        impb = (jnp.dot(hi, ov_ref[...], preferred_element_type=F32)
                + jnp.dot(lo, ov_ref[...], preferred_element_type=F32))
        impb = jnp.where(forced, FORCE_SCORE, impb)
        impb = jnp.where(admissible, impb, -jnp.inf)
        keys = _to_key(impb)
        thr = _kth_key_val(keys, nsel)
        sel_ref[g] = jnp.where(keys >= thr, 1.0, 0.0).astype(BF16)


def _cmp_attn(table, q_h, kc, vc):
    nheads, t, _ = q_h.shape
    ncp = kc.shape[1]
    ns = t // SLC_LEN
    nsel = min(SLC_TOP, ns)
    tq = _pick(t, 128)
    c_start = np.arange(ncp)[:, None] * CMP_STRIDE
    s_start = np.arange(ns)[None, :] * SLC_LEN
    overlap = jnp.asarray((c_start < s_start + SLC_LEN) & (c_start + CMP_LEN > s_start), BF16)
    full = lambda shp: pl.BlockSpec(shp, lambda i: (0,) * len(shp))
    return pl.pallas_call(
        functools.partial(_cmp_attn_kernel, tq=tq, ncp=ncp, ns=ns, nsel=nsel), grid=(t // tq,),
        in_specs=[pl.BlockSpec(memory_space=pltpu.SMEM),
                  pl.BlockSpec((nheads, tq, HEAD_DIM), lambda i: (0, i, 0)),
                  full((NSA_KV_HEADS, ncp, HEAD_DIM)), full((NSA_KV_HEADS, ncp, HEAD_DIM)), full((ncp, ns))],
        out_specs=[pl.BlockSpec((tq, nheads * HEAD_DIM), lambda i: (i, 0)),
                   pl.BlockSpec((NSA_KV_HEADS, tq, ns), lambda i: (0, i, 0))],
        out_shape=[jax.ShapeDtypeStruct((t, nheads * HEAD_DIM), F32),
                   jax.ShapeDtypeStruct((NSA_KV_HEADS, t, ns), BF16)],
        compiler_params=_cparams(("parallel",)), name="cmp_attn")(table.reshape(-1), q_h, kc, vc, overlap)


def _combine_kernel(oc_ref, os_ref, ow_ref, g_ref, y_ref):
    g = g_ref[...]
    for h in range(NSA_HEADS):
        sl = slice(h * HEAD_DIM, (h + 1) * HEAD_DIM)
        y = (g[:, 3 * h:3 * h + 1] * oc_ref[:, sl] + g[:, 3 * h + 1:3 * h + 2] * os_ref[:, sl]
             + g[:, 3 * h + 2:3 * h + 3] * ow_ref[:, sl])
        y_ref[:, sl] = y.astype(y_ref.dtype)


def _combine(oc, os_, ow, gsig):
    t, w = oc.shape
    tm = _pick(t, 256)
    blk = pl.BlockSpec((tm, w), lambda i: (i, 0))
    return pl.pallas_call(
        _combine_kernel, grid=(t // tm,),
        in_specs=[blk, blk, blk, pl.BlockSpec((tm, 128), lambda i: (i, 0))],
        out_specs=blk, out_shape=jax.ShapeDtypeStruct((t, w), BF16),
        compiler_params=_cparams(("parallel",)), name="combine")(oc, os_, ow, gsig)


def _merge_kernel(a_ref, b_ref, ga_ref, gb_ref, o_ref):
    o_ref[...] = (jax.nn.sigmoid(ga_ref[...]) * a_ref[...]
                  + jax.nn.sigmoid(gb_ref[...]) * b_ref[...]).astype(o_ref.dtype)


def _merge(a, b, pm):
    t, d = a.shape
    tm, tn = _pick(t, 256), _pick(d, 512)
    assert OFF_GM % tn == 0
    blk = pl.BlockSpec((tm, tn), lambda i, j: (i, j))
    return pl.pallas_call(
        _merge_kernel, grid=(t // tm, d // tn),
        in_specs=[blk, blk, pl.BlockSpec((tm, tn), lambda i, j: (i, OFF_GM // tn + j)),
                  pl.BlockSpec((tm, tn), lambda i, j: (i, (OFF_GM + d) // tn + j))],
        out_specs=blk, out_shape=jax.ShapeDtypeStruct((t, d), BF16),
        compiler_params=_cparams(("parallel", "parallel")), name="merge")(a, b, pm, pm)


def _peer_score_kernel(q_ref, sk_ref, s1_o, e1_o, s2_o, e2_o, thr_o, *, tm):
    lane = lax.broadcasted_iota(jnp.int32, (tm, PEER_NKEYS), 1).astype(F32)
    ncand = PEER_TOPK * PEER_TOPK
    lane_c = lax.broadcasted_iota(jnp.int32, (tm, ncand), 1)
    for h in range(PEER_HEADS):
        svals, tops = [], []
        for c in range(2):
            hc = 2 * h + c
            s = _dot_nt(q_ref[:, hc * 128:(hc + 1) * 128].astype(BF16), sk_ref[hc])
            svals.append(s)
            work, top = s, []
            for _ in range(PEER_TOPK):
                mx = jnp.max(work, axis=-1, keepdims=True)
                first = jnp.min(jnp.where(work == mx, lane, float(PEER_NKEYS)), axis=-1, keepdims=True)
                work = jnp.where(lane == first, -jnp.inf, work)
                top.append(mx)
            tops.append(top)
        a1 = jnp.zeros((tm, ncand), F32)
        a2 = jnp.zeros((tm, ncand), F32)
        for k in range(PEER_TOPK):
            a1 = jnp.where(lane_c // PEER_TOPK == k, tops[0][k], a1)
            a2 = jnp.where(lane_c % PEER_TOPK == k, tops[1][k], a2)
        cand = a1 + a2
        thr = _from_key(_kth_key_val(_to_key(cand), PEER_TOPK))
        m1, m2 = tops[0][0], tops[1][0]
        ec = jnp.exp(a1 - m1) * jnp.exp(a2 - m2)
        z = jnp.sum(jnp.where(cand >= thr, ec, 0.0), axis=-1, keepdims=True)
        s1_o[h] = svals[0].T
        s2_o[h] = svals[1].T
        e1_o[h] = jnp.exp(svals[0] - m1).T
        e2_o[h] = (jnp.exp(svals[1] - m2) / z).T
        thr_o[h:h + 1, :] = jnp.broadcast_to(thr, (tm, LANE)).T[0:1, :]


def _peer_score(qp, sub_keys):
    t = qp.shape[0]
    tm = _pick(t, 128)
    sk = sub_keys.reshape(2 * PEER_HEADS, PEER_NKEYS, PEER_QDIM // 2).astype(BF16)
    tr = pl.BlockSpec((PEER_HEADS, PEER_NKEYS, tm), lambda i: (0, 0, i))
    shp = jax.ShapeDtypeStruct((PEER_HEADS, PEER_NKEYS, t), F32)
    return pl.pallas_call(
        functools.partial(_peer_score_kernel, tm=tm), grid=(t // tm,),
        in_specs=[pl.BlockSpec((tm, 2 * PEER_HEADS * 128), lambda i: (i, 0)),
                  pl.BlockSpec((2 * PEER_HEADS, PEER_NKEYS, PEER_QDIM // 2), lambda i: (0, 0, 0))],
        out_specs=[tr, tr, tr, tr, pl.BlockSpec((PEER_HEADS, tm), lambda i: (0, i))],
        out_shape=[shp, shp, shp, shp, jax.ShapeDtypeStruct((PEER_HEADS, t), F32)],
        compiler_params=_cparams(("parallel",)), name="peer_score")(qp, sk)


def _peer_dense_kernel(h_ref, u_ref, v_ref, s1_ref, e1_ref, s2_ref, e2_ref, thr_ref, o_ref, *, tm, te):
    j = pl.program_id(1)

    @pl.when(j == 0)
    def _():
        o_ref[...] = jnp.zeros(o_ref.shape, F32)

    a = _dot_nt(h_ref[...], u_ref[...])
    n1 = te // PEER_NKEYS
    parts = []
    for ai in range(n1):
        i1 = j * n1 + ai
        wt = jnp.zeros((PEER_NKEYS, tm), F32)
        for h in range(PEER_HEADS):
            s1row = s1_ref[h, pl.ds(i1, 1), :]
            e1row = e1_ref[h, pl.ds(i1, 1), :]
            hit = (s1row + s2_ref[h]) >= thr_ref[h:h + 1, :]
            wt = wt + jnp.where(hit, e1row * e2_ref[h], 0.0)
        parts.append(wt)
    w = jnp.concatenate(parts, axis=0).T
    act = (_gelu(a) * w).astype(BF16)
    o_ref[...] += jnp.dot(act, v_ref[...], preferred_element_type=F32)


def _peer_dense(h2, u, v, s1, e1, s2, e2, thr):
    t, d = h2.shape
    ne = u.shape[0]
    tm, te = _pick(t, 512), 256
    once = dict(pipeline_mode=pl.Buffered(1))
    tok = pl.BlockSpec((PEER_HEADS, PEER_NKEYS, tm), lambda i, j: (0, 0, i), **once)
    return pl.pallas_call(
        functools.partial(_peer_dense_kernel, tm=tm, te=te), grid=(t // tm, ne // te),
        in_specs=[pl.BlockSpec((tm, d), lambda i, j: (i, 0), **once),
                  pl.BlockSpec((te, d), lambda i, j: (j, 0)), pl.BlockSpec((te, d), lambda i, j: (j, 0)),
                  tok, tok, tok, tok, pl.BlockSpec((PEER_HEADS, tm), lambda i, j: (0, i), **once)],
        out_specs=pl.BlockSpec((tm, d), lambda i, j: (i, 0)),
        out_shape=jax.ShapeDtypeStruct((t, d), F32),
        compiler_params=_cparams(("parallel", "arbitrary")), name="peer_dense")(h2, u, v, s1, e1, s2, e2, thr)


def _pack_w_in(w_in, d):
    offs = np.cumsum([0, DSA_HEADS * 128, DSA_KV_HEADS * 128, DSA_KV_HEADS * 128, IDX_HEADS * IDX_DIM, IDX_DIM,
                      IDX_HEADS, NSA_HEADS * 128, 6 * NSA_KV_HEADS * 128, 3 * NSA_HEADS, 2 * d])
    qa, ka, va, qi, ki, wi, qb, kvb, gb, gm = [w_in[:, int(offs[n]):int(offs[n + 1])] for n in range(10)]
    main = jnp.concatenate([qa, qi, qb, kvb, ka, va, gm], axis=1).astype(BF16)
    pad = lambda n: jnp.zeros((d, n), w_in.dtype)
    tail = jnp.concatenate([ki, wi, pad(128 - IDX_DIM - IDX_HEADS), gb, pad(128 - 3 * NSA_HEADS)], axis=1).astype(BF16)
    return main, tail


def _token_mixer(h, table, w_in, gq_a, gk_a, gq_b, gk_cmp, gk_slc, gk_win, cmp_pe_k, cmp_w1_k, cmp_w2_k,
                 cmp_pe_v, cmp_w1_v, cmp_w2_v, w_branch_a, w_branch_b, w_out):
    t, d = h.shape
    w_main, w_tail = _pack_w_in(w_in, d)
    pm = _matmul(h, w_main, name="proj_main")
    pt = _matmul(h, w_tail, name="proj_tail")
    qa_h, qi_h, qb_h, kvb_h, ka_h, va_h, ki, wi, gsig = _prep(pm, pt, gq_a, gk_a, gq_b, gk_slc, gk_win)
    tile = _pick(t, 256)
    keys, thr = _dsa_index(qi_h, ki.T, wi, min(DSA_TOPK, t // 4))
    bias_a = _bias_tiles(table[:, :DSA_HEADS], tile, tile)
    y_a = _flash("dsa", qa_h, ka_h, va_h, bias_a, (keys, thr), BF16)
    g = NSA_KV_HEADS
    kc = _compress(kvb_h, 0, cmp_pe_k, cmp_w1_k, cmp_w2_k, gk_cmp, True)
    vc = _compress(kvb_h, g, cmp_pe_v, cmp_w1_v, cmp_w2_v, gk_cmp, False)
    o_c, sel = _cmp_attn(table[:, DSA_HEADS:], qb_h, kc, vc)
    bias_b = _bias_tiles(table[:, DSA_HEADS:], tile, tile)
    ns = t // SLC_LEN
    expand = jnp.asarray(np.arange(ns)[:, None] == (np.arange(t)[None, :] // SLC_LEN), BF16)
    o_s = _flash("slc", qb_h, kvb_h[2 * g:3 * g], kvb_h[3 * g:4 * g], bias_b, (sel, expand), F32)
    o_w = _flash("win", qb_h, kvb_h[4 * g:5 * g], kvb_h[5 * g:6 * g], bias_b, (), F32)
    y_b = _combine(o_c, o_s, o_w, gsig)
    za = _matmul(y_a, w_branch_a.astype(BF16), name="branch_a")
    zb = _matmul(y_b, w_branch_b.astype(BF16), name="branch_b")
    merged = _merge(za, zb, pm)
    return _matmul(merged, w_out.astype(BF16), name="w_out")


def _peer_ffn(h2, w_q, sub_keys, u, v):
    qp = _matmul(h2, w_q.astype(BF16), name="peer_q")
    s1, e1, s2, e2, thr = _peer_score(qp, sub_keys)
    return _peer_dense(h2, u.astype(BF16), v.astype(BF16), s1, e1, s2, e2, thr)


def kernel(x, c, rel_bias, w_ada, b_ada, g_mix, w_in, gq_a, gk_a, gq_b, gk_cmp, gk_slc, gk_win, cmp_pe_k, cmp_w1_k,
           cmp_w2_k, cmp_pe_v, cmp_w1_v, cmp_w2_v, w_branch_a, w_branch_b, w_out, g_ffn, w_peer_q, peer_sub_keys,
           peer_u, peer_v):
    bsz, t, d = x.shape
    assert bsz == 1 and t % 256 == 0 and d % 128 == 0
    xs = x[0]
    for i in range(w_ada.shape[0]):
        mod = _ada(c, w_ada[i], b_ada[i])
        sh1, sc1, gt1, sh2, sc2, gt2 = [mod[:, n * d:(n + 1) * d] for n in range(6)]
        h = _normmod(xs, g_mix[i], sc1, sh1)
        mix = _token_mixer(h, rel_bias, w_in[i], gq_a[i], gk_a[i], gq_b[i], gk_cmp[i], gk_slc[i], gk_win[i],
                           cmp_pe_k[i], cmp_w1_k[i], cmp_w2_k[i], cmp_pe_v[i], cmp_w1_v[i], cmp_w2_v[i],
                           w_branch_a[i], w_branch_b[i], w_out[i])
        x1, h2 = _resid_normmod(xs, mix, gt1, g_ffn[i], sc2, sh2)
        ffn = _peer_ffn(h2, w_peer_q[i], peer_sub_keys[i], peer_u[i], peer_v[i])
        xs = _resid(x1, ffn, gt2)
    return xs[None]
```

```python
import functools
import math

import numpy as np
import jax
import jax.numpy as jnp
from jax import lax
from jax.experimental import pallas as pl
from jax.experimental.pallas import tpu as pltpu

HEAD_DIM = 128
DSA_HEADS = 16
DSA_KV_HEADS = 4
IDX_HEADS = 32
IDX_DIM = 64
DSA_TOPK = 256
NSA_HEADS = 16
NSA_KV_HEADS = 4
CMP_LEN = 32
CMP_STRIDE = 16
CMP_HIDDEN = 256
SLC_LEN = 64
SLC_TOP = 16
WIN = 512
FORCE_SCORE = 1e9
PEER_HEADS = 8
PEER_NKEYS = 128
PEER_QDIM = 256
PEER_TOPK = 16
REL_BUCKETS = 32
REL_MAX_DIST = 2048
EPS = 1e-6

LANE = 128
VMEM_LIMIT = 56 * 1024 * 1024
NEG = -1e30
LOG2E = math.log2(math.e)
INT_MIN = -2 ** 31
KEY_NEG_INF = -2 ** 31 + 0x007FFFFF

OFF_QA, OFF_QI, OFF_QB, OFF_KVB, OFF_KA, OFF_VA, OFF_GM = 0, 2048, 4096, 6144, 9216, 9728, 10240

F32 = jnp.float32
BF16 = jnp.bfloat16


def _bucket_starts():
    n = np.arange(2 * REL_MAX_DIST)
    exact = REL_BUCKETS // 2
    nf = np.maximum(n, 1).astype(np.float32)
    lb = exact + (np.log(nf / np.float32(exact)) / np.float32(math.log(REL_MAX_DIST / exact))
                  * np.float32(REL_BUCKETS - exact)).astype(np.int32)
    bucket = np.where(n < exact, n, np.minimum(lb, REL_BUCKETS - 1))
    return [int(np.argmax(bucket >= b)) for b in range(REL_BUCKETS)]


BUCKET_STARTS = _bucket_starts()
FAR_DIST = BUCKET_STARTS[REL_BUCKETS - 1]


def _cparams(sem):
    return pltpu.CompilerParams(dimension_semantics=sem, vmem_limit_bytes=VMEM_LIMIT)


def _pick(n, pref):
    t = pref
    while n % t:
        t //= 2
    return t


def _to_key(x):
    b = lax.bitcast_convert_type(x, jnp.int32)
    return b ^ ((b >> 31) & 0x7FFFFFFF)


def _from_key(k):
    return lax.bitcast_convert_type(k ^ ((k >> 31) & 0x7FFFFFFF), F32)


def _kth_key(count_ge, shape, kk):
    def bit_pass(b, x):
        cand = x + lax.shift_left(jnp.int32(1), 31 - b)
        return jnp.where(count_ge(cand) >= kk, cand, x)
    x = lax.fori_loop(0, 32, bit_pass, jnp.full(shape, INT_MIN, jnp.int32))
    return jnp.maximum(x, KEY_NEG_INF)


def _kth_key_val(keys, kk, axis=-1):
    def count_ge(cand):
        return jnp.sum(jnp.where(keys >= cand, 1.0, 0.0), axis=axis, keepdims=True)
    shape = list(keys.shape)
    shape[axis] = 1
    return _kth_key(count_ge, tuple(shape), float(kk))


def _gelu(x):
    return 0.5 * x * (1.0 + jnp.tanh(math.sqrt(2.0 / math.pi) * (x + 0.044715 * (x * x * x))))


def _dot_nt(a, b):
    return lax.dot_general(a, b, (((1,), (1,)), ((), ())), preferred_element_type=F32)


def _head_norm(x, g, scale):
    ms = jnp.mean(x * x, axis=-1, keepdims=True)
    return x * lax.rsqrt(ms + EPS) * g * scale


def _mm_kernel(a_ref, b_ref, o_ref):
    o_ref[...] = jnp.dot(a_ref[...], b_ref[...], preferred_element_type=F32).astype(o_ref.dtype)


def _matmul(a, b, out_dtype=F32, tm=1024, tn=512, name="matmul"):
    m, k = a.shape
    n = b.shape[1]
    tm, tn = _pick(m, tm), _pick(n, tn)
    return pl.pallas_call(
        _mm_kernel, grid=(m // tm, n // tn),
        in_specs=[pl.BlockSpec((tm, k), lambda i, j: (i, 0)), pl.BlockSpec((k, tn), lambda i, j: (0, j))],
        out_specs=pl.BlockSpec((tm, tn), lambda i, j: (i, j)),
        out_shape=jax.ShapeDtypeStruct((m, n), out_dtype),
        compiler_params=_cparams(("parallel", "arbitrary")), name=name)(a, b)


def _mm_resid_kernel(a_ref, b_ref, x_ref, gt_ref, o_ref):
    k = pl.program_id(2)
    part = jnp.dot(a_ref[...], b_ref[...], preferred_element_type=F32)

    @pl.when(k == 0)
    def _():
        o_ref[...] = part

    @pl.when(k > 0)
    def _():
        o_ref[...] += part

    @pl.when(k == pl.num_programs(2) - 1)
    def _():
        o_ref[...] = x_ref[...] + gt_ref[...] * o_ref[...]


def _matmul_resid(a, b, x, gt, name):
    m, kd = a.shape
    n = b.shape[1]
    tm, tn, tk = _pick(m, 1024), _pick(n, 1024), _pick(kd, 1024)
    return pl.pallas_call(
        _mm_resid_kernel, grid=(m // tm, n // tn, kd // tk),
        in_specs=[pl.BlockSpec((tm, tk), lambda i, j, k: (i, k)), pl.BlockSpec((tk, tn), lambda i, j, k: (k, j)),
                  pl.BlockSpec((tm, tn), lambda i, j, k: (i, j)), pl.BlockSpec((1, tn), lambda i, j, k: (0, j))],
        out_specs=pl.BlockSpec((tm, tn), lambda i, j, k: (i, j)),
        out_shape=jax.ShapeDtypeStruct((m, n), F32),
        compiler_params=_cparams(("parallel", "parallel", "arbitrary")), name=name)(a, b, x, gt)


def _ada_kernel(c_ref, w_ref, b_ref, o_ref):
    o_ref[...] = jnp.dot(c_ref[...], w_ref[...].astype(BF16), preferred_element_type=F32) + b_ref[...]


def _ada(c, w, b):
    d, n = w.shape
    tn = _pick(n, 512)
    c8 = jnp.broadcast_to(c.astype(BF16), (8, d))
    out = pl.pallas_call(
        _ada_kernel, grid=(n // tn,),
        in_specs=[pl.BlockSpec((8, d), lambda j: (0, 0)), pl.BlockSpec((d, tn), lambda j: (0, j)),
                  pl.BlockSpec((1, tn), lambda j: (0, j))],
        out_specs=pl.BlockSpec((8, tn), lambda j: (0, j)),
        out_shape=jax.ShapeDtypeStruct((8, n), F32),
        compiler_params=_cparams(("arbitrary",)), name="ada")(c8, w, b.reshape(1, n))
    return out[0:1]


def _normmod_kernel(x_ref, g_ref, sc_ref, sh_ref, o_ref):
    x = x_ref[...]
    y = x * lax.rsqrt(jnp.mean(x * x, axis=-1, keepdims=True) + EPS) * g_ref[...]
    o_ref[...] = (y * (1.0 + sc_ref[...]) + sh_ref[...]).astype(o_ref.dtype)


def _normmod(x, g, sc, sh):
    t, d = x.shape
    tm = _pick(t, 256)
    row = pl.BlockSpec((1, d), lambda i: (0, 0))
    return pl.pallas_call(
        _normmod_kernel, grid=(t // tm,),
        in_specs=[pl.BlockSpec((tm, d), lambda i: (i, 0)), row, row, row],
        out_specs=pl.BlockSpec((tm, d), lambda i: (i, 0)),
        out_shape=jax.ShapeDtypeStruct((t, d), BF16),
        compiler_params=_cparams(("parallel",)), name="normmod")(x, g.reshape(1, d), sc, sh)


def _resid_normmod_kernel(x_ref, o_ref, gt_ref, g_ref, sc_ref, sh_ref, x1_ref, h_ref):
    x1 = x_ref[...] + gt_ref[...] * o_ref[...]
    x1_ref[...] = x1
    y = x1 * lax.rsqrt(jnp.mean(x1 * x1, axis=-1, keepdims=True) + EPS) * g_ref[...]
    h_ref[...] = (y * (1.0 + sc_ref[...]) + sh_ref[...]).astype(h_ref.dtype)


def _resid_normmod(x, o, gt, g, sc, sh):
    t, d = x.shape
    tm = _pick(t, 256)
    row = pl.BlockSpec((1, d), lambda i: (0, 0))
    blk = pl.BlockSpec((tm, d), lambda i: (i, 0))
    return pl.pallas_call(
        _resid_normmod_kernel, grid=(t // tm,),
        in_specs=[blk, blk, row, row, row, row],
        out_specs=[blk, blk],
        out_shape=[jax.ShapeDtypeStruct((t, d), F32), jax.ShapeDtypeStruct((t, d), BF16)],
        compiler_params=_cparams(("parallel",)), name="resid_normmod")(x, o, gt, g.reshape(1, d), sc, sh)


def _prep_kernel(qa_ref, qi_ref, qb_ref, kvb_ref, ka_ref, va_ref, tail_ref,
                 gqa_ref, gka_ref, gqb_ref, gslc_ref, gwin_ref,
                 qa_o, qi_o, qb_o, kvb_o, ka_o, va_o, ki_o, wi_o, gs_o):
    qscale = HEAD_DIM ** -0.5 * LOG2E
    for h in range(DSA_HEADS):
        qa_o[h] = _head_norm(qa_ref[:, h * 128:(h + 1) * 128], gqa_ref[...], qscale).astype(BF16)
    for h in range(NSA_HEADS):
        qb_o[h] = _head_norm(qb_ref[:, h * 128:(h + 1) * 128], gqb_ref[...], qscale).astype(BF16)
    for h in range(IDX_HEADS):
        qi_o[h] = (qi_ref[:, h * IDX_DIM:(h + 1) * IDX_DIM] * IDX_DIM ** -0.5).astype(BF16)
    for g in range(DSA_KV_HEADS):
        ka_o[g] = _head_norm(ka_ref[:, g * 128:(g + 1) * 128], gka_ref[...], 1.0).astype(BF16)
        va_o[g] = va_ref[:, g * 128:(g + 1) * 128].astype(BF16)
    for s in range(6 * NSA_KV_HEADS):
        xs = kvb_ref[:, s * 128:(s + 1) * 128]
        part = s // NSA_KV_HEADS
        if part == 2:
            xs = _head_norm(xs, gslc_ref[...], 1.0)
        elif part == 4:
            xs = _head_norm(xs, gwin_ref[...], 1.0)
        kvb_o[s] = xs.astype(BF16)
    tail = tail_ref[...]
    ki_o[...] = tail[:, 0:IDX_DIM].astype(BF16)
    wi_o[...] = tail[:, 0:128] * IDX_HEADS ** -0.5
    gs_o[...] = jax.nn.sigmoid(tail[:, 128:256])


def _prep(pm, pt, gq_a, gk_a, gq_b, gk_slc, gk_win):
    t = pm.shape[0]
    tm = _pick(t, 128)
    g128 = pl.BlockSpec((1, 128), lambda i: (0, 0))

    def col(w, off):
        assert off % w == 0
        return pl.BlockSpec((tm, w), lambda i: (i, off // w))

    def hm(n, w):
        return pl.BlockSpec((n, tm, w), lambda i: (0, i, 0))

    row = lambda w: pl.BlockSpec((tm, w), lambda i: (i, 0))
    outs = pl.pallas_call(
        _prep_kernel, grid=(t // tm,),
        in_specs=[col(2048, OFF_QA), col(2048, OFF_QI), col(2048, OFF_QB), col(3072, OFF_KVB),
                  col(512, OFF_KA), col(512, OFF_VA), row(256), g128, g128, g128, g128, g128],
        out_specs=[hm(DSA_HEADS, 128), hm(IDX_HEADS, IDX_DIM), hm(NSA_HEADS, 128), hm(24, 128),
                   hm(DSA_KV_HEADS, 128), hm(DSA_KV_HEADS, 128), row(IDX_DIM), row(128), row(128)],
        out_shape=[jax.ShapeDtypeStruct((DSA_HEADS, t, 128), BF16),
                   jax.ShapeDtypeStruct((IDX_HEADS, t, IDX_DIM), BF16),
                   jax.ShapeDtypeStruct((NSA_HEADS, t, 128), BF16),
                   jax.ShapeDtypeStruct((24, t, 128), BF16),
                   jax.ShapeDtypeStruct((DSA_KV_HEADS, t, 128), BF16),
                   jax.ShapeDtypeStruct((DSA_KV_HEADS, t, 128), BF16),
                   jax.ShapeDtypeStruct((t, IDX_DIM), BF16),
                   jax.ShapeDtypeStruct((t, 128), F32),
                   jax.ShapeDtypeStruct((t, 128), F32)],
        compiler_params=_cparams(("parallel",)), name="prep")(
            pm, pm, pm, pm, pm, pm, pt,
            gq_a.reshape(1, 128), gk_a.reshape(1, 128), gq_b.reshape(1, 128),
            gk_slc.reshape(1, 128), gk_win.reshape(1, 128))
    return outs


def _bias_from_dist(dist, tbl_ref, h, nheads):
    val = jnp.full(dist.shape, tbl_ref[h], F32)
    for b in range(1, REL_BUCKETS):
        val = jnp.where(dist >= BUCKET_STARTS[b], tbl_ref[b * nheads + h], val)
    return (val - tbl_ref[(REL_BUCKETS - 1) * nheads + h]) * LOG2E


def _bias_tiles_kernel(tbl_ref, o_ref, *, tq, tk, nheads):
    h, d = pl.program_id(0), pl.program_id(1)
    dist = d * tq + lax.broadcasted_iota(jnp.int32, (tq, tk), 0) - lax.broadcasted_iota(jnp.int32, (tq, tk), 1)
    o_ref[0, 0] = _bias_from_dist(dist, tbl_ref, h, nheads)


def _n_near(tq, tk):
    return -(-(FAR_DIST + tk - 1) // tq)


def _bias_tiles(table, tq, tk):
    nheads = table.shape[1]
    nd = _n_near(tq, tk)
    return pl.pallas_call(
        functools.partial(_bias_tiles_kernel, tq=tq, tk=tk, nheads=nheads), grid=(nheads, nd),
        in_specs=[pl.BlockSpec(memory_space=pltpu.SMEM)],
        out_specs=pl.BlockSpec((1, 1, tq, tk), lambda h, d: (h, d, 0, 0)),
        out_shape=jax.ShapeDtypeStruct((nheads, nd, tq, tk), F32),
        compiler_params=_cparams(("parallel", "parallel")), name="bias_tiles")(table.reshape(-1))


def _idx_kernel(q_ref, kt_ref, w_ref, key_ref, thr_ref, s_ref, wb_ref, *, tq, tk, t, kk):
    i = pl.program_id(0)
    nkt = (i * tq + tq + tk - 1) // tk
    w = w_ref[...]
    for h in range(IDX_HEADS):
        wb_ref[h] = jnp.broadcast_to(w[:, IDX_DIM + h:IDX_DIM + h + 1], (tq, LANE))
    qall = q_ref[...].reshape(IDX_HEADS * tq, IDX_DIM)
    rows = lax.broadcasted_iota(jnp.int32, (tq, tk), 0)
    cols = lax.broadcasted_iota(jnp.int32, (tq, tk), 1)

    def tile(j, carry):
        off = pl.multiple_of(j * tk, tk)
        s_ref[...] = jnp.dot(qall, kt_ref[:, pl.ds(off, tk)], preferred_element_type=F32)
        acc = jnp.zeros((tq, tk), F32)
        for h in range(IDX_HEADS):
            wbh = jnp.concatenate([wb_ref[h]] * (tk // LANE), axis=1)
            acc = acc + jnp.maximum(s_ref[h * tq:(h + 1) * tq, :], 0.0) * wbh
        acc = jnp.where(off + cols <= i * tq + rows, acc, -jnp.inf)
        key_ref[:, pl.ds(off, tk)] = _to_key(acc)
        return carry

    lax.fori_loop(0, nkt, tile, 0)

    def fill(j, carry):
        key_ref[:, pl.ds(pl.multiple_of(j * tk, tk), tk)] = jnp.full((tq, tk), KEY_NEG_INF, jnp.int32)
        return carry

    lax.fori_loop(nkt, t // tk, fill, 0)

    def count_ge(cand):
        def body(j, c):
            kt = key_ref[:, pl.ds(pl.multiple_of(j * tk, tk), tk)]
            ge = jnp.where(kt >= cand, 1.0, 0.0)
            for a in range(tk // LANE):
                c = c + ge[:, a * LANE:(a + 1) * LANE]
            return c
        c = lax.fori_loop(0, nkt, body, jnp.zeros((tq, LANE), F32))
        return jnp.sum(c, axis=-1, keepdims=True)

    thr_ref[...] = _kth_key(count_ge, (tq, 1), float(kk))


def _dsa_index(qi_h, ki_t, wi, kk):
    t = ki_t.shape[1]
    tq, tk = _pick(t, 128), _pick(t, 256)
    return pl.pallas_call(
        functools.partial(_idx_kernel, tq=tq, tk=tk, t=t, kk=kk), grid=(t // tq,),
        in_specs=[pl.BlockSpec((IDX_HEADS, tq, IDX_DIM), lambda i: (0, i, 0)),
                  pl.BlockSpec((IDX_DIM, t), lambda i: (0, 0)),
                  pl.BlockSpec((tq, 128), lambda i: (i, 0))],
        out_specs=[pl.BlockSpec((tq, t), lambda i: (i, 0)), pl.BlockSpec((tq, 1), lambda i: (i, 0))],
        out_shape=[jax.ShapeDtypeStruct((t, t), jnp.int32), jax.ShapeDtypeStruct((t, 1), jnp.int32)],
        scratch_shapes=[pltpu.VMEM((IDX_HEADS * tq, tk), F32), pltpu.VMEM((IDX_HEADS, tq, LANE), F32)],
        compiler_params=_cparams(("parallel",)), name="dsa_index")(qi_h, ki_t, wi)


def _kv_tile(mode, i, s, tq, tk):
    if mode == "win":
        return jnp.maximum(i * tq - (WIN - 1), 0) // tk + s
    return s


def _last_tile(i, tq, tk):
    return (i * tq + tq - 1) // tk


def _flash_kernel(*refs, mode, tq, tk, nheads, ngroups, nnear):
    if mode == "dsa":
        q_ref, k_ref, v_ref, b_ref, key_ref, thr_ref, o_ref, m_ref, l_ref, acc_ref = refs
    elif mode == "slc":
        q_ref, k_ref, v_ref, b_ref, sel_ref, e_ref, o_ref, m_ref, l_ref, acc_ref = refs
    else:
        q_ref, k_ref, v_ref, b_ref, o_ref, m_ref, l_ref, acc_ref = refs
    rep = nheads // ngroups
    nch = tk // LANE
    i, s = pl.program_id(0), pl.program_id(1)
    j = _kv_tile(mode, i, s, tq, tk)
    jlast = _last_tile(i, tq, tk)
    d = i - (tk // tq) * j

    @pl.when(s == 0)
    def _():
        m_ref[...] = jnp.full(m_ref.shape, NEG, F32)
        l_ref[...] = jnp.zeros(l_ref.shape, F32)
        acc_ref[...] = jnp.zeros(acc_ref.shape, F32)

    def step(near):
        if near:
            dist = ((i * tq - j * tk) + lax.broadcasted_iota(jnp.int32, (tq, tk), 0)
                    - lax.broadcasted_iota(jnp.int32, (tq, tk), 1))
            causal = dist >= 0
            if mode == "win":
                causal = causal & (dist < WIN)
        if mode == "dsa":
            hit = key_ref[...] >= thr_ref[...]
            mask = (hit & causal) if near else hit
        elif mode == "win":
            mask = causal
        for g in range(ngroups):
            s_all = _dot_nt(q_ref[g * rep:(g + 1) * rep].reshape(rep * tq, HEAD_DIM), k_ref[g])
            if mode == "slc":
                hit = jnp.dot(sel_ref[g], e_ref[...], preferred_element_type=F32) > 0.5
                mask = (hit & causal) if near else hit
            ps = []
            for r in range(rep):
                h = g * rep + r
                sh = s_all[r * tq:(r + 1) * tq]
                if near:
                    sh = sh + b_ref[h, 0]
                sh = jnp.where(mask, sh, NEG)
                chunks = [sh[:, c * LANE:(c + 1) * LANE] for c in range(nch)]
                m_old = m_ref[h]
                tile_max = jnp.max(functools.reduce(jnp.maximum, chunks), axis=-1, keepdims=True)
                m_new = jnp.maximum(m_old, tile_max)
                alpha = jnp.exp2(m_old - m_new)
                pcs = [jnp.exp2(c - m_new) for c in chunks]
                l_ref[h] = alpha * l_ref[h] + functools.reduce(jnp.add, pcs)
                m_ref[h] = m_new
                acc_ref[h] = acc_ref[h] * alpha
                ps.append(jnp.concatenate(pcs, axis=1).astype(BF16))
            pv = jnp.dot(jnp.concatenate(ps, axis=0), v_ref[g], preferred_element_type=F32)
            for r in range(rep):
                h = g * rep + r
                acc_ref[h] = acc_ref[h] + pv[r * tq:(r + 1) * tq]

    valid = j <= jlast
    if mode == "win":
        pl.when(valid)(lambda: step(True))
    else:
        pl.when(jnp.logical_and(valid, d < nnear))(lambda: step(True))
        pl.when(jnp.logical_and(valid, d >= nnear))(lambda: step(False))

    @pl.when(j == jlast)
    def _():
        for h in range(nheads):
            l = jnp.sum(l_ref[h], axis=-1, keepdims=True)
            o_ref[:, h * HEAD_DIM:(h + 1) * HEAD_DIM] = (acc_ref[h] / l).astype(o_ref.dtype)


def _flash(mode, q_h, k_h, v_h, bias, extra, out_dtype):
    nheads, t, _ = q_h.shape
    ngroups = k_h.shape[0]
    _, nnear, tq, tk = bias.shape
    nq = t // tq
    if mode == "win":
        nsteps = max((i * tq + tq - 1) // tk - max(i * tq - (WIN - 1), 0) // tk + 1 for i in range(nq))
    else:
        nsteps = t // tk

    def jmap(i, s):
        return jnp.minimum(_kv_tile(mode, i, s, tq, tk), _last_tile(i, tq, tk))

    def bmap(i, s):
        return jnp.minimum(i - (tk // tq) * jmap(i, s), nnear - 1)

    in_specs = [pl.BlockSpec((nheads, tq, HEAD_DIM), lambda i, s: (0, i, 0)),
                pl.BlockSpec((ngroups, tk, HEAD_DIM), lambda i, s: (0, jmap(i, s), 0)),
                pl.BlockSpec((ngroups, tk, HEAD_DIM), lambda i, s: (0, jmap(i, s), 0)),
                pl.BlockSpec((nheads, 1, tq, tk), lambda i, s: (0, bmap(i, s), 0, 0))]
    if mode == "dsa":
        in_specs += [pl.BlockSpec((tq, tk), lambda i, s: (i, jmap(i, s))),
                     pl.BlockSpec((tq, 1), lambda i, s: (i, 0))]
    elif mode == "slc":
        ns = extra[0].shape[2]
        in_specs += [pl.BlockSpec((ngroups, tq, ns), lambda i, s: (0, i, 0)),
                     pl.BlockSpec((ns, tk), lambda i, s: (0, jmap(i, s)))]
    return pl.pallas_call(
        functools.partial(_flash_kernel, mode=mode, tq=tq, tk=tk, nheads=nheads, ngroups=ngroups, nnear=nnear),
        grid=(nq, nsteps), in_specs=in_specs,
        out_specs=pl.BlockSpec((tq, nheads * HEAD_DIM), lambda i, s: (i, 0)),
        out_shape=jax.ShapeDtypeStruct((t, nheads * HEAD_DIM), out_dtype),
        scratch_shapes=[pltpu.VMEM((nheads, tq, LANE), F32), pltpu.VMEM((nheads, tq, LANE), F32),
                        pltpu.VMEM((nheads, tq, HEAD_DIM), F32)],
        compiler_params=_cparams(("parallel", "arbitrary")), name="flash_" + mode)(q_h, k_h, v_h, bias, *extra)


def _compress_kernel(c_ref, w1_ref, pe_ref, w2_ref, g_ref, o_ref, *, norm):
    c = c_ref[0]
    half = CMP_STRIDE * HEAD_DIM
    ncp = c.shape[0]
    a = jnp.dot(c, w1_ref[0:half], preferred_element_type=F32)
    b = jnp.dot(c, w1_ref[half:2 * half], preferred_element_type=F32)
    pet = jnp.dot(pe_ref[...], w1_ref[...], preferred_element_type=F32)[0:1]
    hid = _gelu(a + pltpu.roll(b, ncp - 1, 0) + pet)
    y = jnp.dot(hid.astype(BF16), w2_ref[...], preferred_element_type=F32)
    if norm:
        y = _head_norm(y, g_ref[...], 1.0)
    o_ref[0] = y.astype(BF16)


def _compress(slabs, base, pe, w1, w2, gain, norm):
    _, t, _ = slabs.shape
    ncp = t // CMP_STRIDE
    width = CMP_STRIDE * HEAD_DIM
    chunks = slabs.reshape(slabs.shape[0], ncp, width)
    w1f = w1.reshape(CMP_LEN * HEAD_DIM, CMP_HIDDEN).astype(BF16)
    pe8 = jnp.broadcast_to(pe.reshape(1, CMP_LEN * HEAD_DIM).astype(BF16), (8, CMP_LEN * HEAD_DIM))
    full = lambda shp: pl.BlockSpec(shp, lambda g: (0,) * len(shp))
    return pl.pallas_call(
        functools.partial(_compress_kernel, norm=norm), grid=(NSA_KV_HEADS,),
        in_specs=[pl.BlockSpec((1, ncp, width), lambda g: (base + g, 0, 0)),
                  full((CMP_LEN * HEAD_DIM, CMP_HIDDEN)), full((8, CMP_LEN * HEAD_DIM)),
                  full((CMP_HIDDEN, HEAD_DIM)), full((1, HEAD_DIM))],
        out_specs=pl.BlockSpec((1, ncp, HEAD_DIM), lambda g: (g, 0, 0)),
        out_shape=jax.ShapeDtypeStruct((NSA_KV_HEADS, ncp, HEAD_DIM), BF16),
        compiler_params=_cparams(("parallel",)), name="compress")(
            chunks, w1f, pe8, w2.astype(BF16), gain.reshape(1, HEAD_DIM))


def _cmp_attn_kernel(tbl_ref, q_ref, kc_ref, vc_ref, ov_ref, oc_ref, sel_ref, s_ref, *, tq, ncp, ns, nsel, ww):
    i = pl.program_id(0)
    rep = NSA_HEADS // NSA_KV_HEADS
    qpos = i * tq + lax.broadcasted_iota(jnp.int32, (tq, ncp), 0)
    dist = qpos - (lax.broadcasted_iota(jnp.int32, (tq, ncp), 1) * CMP_STRIDE + CMP_LEN - 1)
    mask = dist >= 0
    w0 = jnp.maximum(i * tq - (CMP_LEN - 1) - FAR_DIST + CMP_STRIDE, 0) // (CMP_STRIDE * LANE) * LANE
    w0 = pl.multiple_of(jnp.minimum(w0, ncp - ww), LANE)
    dist_w = (i * tq + lax.broadcasted_iota(jnp.int32, (tq, ww), 0)
              - ((w0 + lax.broadcasted_iota(jnp.int32, (tq, ww), 1)) * CMP_STRIDE + CMP_LEN - 1))
    blk = lax.broadcasted_iota(jnp.int32, (tq, ns), 1)
    qp = i * tq + lax.broadcasted_iota(jnp.int32, (tq, ns), 0)
    cur = qp // SLC_LEN
    forced = (blk == 0) | (blk == cur) | (blk == cur - 1)
    admissible = blk * SLC_LEN <= qp
    for g in range(NSA_KV_HEADS):
        s_ref[...] = _dot_nt(q_ref[g * rep:(g + 1) * rep].reshape(rep * tq, HEAD_DIM), kc_ref[g])
        imp = jnp.zeros((tq, ncp), F32)
        ps = []
        for r in range(rep):
            h = g * rep + r
            s_ref[r * tq:(r + 1) * tq, pl.ds(w0, ww)] += _bias_from_dist(dist_w, tbl_ref, h, NSA_HEADS)
            sh = jnp.where(mask, s_ref[r * tq:(r + 1) * tq, :], NEG)
            m = jnp.max(sh, axis=-1, keepdims=True)
            p = jnp.where(mask, jnp.exp2(sh - m), 0.0)
            pc = p / jnp.maximum(jnp.sum(p, axis=-1, keepdims=True), 1e-30)
            imp = imp + pc
            ps.append(pc.astype(BF16))
        o = jnp.dot(jnp.concatenate(ps, axis=0), vc_ref[g], preferred_element_type=F32)
        for r in range(rep):
            h = g * rep + r
            oc_ref[:, h * HEAD_DIM:(h + 1) * HEAD_DIM] = o[r * tq:(r + 1) * tq]
        hi = imp.astype(BF16)
        lo = (imp - hi.astype(F32)).astype(BF16)
        impb = (jnp.dot(hi, ov_ref[...], preferred_element_type=F32)
                + jnp.dot(lo, ov_ref[...], preferred_element_type=F32))
        impb = jnp.where(forced, FORCE_SCORE, impb)
        impb = jnp.where(admissible, impb, -jnp.inf)
        keys = _to_key(impb)
        thr = _kth_key_val(keys, nsel)
        sel_ref[g] = jnp.where(keys >= thr, 1.0, 0.0).astype(BF16)


def _cmp_attn(table, q_h, kc, vc):
    nheads, t, _ = q_h.shape
    ncp = kc.shape[1]
    ns = t // SLC_LEN
    nsel = min(SLC_TOP, ns)
    tq = _pick(t, 128)
    ww = min(ncp, LANE * (-(-(FAR_DIST + tq + CMP_STRIDE * LANE) // (CMP_STRIDE * LANE))))
    rep = nheads // NSA_KV_HEADS
    c_start = np.arange(ncp)[:, None] * CMP_STRIDE
    s_start = np.arange(ns)[None, :] * SLC_LEN
    overlap = jnp.asarray((c_start < s_start + SLC_LEN) & (c_start + CMP_LEN > s_start), BF16)
    full = lambda shp: pl.BlockSpec(shp, lambda i: (0,) * len(shp))
    return pl.pallas_call(
        functools.partial(_cmp_attn_kernel, tq=tq, ncp=ncp, ns=ns, nsel=nsel, ww=ww), grid=(t // tq,),
        in_specs=[pl.BlockSpec(memory_space=pltpu.SMEM),
                  pl.BlockSpec((nheads, tq, HEAD_DIM), lambda i: (0, i, 0)),
                  full((NSA_KV_HEADS, ncp, HEAD_DIM)), full((NSA_KV_HEADS, ncp, HEAD_DIM)), full((ncp, ns))],
        out_specs=[pl.BlockSpec((tq, nheads * HEAD_DIM), lambda i: (i, 0)),
                   pl.BlockSpec((NSA_KV_HEADS, tq, ns), lambda i: (0, i, 0))],
        out_shape=[jax.ShapeDtypeStruct((t, nheads * HEAD_DIM), F32),
                   jax.ShapeDtypeStruct((NSA_KV_HEADS, t, ns), BF16)],
        scratch_shapes=[pltpu.VMEM((rep * tq, ncp), F32)],
        compiler_params=_cparams(("parallel",)), name="cmp_attn")(table.reshape(-1), q_h, kc, vc, overlap)


def _combine_kernel(oc_ref, os_ref, ow_ref, g_ref, y_ref):
    g = g_ref[...]
    for h in range(NSA_HEADS):
        sl = slice(h * HEAD_DIM, (h + 1) * HEAD_DIM)
        y = (g[:, 3 * h:3 * h + 1] * oc_ref[:, sl] + g[:, 3 * h + 1:3 * h + 2] * os_ref[:, sl]
             + g[:, 3 * h + 2:3 * h + 3] * ow_ref[:, sl])
        y_ref[:, sl] = y.astype(y_ref.dtype)


def _combine(oc, os_, ow, gsig):
    t, w = oc.shape
    tm = _pick(t, 256)
    blk = pl.BlockSpec((tm, w), lambda i: (i, 0))
    return pl.pallas_call(
        _combine_kernel, grid=(t // tm,),
        in_specs=[blk, blk, blk, pl.BlockSpec((tm, 128), lambda i: (i, 0))],
        out_specs=blk, out_shape=jax.ShapeDtypeStruct((t, w), BF16),
        compiler_params=_cparams(("parallel",)), name="combine")(oc, os_, ow, gsig)


def _merge_kernel(a_ref, b_ref, ga_ref, gb_ref, o_ref):
    o_ref[...] = (jax.nn.sigmoid(ga_ref[...]) * a_ref[...]
                  + jax.nn.sigmoid(gb_ref[...]) * b_ref[...]).astype(o_ref.dtype)


def _merge(a, b, pm):
    t, d = a.shape
    tm, tn = _pick(t, 256), _pick(d, 512)
    assert OFF_GM % tn == 0
    blk = pl.BlockSpec((tm, tn), lambda i, j: (i, j))
    return pl.pallas_call(
        _merge_kernel, grid=(t // tm, d // tn),
        in_specs=[blk, blk, pl.BlockSpec((tm, tn), lambda i, j: (i, OFF_GM // tn + j)),
                  pl.BlockSpec((tm, tn), lambda i, j: (i, (OFF_GM + d) // tn + j))],
        out_specs=blk, out_shape=jax.ShapeDtypeStruct((t, d), BF16),
        compiler_params=_cparams(("parallel", "parallel")), name="merge")(a, b, pm, pm)


_CAND_FULL = PEER_TOPK // 2


def _peer_cands(a1, a2, op):
    pieces = [op(a1[0:1], a2)]
    pieces += [op(a1[k:k + 1], a2[0:_CAND_FULL]) for k in range(1, _CAND_FULL)]
    pieces.append(op(a1[_CAND_FULL:], a2[0:1]))
    return jnp.concatenate(pieces, axis=0)


def _peer_score_kernel(q_ref, sk_ref, s1_o, e1_o, s2_o, e2_o, thr_o, *, tm):
    row = lax.broadcasted_iota(jnp.int32, (PEER_NKEYS, tm), 0).astype(F32)
    for h in range(PEER_HEADS):
        svals, tops = [], []
        for c in range(2):
            hc = 2 * h + c
            s = _dot_nt(sk_ref[hc], q_ref[:, hc * 128:(hc + 1) * 128].astype(BF16))
            svals.append(s)
            work, top = s, []
            for _ in range(PEER_TOPK):
                mx = jnp.max(work, axis=0, keepdims=True)
                first = jnp.min(jnp.where(work == mx, row, float(PEER_NKEYS)), axis=0, keepdims=True)
                work = jnp.where(row == first, -jnp.inf, work)
                top.append(mx)
            tops.append(jnp.concatenate(top, axis=0))
        a1, a2 = tops
        cand = _peer_cands(a1, a2, jnp.add)
        thr = _from_key(_kth_key_val(_to_key(cand), PEER_TOPK, axis=0))
        m1, m2 = a1[0:1], a2[0:1]
        ec = _peer_cands(jnp.exp(a1 - m1), jnp.exp(a2 - m2), jnp.multiply)
        z = jnp.sum(jnp.where(cand >= thr, ec, 0.0), axis=0, keepdims=True)
        s1_o[h] = svals[0]
        s2_o[h] = svals[1]
        e1_o[h] = jnp.exp(svals[0] - m1)
        e2_o[h] = jnp.exp(svals[1] - m2) / z
        thr_o[h:h + 1, :] = thr


def _peer_score(qp, sub_keys):
    t = qp.shape[0]
    tm = _pick(t, 256)
    sk = sub_keys.reshape(2 * PEER_HEADS, PEER_NKEYS, PEER_QDIM // 2).astype(BF16)
    tr = pl.BlockSpec((PEER_HEADS, PEER_NKEYS, tm), lambda i: (0, 0, i))
    shp = jax.ShapeDtypeStruct((PEER_HEADS, PEER_NKEYS, t), F32)
    return pl.pallas_call(
        functools.partial(_peer_score_kernel, tm=tm), grid=(t // tm,),
        in_specs=[pl.BlockSpec((tm, 2 * PEER_HEADS * 128), lambda i: (i, 0)),
                  pl.BlockSpec((2 * PEER_HEADS, PEER_NKEYS, PEER_QDIM // 2), lambda i: (0, 0, 0))],
        out_specs=[tr, tr, tr, tr, pl.BlockSpec((PEER_HEADS, tm), lambda i: (0, i))],
        out_shape=[shp, shp, shp, shp, jax.ShapeDtypeStruct((PEER_HEADS, t), F32)],
        compiler_params=_cparams(("parallel",)), name="peer_score")(qp, sk)


def _peer_act_kernel(h_ref, u_ref, s1_ref, e1_ref, s2_ref, e2_ref, thr_ref, o_ref, *, tm, te, sub):
    j = pl.program_id(1)
    for c in range(te // sub):
        parts = []
        for ai in range(sub // PEER_NKEYS):
            i1 = j * (te // PEER_NKEYS) + c * (sub // PEER_NKEYS) + ai
            wt = jnp.zeros((PEER_NKEYS, tm), F32)
            for h in range(PEER_HEADS):
                s1row = s1_ref[h, pl.ds(i1, 1), :]
                e1row = e1_ref[h, pl.ds(i1, 1), :]
                hit = (s1row + s2_ref[h]) >= thr_ref[h:h + 1, :]
                wt = wt + jnp.where(hit, e1row * e2_ref[h], 0.0)
            parts.append(wt)
        w = jnp.concatenate(parts, axis=0).T
        a = _dot_nt(h_ref[...], u_ref[c * sub:(c + 1) * sub, :])
        o_ref[:, c * sub:(c + 1) * sub] = (_gelu(a) * w).astype(o_ref.dtype)


def _peer_act(h2, u, s1, e1, s2, e2, thr):
    t, d = h2.shape
    ne = u.shape[0]
    tm, te, sub = _pick(t, 512), 512, 256
    once = dict(pipeline_mode=pl.Buffered(1))
    tok = pl.BlockSpec((PEER_HEADS, PEER_NKEYS, tm), lambda i, j: (0, 0, i), **once)
    return pl.pallas_call(
        functools.partial(_peer_act_kernel, tm=tm, te=te, sub=sub), grid=(t // tm, ne // te),
        in_specs=[pl.BlockSpec((tm, d), lambda i, j: (i, 0), **once),
                  pl.BlockSpec((te, d), lambda i, j: (j, 0)),
                  tok, tok, tok, tok, pl.BlockSpec((PEER_HEADS, tm), lambda i, j: (0, i), **once)],
        out_specs=pl.BlockSpec((tm, te), lambda i, j: (i, j)),
        out_shape=jax.ShapeDtypeStruct((t, ne), BF16),
        compiler_params=_cparams(("parallel", "arbitrary")), name="peer_act")(h2, u, s1, e1, s2, e2, thr)


def _pack_w_in(w_in, d):
    offs = np.cumsum([0, DSA_HEADS * 128, DSA_KV_HEADS * 128, DSA_KV_HEADS * 128, IDX_HEADS * IDX_DIM, IDX_DIM,
                      IDX_HEADS, NSA_HEADS * 128, 6 * NSA_KV_HEADS * 128, 3 * NSA_HEADS, 2 * d])
    qa, ka, va, qi, ki, wi, qb, kvb, gb, gm = [w_in[:, int(offs[n]):int(offs[n + 1])] for n in range(10)]
    main = jnp.concatenate([qa, qi, qb, kvb, ka, va, gm], axis=1).astype(BF16)
    pad = lambda n: jnp.zeros((d, n), w_in.dtype)
    tail = jnp.concatenate([ki, wi, pad(128 - IDX_DIM - IDX_HEADS), gb, pad(128 - 3 * NSA_HEADS)], axis=1).astype(BF16)
    return main, tail


def _token_mixer(h, table, w_in, gq_a, gk_a, gq_b, gk_cmp, gk_slc, gk_win, cmp_pe_k, cmp_w1_k, cmp_w2_k,
                 cmp_pe_v, cmp_w1_v, cmp_w2_v, w_branch_a, w_branch_b, w_out):
    t, d = h.shape
    w_main, w_tail = _pack_w_in(w_in, d)
    pm = _matmul(h, w_main, name="proj_main")
    pt = _matmul(h, w_tail, name="proj_tail")
    qa_h, qi_h, qb_h, kvb_h, ka_h, va_h, ki, wi, gsig = _prep(pm, pt, gq_a, gk_a, gq_b, gk_slc, gk_win)
    tq, tk = _pick(t, 256), _pick(t, 512)
    keys, thr = _dsa_index(qi_h, ki.T, wi, min(DSA_TOPK, t // 4))
    bias_a = _bias_tiles(table[:, :DSA_HEADS], tq, tk)
    y_a = _flash("dsa", qa_h, ka_h, va_h, bias_a, (keys, thr), BF16)
    g = NSA_KV_HEADS
    kc = _compress(kvb_h, 0, cmp_pe_k, cmp_w1_k, cmp_w2_k, gk_cmp, True)
    vc = _compress(kvb_h, g, cmp_pe_v, cmp_w1_v, cmp_w2_v, gk_cmp, False)
    o_c, sel = _cmp_attn(table[:, DSA_HEADS:], qb_h, kc, vc)
    bias_b = _bias_tiles(table[:, DSA_HEADS:], tq, tk)
    ns = t // SLC_LEN
    expand = jnp.asarray(np.arange(ns)[:, None] == (np.arange(t)[None, :] // SLC_LEN), BF16)
    o_s = _flash("slc", qb_h, kvb_h[2 * g:3 * g], kvb_h[3 * g:4 * g], bias_b, (sel, expand), F32)
    o_w = _flash("win", qb_h, kvb_h[4 * g:5 * g], kvb_h[5 * g:6 * g], bias_b, (), F32)
    y_b = _combine(o_c, o_s, o_w, gsig)
    za = _matmul(y_a, w_branch_a.astype(BF16), name="branch_a")
    zb = _matmul(y_b, w_branch_b.astype(BF16), name="branch_b")
    merged = _merge(za, zb, pm)
    return _matmul(merged, w_out.astype(BF16), name="w_out")


def _peer_ffn(h2, x1, gt2, w_q, sub_keys, u, v):
    qp = _matmul(h2, w_q.astype(BF16), name="peer_q")
    s1, e1, s2, e2, thr = _peer_score(qp, sub_keys)
    act = _peer_act(h2, u.astype(BF16), s1, e1, s2, e2, thr)
    return _matmul_resid(act, v.astype(BF16), x1, gt2, name="peer_out")


def kernel(x, c, rel_bias, w_ada, b_ada, g_mix, w_in, gq_a, gk_a, gq_b, gk_cmp, gk_slc, gk_win, cmp_pe_k, cmp_w1_k,
           cmp_w2_k, cmp_pe_v, cmp_w1_v, cmp_w2_v, w_branch_a, w_branch_b, w_out, g_ffn, w_peer_q, peer_sub_keys,
           peer_u, peer_v):
    bsz, t, d = x.shape
    assert bsz == 1 and t % 512 == 0 and d % 128 == 0
    xs = x[0]
    for i in range(w_ada.shape[0]):
        mod = _ada(c, w_ada[i], b_ada[i])
        sh1, sc1, gt1, sh2, sc2, gt2 = [mod[:, n * d:(n + 1) * d] for n in range(6)]
        h = _normmod(xs, g_mix[i], sc1, sh1)
        mix = _token_mixer(h, rel_bias, w_in[i], gq_a[i], gk_a[i], gq_b[i], gk_cmp[i], gk_slc[i], gk_win[i],
                           cmp_pe_k[i], cmp_w1_k[i], cmp_w2_k[i], cmp_pe_v[i], cmp_w1_v[i], cmp_w2_v[i],
                           w_branch_a[i], w_branch_b[i], w_out[i])
        x1, h2 = _resid_normmod(xs, mix, gt1, g_ffn[i], sc2, sh2)
        xs = _peer_ffn(h2, x1, gt2, w_peer_q[i], peer_sub_keys[i], peer_u[i], peer_v[i])
    return xs[None]
```

```python
import functools
import math

import numpy as np
import jax
import jax.numpy as jnp
from jax import lax
from jax.experimental import pallas as pl
from jax.experimental.pallas import tpu as pltpu

HEAD_DIM = 128
DSA_HEADS = 16
DSA_KV_HEADS = 4
IDX_HEADS = 32
IDX_DIM = 64
DSA_TOPK = 256
NSA_HEADS = 16
NSA_KV_HEADS = 4
CMP_LEN = 32
CMP_STRIDE = 16
CMP_HIDDEN = 256
SLC_LEN = 64
SLC_TOP = 16
WIN = 512
FORCE_SCORE = 1e9
PEER_HEADS = 8
PEER_NKEYS = 128
PEER_QDIM = 256
PEER_TOPK = 16
REL_BUCKETS = 32
REL_MAX_DIST = 2048
EPS = 1e-6

LANE = 128
VMEM_LIMIT = 56 * 1024 * 1024
NEG = -1e30
LOG2E = math.log2(math.e)
INT_MIN = -2 ** 31
KEY_NEG_INF = -2 ** 31 + 0x007FFFFF

OFF_QA, OFF_QI, OFF_QB, OFF_KVB, OFF_KA, OFF_VA, OFF_GM = 0, 2048, 4096, 6144, 9216, 9728, 10240

F32 = jnp.float32
BF16 = jnp.bfloat16


def _bucket_starts():
    n = np.arange(2 * REL_MAX_DIST)
    exact = REL_BUCKETS // 2
    nf = np.maximum(n, 1).astype(np.float32)
    lb = exact + (np.log(nf / np.float32(exact)) / np.float32(math.log(REL_MAX_DIST / exact))
                  * np.float32(REL_BUCKETS - exact)).astype(np.int32)
    bucket = np.where(n < exact, n, np.minimum(lb, REL_BUCKETS - 1))
    return [int(np.argmax(bucket >= b)) for b in range(REL_BUCKETS)]


BUCKET_STARTS = _bucket_starts()
FAR_DIST = BUCKET_STARTS[REL_BUCKETS - 1]


def _cparams(sem):
    return pltpu.CompilerParams(dimension_semantics=sem, vmem_limit_bytes=VMEM_LIMIT)


def _pick(n, pref):
    t = pref
    while n % t:
        t //= 2
    return t


def _to_key(x):
    b = lax.bitcast_convert_type(x, jnp.int32)
    return b ^ ((b >> 31) & 0x7FFFFFFF)


def _from_key(k):
    return lax.bitcast_convert_type(k ^ ((k >> 31) & 0x7FFFFFFF), F32)


def _kth_key(count_ge, shape, kk):
    def bit_pass(b, x):
        cand = x + lax.shift_left(jnp.int32(1), 31 - b)
        return jnp.where(count_ge(cand) >= kk, cand, x)
    x = lax.fori_loop(0, 32, bit_pass, jnp.full(shape, INT_MIN, jnp.int32))
    return jnp.maximum(x, KEY_NEG_INF)


def _kth_key_val(keys, kk, axis=-1):
    def count_ge(cand):
        return jnp.sum(jnp.where(keys >= cand, 1.0, 0.0), axis=axis, keepdims=True)
    shape = list(keys.shape)
    shape[axis] = 1
    return _kth_key(count_ge, tuple(shape), float(kk))


def _gelu(x):
    return 0.5 * x * (1.0 + jnp.tanh(math.sqrt(2.0 / math.pi) * (x + 0.044715 * (x * x * x))))


def _dot_nt(a, b):
    return lax.dot_general(a, b, (((1,), (1,)), ((), ())), preferred_element_type=F32)


def _head_norm(x, g, scale):
    ms = jnp.mean(x * x, axis=-1, keepdims=True)
    return x * lax.rsqrt(ms + EPS) * g * scale


def _mm_kernel(a_ref, b_ref, o_ref):
    o_ref[...] = jnp.dot(a_ref[...], b_ref[...], preferred_element_type=F32).astype(o_ref.dtype)


def _matmul(a, b, out_dtype=F32, tm=1024, tn=512, name="matmul"):
    m, k = a.shape
    n = b.shape[1]
    tm, tn = _pick(m, tm), _pick(n, tn)
    return pl.pallas_call(
        _mm_kernel, grid=(m // tm, n // tn),
        in_specs=[pl.BlockSpec((tm, k), lambda i, j: (i, 0)), pl.BlockSpec((k, tn), lambda i, j: (0, j))],
        out_specs=pl.BlockSpec((tm, tn), lambda i, j: (i, j)),
        out_shape=jax.ShapeDtypeStruct((m, n), out_dtype),
        compiler_params=_cparams(("parallel", "arbitrary")), name=name)(a, b)


def _mm_resid_kernel(a_ref, b_ref, x_ref, gt_ref, o_ref):
    k = pl.program_id(2)
    part = jnp.dot(a_ref[...], b_ref[...], preferred_element_type=F32)

    @pl.when(k == 0)
    def _():
        o_ref[...] = part

    @pl.when(k > 0)
    def _():
        o_ref[...] += part

    @pl.when(k == pl.num_programs(2) - 1)
    def _():
        o_ref[...] = x_ref[...] + gt_ref[...] * o_ref[...]


def _matmul_resid(a, b, x, gt, name):
    m, kd = a.shape
    n = b.shape[1]
    tm, tn, tk = _pick(m, 1024), _pick(n, 1024), _pick(kd, 2048)
    return pl.pallas_call(
        _mm_resid_kernel, grid=(m // tm, n // tn, kd // tk),
        in_specs=[pl.BlockSpec((tm, tk), lambda i, j, k: (i, k)), pl.BlockSpec((tk, tn), lambda i, j, k: (k, j)),
                  pl.BlockSpec((tm, tn), lambda i, j, k: (i, j)), pl.BlockSpec((1, tn), lambda i, j, k: (0, j))],
        out_specs=pl.BlockSpec((tm, tn), lambda i, j, k: (i, j)),
        out_shape=jax.ShapeDtypeStruct((m, n), F32),
        compiler_params=_cparams(("parallel", "parallel", "arbitrary")), name=name)(a, b, x, gt)


def _ada_kernel(c_ref, w_ref, b_ref, o_ref):
    o_ref[...] = jnp.dot(c_ref[...], w_ref[...].astype(BF16), preferred_element_type=F32) + b_ref[...]


def _ada(c, w, b):
    d, n = w.shape
    tn = _pick(n, 512)
    c8 = jnp.broadcast_to(c.astype(BF16), (8, d))
    out = pl.pallas_call(
        _ada_kernel, grid=(n // tn,),
        in_specs=[pl.BlockSpec((8, d), lambda j: (0, 0)), pl.BlockSpec((d, tn), lambda j: (0, j)),
                  pl.BlockSpec((1, tn), lambda j: (0, j))],
        out_specs=pl.BlockSpec((8, tn), lambda j: (0, j)),
        out_shape=jax.ShapeDtypeStruct((8, n), F32),
        compiler_params=_cparams(("arbitrary",)), name="ada")(c8, w, b.reshape(1, n))
    return out[0:1]


def _normmod_kernel(x_ref, g_ref, sc_ref, sh_ref, o_ref):
    x = x_ref[...]
    y = x * lax.rsqrt(jnp.mean(x * x, axis=-1, keepdims=True) + EPS) * g_ref[...]
    o_ref[...] = (y * (1.0 + sc_ref[...]) + sh_ref[...]).astype(o_ref.dtype)


def _normmod(x, g, sc, sh):
    t, d = x.shape
    tm = _pick(t, 256)
    row = pl.BlockSpec((1, d), lambda i: (0, 0))
    return pl.pallas_call(
        _normmod_kernel, grid=(t // tm,),
        in_specs=[pl.BlockSpec((tm, d), lambda i: (i, 0)), row, row, row],
        out_specs=pl.BlockSpec((tm, d), lambda i: (i, 0)),
        out_shape=jax.ShapeDtypeStruct((t, d), BF16),
        compiler_params=_cparams(("parallel",)), name="normmod")(x, g.reshape(1, d), sc, sh)


def _resid_normmod_kernel(x_ref, o_ref, gt_ref, g_ref, sc_ref, sh_ref, x1_ref, h_ref):
    x1 = x_ref[...] + gt_ref[...] * o_ref[...]
    x1_ref[...] = x1
    y = x1 * lax.rsqrt(jnp.mean(x1 * x1, axis=-1, keepdims=True) + EPS) * g_ref[...]
    h_ref[...] = (y * (1.0 + sc_ref[...]) + sh_ref[...]).astype(h_ref.dtype)


def _resid_normmod(x, o, gt, g, sc, sh):
    t, d = x.shape
    tm = _pick(t, 256)
    row = pl.BlockSpec((1, d), lambda i: (0, 0))
    blk = pl.BlockSpec((tm, d), lambda i: (i, 0))
    return pl.pallas_call(
        _resid_normmod_kernel, grid=(t // tm,),
        in_specs=[blk, blk, row, row, row, row],
        out_specs=[blk, blk],
        out_shape=[jax.ShapeDtypeStruct((t, d), F32), jax.ShapeDtypeStruct((t, d), BF16)],
        compiler_params=_cparams(("parallel",)), name="resid_normmod")(x, o, gt, g.reshape(1, d), sc, sh)


def _prep_kernel(qa_ref, qi_ref, qb_ref, kvb_ref, ka_ref, va_ref, tail_ref,
                 gqa_ref, gka_ref, gqb_ref, gslc_ref, gwin_ref,
                 qa_o, qi_o, qb_o, kvb_o, ka_o, va_o, ki_o, wi_o, gs_o):
    qscale = HEAD_DIM ** -0.5 * LOG2E
    for h in range(DSA_HEADS):
        qa_o[h] = _head_norm(qa_ref[:, h * 128:(h + 1) * 128], gqa_ref[...], qscale).astype(BF16)
    for h in range(NSA_HEADS):
        qb_o[h] = _head_norm(qb_ref[:, h * 128:(h + 1) * 128], gqb_ref[...], qscale).astype(BF16)
    for h in range(IDX_HEADS):
        qi_o[h] = (qi_ref[:, h * IDX_DIM:(h + 1) * IDX_DIM] * IDX_DIM ** -0.5).astype(BF16)
    for g in range(DSA_KV_HEADS):
        ka_o[g] = _head_norm(ka_ref[:, g * 128:(g + 1) * 128], gka_ref[...], 1.0).astype(BF16)
        va_o[g] = va_ref[:, g * 128:(g + 1) * 128].astype(BF16)
    for s in range(6 * NSA_KV_HEADS):
        xs = kvb_ref[:, s * 128:(s + 1) * 128]
        part = s // NSA_KV_HEADS
        if part == 2:
            xs = _head_norm(xs, gslc_ref[...], 1.0)
        elif part == 4:
            xs = _head_norm(xs, gwin_ref[...], 1.0)
        kvb_o[s] = xs.astype(BF16)
    tail = tail_ref[...]
    ki_o[...] = tail[:, 0:IDX_DIM].astype(BF16)
    wi_o[...] = tail[:, 0:128] * IDX_HEADS ** -0.5
    gs_o[...] = jax.nn.sigmoid(tail[:, 128:256])


def _prep(pm, pt, gq_a, gk_a, gq_b, gk_slc, gk_win):
    t = pm.shape[0]
    tm = _pick(t, 128)
    g128 = pl.BlockSpec((1, 128), lambda i: (0, 0))

    def col(w, off):
        assert off % w == 0
        return pl.BlockSpec((tm, w), lambda i: (i, off // w))

    def hm(n, w):
        return pl.BlockSpec((n, tm, w), lambda i: (0, i, 0))

    row = lambda w: pl.BlockSpec((tm, w), lambda i: (i, 0))
    outs = pl.pallas_call(
        _prep_kernel, grid=(t // tm,),
        in_specs=[col(2048, OFF_QA), col(2048, OFF_QI), col(2048, OFF_QB), col(3072, OFF_KVB),
                  col(512, OFF_KA), col(512, OFF_VA), row(256), g128, g128, g128, g128, g128],
        out_specs=[hm(DSA_HEADS, 128), hm(IDX_HEADS, IDX_DIM), hm(NSA_HEADS, 128), hm(24, 128),
                   hm(DSA_KV_HEADS, 128), hm(DSA_KV_HEADS, 128), row(IDX_DIM), row(128), row(128)],
        out_shape=[jax.ShapeDtypeStruct((DSA_HEADS, t, 128), BF16),
                   jax.ShapeDtypeStruct((IDX_HEADS, t, IDX_DIM), BF16),
                   jax.ShapeDtypeStruct((NSA_HEADS, t, 128), BF16),
                   jax.ShapeDtypeStruct((24, t, 128), BF16),
                   jax.ShapeDtypeStruct((DSA_KV_HEADS, t, 128), BF16),
                   jax.ShapeDtypeStruct((DSA_KV_HEADS, t, 128), BF16),
                   jax.ShapeDtypeStruct((t, IDX_DIM), BF16),
                   jax.ShapeDtypeStruct((t, 128), F32),
                   jax.ShapeDtypeStruct((t, 128), F32)],
        compiler_params=_cparams(("parallel",)), name="prep")(
            pm, pm, pm, pm, pm, pm, pt,
            gq_a.reshape(1, 128), gk_a.reshape(1, 128), gq_b.reshape(1, 128),
            gk_slc.reshape(1, 128), gk_win.reshape(1, 128))
    return outs


def _bias_from_dist(dist, tbl_ref, h, nheads):
    val = jnp.full(dist.shape, tbl_ref[h], F32)
    for b in range(1, REL_BUCKETS):
        val = jnp.where(dist >= BUCKET_STARTS[b], tbl_ref[b * nheads + h], val)
    return (val - tbl_ref[(REL_BUCKETS - 1) * nheads + h]) * LOG2E


def _bias_tiles_kernel(tbl_ref, o_ref, *, tq, tk, nheads):
    h, d = pl.program_id(0), pl.program_id(1)
    dist = d * tq + lax.broadcasted_iota(jnp.int32, (tq, tk), 0) - lax.broadcasted_iota(jnp.int32, (tq, tk), 1)
    o_ref[0, 0] = _bias_from_dist(dist, tbl_ref, h, nheads)


def _n_near(tq, tk):
    return -(-(FAR_DIST + tk - 1) // tq)


def _bias_tiles(table, tq, tk):
    nheads = table.shape[1]
    nd = _n_near(tq, tk)
    return pl.pallas_call(
        functools.partial(_bias_tiles_kernel, tq=tq, tk=tk, nheads=nheads), grid=(nheads, nd),
        in_specs=[pl.BlockSpec(memory_space=pltpu.SMEM)],
        out_specs=pl.BlockSpec((1, 1, tq, tk), lambda h, d: (h, d, 0, 0)),
        out_shape=jax.ShapeDtypeStruct((nheads, nd, tq, tk), F32),
        compiler_params=_cparams(("parallel", "parallel")), name="bias_tiles")(table.reshape(-1))


def _idx_kernel(q_ref, kt_ref, w_ref, key_ref, thr_ref, s_ref, wb_ref, *, tq, tk, ts, t, kk):
    i = pl.program_id(0)
    nkt = (i * tq + tq + tk - 1) // tk
    w = w_ref[...]
    for h in range(IDX_HEADS):
        wb_ref[h] = jnp.broadcast_to(w[:, IDX_DIM + h:IDX_DIM + h + 1], (tq, LANE))
    qall = q_ref[...].reshape(IDX_HEADS * tq, IDX_DIM)
    rows = lax.broadcasted_iota(jnp.int32, (tq, ts), 0)
    cols = lax.broadcasted_iota(jnp.int32, (tq, ts), 1)

    def tile(j, carry):
        offs = [pl.multiple_of(j * tk + u * ts, ts) for u in range(tk // ts)]
        for u, off in enumerate(offs):
            s_ref[u] = jnp.dot(qall, kt_ref[:, pl.ds(off, ts)], preferred_element_type=F32)
        for u, off in enumerate(offs):
            acc = jnp.zeros((tq, ts), F32)
            for h in range(IDX_HEADS):
                wbh = jnp.concatenate([wb_ref[h]] * (ts // LANE), axis=1)
                acc = acc + jnp.maximum(s_ref[u, h * tq:(h + 1) * tq, :], 0.0) * wbh
            acc = jnp.where(off + cols <= i * tq + rows, acc, -jnp.inf)
            key_ref[:, pl.ds(off, ts)] = _to_key(acc)
        return carry

    lax.fori_loop(0, nkt, tile, 0)

    def fill(j, carry):
        key_ref[:, pl.ds(pl.multiple_of(j * tk, tk), tk)] = jnp.full((tq, tk), KEY_NEG_INF, jnp.int32)
        return carry

    lax.fori_loop(nkt, t // tk, fill, 0)

    def count_ge(cand):
        def body(j, c):
            kt = key_ref[:, pl.ds(pl.multiple_of(j * tk, tk), tk)]
            ge = jnp.where(kt >= cand, 1.0, 0.0)
            for a in range(tk // LANE):
                c = c + ge[:, a * LANE:(a + 1) * LANE]
            return c
        c = lax.fori_loop(0, nkt, body, jnp.zeros((tq, LANE), F32))
        return jnp.sum(c, axis=-1, keepdims=True)

    thr_ref[...] = _kth_key(count_ge, (tq, 1), float(kk))


def _dsa_index(qi_h, ki_t, wi, kk):
    t = ki_t.shape[1]
    tq, tk = _pick(t, 128), _pick(t, 512)
    ts = tk // 2
    return pl.pallas_call(
        functools.partial(_idx_kernel, tq=tq, tk=tk, ts=ts, t=t, kk=kk), grid=(t // tq,),
        in_specs=[pl.BlockSpec((IDX_HEADS, tq, IDX_DIM), lambda i: (0, i, 0)),
                  pl.BlockSpec((IDX_DIM, t), lambda i: (0, 0)),
                  pl.BlockSpec((tq, 128), lambda i: (i, 0))],
        out_specs=[pl.BlockSpec((tq, t), lambda i: (i, 0)), pl.BlockSpec((tq, 1), lambda i: (i, 0))],
        out_shape=[jax.ShapeDtypeStruct((t, t), jnp.int32), jax.ShapeDtypeStruct((t, 1), jnp.int32)],
        scratch_shapes=[pltpu.VMEM((tk // ts, IDX_HEADS * tq, ts), F32), pltpu.VMEM((IDX_HEADS, tq, LANE), F32)],
        compiler_params=_cparams(("parallel",)), name="dsa_index")(qi_h, ki_t, wi)


def _kv_tile(mode, i, s, tq, tk):
    if mode == "win":
        return jnp.maximum(i * tq - (WIN - 1), 0) // tk + s
    return s


def _last_tile(i, tq, tk):
    return (i * tq + tq - 1) // tk


def _flash_kernel(*refs, mode, tq, tk, nheads, ngroups, nnear):
    if mode == "dsa":
        q_ref, k_ref, v_ref, b_ref, key_ref, thr_ref, o_ref, m_ref, l_ref, acc_ref = refs
    elif mode == "slc":
        q_ref, k_ref, v_ref, b_ref, sel_ref, e_ref, o_ref, m_ref, l_ref, acc_ref = refs
    else:
        q_ref, k_ref, v_ref, b_ref, o_ref, m_ref, l_ref, acc_ref = refs
    rep = nheads // ngroups
    nch = tk // LANE
    i, s = pl.program_id(0), pl.program_id(1)
    j = _kv_tile(mode, i, s, tq, tk)
    jlast = _last_tile(i, tq, tk)
    d = i - (tk // tq) * j

    @pl.when(s == 0)
    def _():
        m_ref[...] = jnp.full(m_ref.shape, NEG, F32)
        l_ref[...] = jnp.zeros(l_ref.shape, F32)
        acc_ref[...] = jnp.zeros(acc_ref.shape, F32)

    def step(near):
        if near:
            dist = ((i * tq - j * tk) + lax.broadcasted_iota(jnp.int32, (tq, tk), 0)
                    - lax.broadcasted_iota(jnp.int32, (tq, tk), 1))
            causal = dist >= 0
            if mode == "win":
                causal = causal & (dist < WIN)
        if mode == "dsa":
            hit = key_ref[...] >= thr_ref[...]
            madd = jnp.where((hit & causal) if near else hit, 0.0, NEG)
        elif mode == "win":
            madd = jnp.where(causal, 0.0, NEG)
        for g in range(ngroups):
            s_all = _dot_nt(q_ref[g * rep:(g + 1) * rep].reshape(rep * tq, HEAD_DIM), k_ref[g])
            if mode == "slc":
                hit = jnp.dot(sel_ref[g], e_ref[...], preferred_element_type=F32) > 0.5
                madd = jnp.where((hit & causal) if near else hit, 0.0, NEG)
            ps = []
            for r in range(rep):
                h = g * rep + r
                sh = s_all[r * tq:(r + 1) * tq] + madd
                if near:
                    sh = sh + b_ref[h, 0]
                chunks = [sh[:, c * LANE:(c + 1) * LANE] for c in range(nch)]
                m_old = m_ref[h]
                tile_max = jnp.max(functools.reduce(jnp.maximum, chunks), axis=-1, keepdims=True)
                m_new = jnp.maximum(m_old, tile_max)
                alpha = jnp.exp2(m_old - m_new)
                pcs = [jnp.exp2(c - m_new) for c in chunks]
                l_ref[h] = alpha * l_ref[h] + functools.reduce(jnp.add, pcs)
                m_ref[h] = m_new
                acc_ref[h] = acc_ref[h] * alpha
                ps.append(jnp.concatenate(pcs, axis=1).astype(BF16))
            pv = jnp.dot(jnp.concatenate(ps, axis=0), v_ref[g], preferred_element_type=F32)
            for r in range(rep):
                h = g * rep + r
                acc_ref[h] = acc_ref[h] + pv[r * tq:(r + 1) * tq]

    valid = j <= jlast
    if mode == "win":
        pl.when(valid)(lambda: step(True))
    else:
        pl.when(jnp.logical_and(valid, d < nnear))(lambda: step(True))
        pl.when(jnp.logical_and(valid, d >= nnear))(lambda: step(False))

    @pl.when(j == jlast)
    def _():
        for h in range(nheads):
            l = jnp.sum(l_ref[h], axis=-1, keepdims=True)
            o_ref[:, h * HEAD_DIM:(h + 1) * HEAD_DIM] = (acc_ref[h] / l).astype(o_ref.dtype)


def _flash(mode, q_h, k_h, v_h, bias, extra, out_dtype):
    nheads, t, _ = q_h.shape
    ngroups = k_h.shape[0]
    _, nnear, tq, tk = bias.shape
    nq = t // tq
    if mode == "win":
        nsteps = max((i * tq + tq - 1) // tk - max(i * tq - (WIN - 1), 0) // tk + 1 for i in range(nq))
    else:
        nsteps = t // tk

    def jmap(i, s):
        return jnp.minimum(_kv_tile(mode, i, s, tq, tk), _last_tile(i, tq, tk))

    def bmap(i, s):
        return jnp.minimum(i - (tk // tq) * jmap(i, s), nnear - 1)

    in_specs = [pl.BlockSpec((nheads, tq, HEAD_DIM), lambda i, s: (0, i, 0)),
                pl.BlockSpec((ngroups, tk, HEAD_DIM), lambda i, s: (0, jmap(i, s), 0)),
                pl.BlockSpec((ngroups, tk, HEAD_DIM), lambda i, s: (0, jmap(i, s), 0)),
                pl.BlockSpec((nheads, 1, tq, tk), lambda i, s: (0, bmap(i, s), 0, 0))]
    if mode == "dsa":
        in_specs += [pl.BlockSpec((tq, tk), lambda i, s: (i, jmap(i, s))),
                     pl.BlockSpec((tq, 1), lambda i, s: (i, 0))]
    elif mode == "slc":
        ns = extra[0].shape[2]
        in_specs += [pl.BlockSpec((ngroups, tq, ns), lambda i, s: (0, i, 0)),
                     pl.BlockSpec((ns, tk), lambda i, s: (0, jmap(i, s)))]
    return pl.pallas_call(
        functools.partial(_flash_kernel, mode=mode, tq=tq, tk=tk, nheads=nheads, ngroups=ngroups, nnear=nnear),
        grid=(nq, nsteps), in_specs=in_specs,
        out_specs=pl.BlockSpec((tq, nheads * HEAD_DIM), lambda i, s: (i, 0)),
        out_shape=jax.ShapeDtypeStruct((t, nheads * HEAD_DIM), out_dtype),
        scratch_shapes=[pltpu.VMEM((nheads, tq, LANE), F32), pltpu.VMEM((nheads, tq, LANE), F32),
                        pltpu.VMEM((nheads, tq, HEAD_DIM), F32)],
        compiler_params=_cparams(("parallel", "arbitrary")), name="flash_" + mode)(q_h, k_h, v_h, bias, *extra)


def _compress_kernel(c_ref, w1_ref, pe_ref, w2_ref, g_ref, o_ref, *, norm):
    c = c_ref[0]
    half = CMP_STRIDE * HEAD_DIM
    ncp = c.shape[0]
    a = jnp.dot(c, w1_ref[0:half], preferred_element_type=F32)
    b = jnp.dot(c, w1_ref[half:2 * half], preferred_element_type=F32)
    pet = jnp.dot(pe_ref[...], w1_ref[...], preferred_element_type=F32)[0:1]
    hid = _gelu(a + pltpu.roll(b, ncp - 1, 0) + pet)
    y = jnp.dot(hid.astype(BF16), w2_ref[...], preferred_element_type=F32)
    if norm:
        y = _head_norm(y, g_ref[...], 1.0)
    o_ref[0] = y.astype(BF16)


def _compress(slabs, base, pe, w1, w2, gain, norm):
    _, t, _ = slabs.shape
    ncp = t // CMP_STRIDE
    width = CMP_STRIDE * HEAD_DIM
    chunks = slabs.reshape(slabs.shape[0], ncp, width)
    w1f = w1.reshape(CMP_LEN * HEAD_DIM, CMP_HIDDEN).astype(BF16)
    pe8 = jnp.broadcast_to(pe.reshape(1, CMP_LEN * HEAD_DIM).astype(BF16), (8, CMP_LEN * HEAD_DIM))
    full = lambda shp: pl.BlockSpec(shp, lambda g: (0,) * len(shp))
    return pl.pallas_call(
        functools.partial(_compress_kernel, norm=norm), grid=(NSA_KV_HEADS,),
        in_specs=[pl.BlockSpec((1, ncp, width), lambda g: (base + g, 0, 0)),
                  full((CMP_LEN * HEAD_DIM, CMP_HIDDEN)), full((8, CMP_LEN * HEAD_DIM)),
                  full((CMP_HIDDEN, HEAD_DIM)), full((1, HEAD_DIM))],
        out_specs=pl.BlockSpec((1, ncp, HEAD_DIM), lambda g: (g, 0, 0)),
        out_shape=jax.ShapeDtypeStruct((NSA_KV_HEADS, ncp, HEAD_DIM), BF16),
        compiler_params=_cparams(("parallel",)), name="compress")(
            chunks, w1f, pe8, w2.astype(BF16), gain.reshape(1, HEAD_DIM))


def _cmp_attn_kernel(tbl_ref, q_ref, kc_ref, vc_ref, ov_ref, oc_ref, sel_ref, s_ref, key_ref,
                     *, tq, ncp, ns, nsel, ww):
    i = pl.program_id(0)
    rep = NSA_HEADS // NSA_KV_HEADS
    qpos = i * tq + lax.broadcasted_iota(jnp.int32, (tq, ncp), 0)
    dist = qpos - (lax.broadcasted_iota(jnp.int32, (tq, ncp), 1) * CMP_STRIDE + CMP_LEN - 1)
    mask = dist >= 0
    w0 = jnp.maximum(i * tq - (CMP_LEN - 1) - FAR_DIST + CMP_STRIDE, 0) // (CMP_STRIDE * LANE) * LANE
    w0 = pl.multiple_of(jnp.minimum(w0, ncp - ww), LANE)
    dist_w = (i * tq + lax.broadcasted_iota(jnp.int32, (tq, ww), 0)
              - ((w0 + lax.broadcasted_iota(jnp.int32, (tq, ww), 1)) * CMP_STRIDE + CMP_LEN - 1))
    bucket = jnp.zeros((tq, ww), jnp.int32)
    for b in range(1, REL_BUCKETS):
        bucket = jnp.where(dist_w >= BUCKET_STARTS[b], b, bucket)
    blk = lax.broadcasted_iota(jnp.int32, (tq, ns), 1)
    qp = i * tq + lax.broadcasted_iota(jnp.int32, (tq, ns), 0)
    cur = qp // SLC_LEN
    forced = (blk == 0) | (blk == cur) | (blk == cur - 1)
    admissible = blk * SLC_LEN <= qp
    for g in range(NSA_KV_HEADS):
        s_ref[...] = _dot_nt(q_ref[g * rep:(g + 1) * rep].reshape(rep * tq, HEAD_DIM), kc_ref[g])
        imp = jnp.zeros((tq, ncp), F32)
        ps = []
        for r in range(rep):
            h = g * rep + r
            trow = (tbl_ref[h:h + 1, :] - tbl_ref[h:h + 1, REL_BUCKETS - 1:REL_BUCKETS]) * LOG2E
            trow = jnp.broadcast_to(trow, (tq, LANE))
            bias = [jnp.take_along_axis(trow, bucket[:, c * LANE:(c + 1) * LANE], axis=1)
                    for c in range(ww // LANE)]
            s_ref[r * tq:(r + 1) * tq, pl.ds(w0, ww)] += jnp.concatenate(bias, axis=1)
            sh = jnp.where(mask, s_ref[r * tq:(r + 1) * tq, :], NEG)
            m = jnp.max(sh, axis=-1, keepdims=True)
            p = jnp.where(mask, jnp.exp2(sh - m), 0.0)
            pc = p / jnp.maximum(jnp.sum(p, axis=-1, keepdims=True), 1e-30)
            imp = imp + pc
            ps.append(pc.astype(BF16))
        o = jnp.dot(jnp.concatenate(ps, axis=0), vc_ref[g], preferred_element_type=F32)
        for r in range(rep):
            h = g * rep + r
            oc_ref[:, h * HEAD_DIM:(h + 1) * HEAD_DIM] = o[r * tq:(r + 1) * tq]
        hi = imp.astype(BF16)
        lo = (imp - hi.astype(F32)).astype(BF16)
        impb = (jnp.dot(hi, ov_ref[...], preferred_element_type=F32)
                + jnp.dot(lo, ov_ref[...], preferred_element_type=F32))
        impb = jnp.where(forced, FORCE_SCORE, impb)
        key_ref[g] = _to_key(jnp.where(admissible, impb, -jnp.inf))
    keys = key_ref[...].reshape(NSA_KV_HEADS * tq, ns)
    thr = _kth_key_val(keys, nsel)
    sel_ref[...] = jnp.where(keys >= thr, 1.0, 0.0).reshape(NSA_KV_HEADS, tq, ns).astype(BF16)


def _cmp_attn(table, q_h, kc, vc):
    nheads, t, _ = q_h.shape
    ncp = kc.shape[1]
    ns = t // SLC_LEN
    nsel = min(SLC_TOP, ns)
    tq = _pick(t, 128)
    ww = min(ncp, LANE * (-(-(FAR_DIST + tq + CMP_STRIDE * LANE) // (CMP_STRIDE * LANE))))
    rep = nheads // NSA_KV_HEADS
    c_start = np.arange(ncp)[:, None] * CMP_STRIDE
    s_start = np.arange(ns)[None, :] * SLC_LEN
    overlap = jnp.asarray((c_start < s_start + SLC_LEN) & (c_start + CMP_LEN > s_start), BF16)
    full = lambda shp: pl.BlockSpec(shp, lambda i: (0,) * len(shp))
    table_t = jnp.pad(table.T, ((0, 0), (0, LANE - REL_BUCKETS)))
    return pl.pallas_call(
        functools.partial(_cmp_attn_kernel, tq=tq, ncp=ncp, ns=ns, nsel=nsel, ww=ww), grid=(t // tq,),
        in_specs=[full((nheads, LANE)),
                  pl.BlockSpec((nheads, tq, HEAD_DIM), lambda i: (0, i, 0)),
                  full((NSA_KV_HEADS, ncp, HEAD_DIM)), full((NSA_KV_HEADS, ncp, HEAD_DIM)), full((ncp, ns))],
        out_specs=[pl.BlockSpec((tq, nheads * HEAD_DIM), lambda i: (i, 0)),
                   pl.BlockSpec((NSA_KV_HEADS, tq, ns), lambda i: (0, i, 0))],
        out_shape=[jax.ShapeDtypeStruct((t, nheads * HEAD_DIM), F32),
                   jax.ShapeDtypeStruct((NSA_KV_HEADS, t, ns), BF16)],
        scratch_shapes=[pltpu.VMEM((rep * tq, ncp), F32), pltpu.VMEM((NSA_KV_HEADS, tq, ns), jnp.int32)],
        compiler_params=_cparams(("parallel",)), name="cmp_attn")(table_t, q_h, kc, vc, overlap)


def _combine_kernel(oc_ref, os_ref, ow_ref, g_ref, y_ref):
    g = g_ref[...]
    for h in range(NSA_HEADS):
        sl = slice(h * HEAD_DIM, (h + 1) * HEAD_DIM)
        y = (g[:, 3 * h:3 * h + 1] * oc_ref[:, sl] + g[:, 3 * h + 1:3 * h + 2] * os_ref[:, sl]
             + g[:, 3 * h + 2:3 * h + 3] * ow_ref[:, sl])
        y_ref[:, sl] = y.astype(y_ref.dtype)


def _combine(oc, os_, ow, gsig):
    t, w = oc.shape
    tm = _pick(t, 256)
    blk = pl.BlockSpec((tm, w), lambda i: (i, 0))
    return pl.pallas_call(
        _combine_kernel, grid=(t // tm,),
        in_specs=[blk, blk, blk, pl.BlockSpec((tm, 128), lambda i: (i, 0))],
        out_specs=blk, out_shape=jax.ShapeDtypeStruct((t, w), BF16),
        compiler_params=_cparams(("parallel",)), name="combine")(oc, os_, ow, gsig)


def _merge_kernel(a_ref, b_ref, ga_ref, gb_ref, o_ref):
    o_ref[...] = (jax.nn.sigmoid(ga_ref[...]) * a_ref[...]
                  + jax.nn.sigmoid(gb_ref[...]) * b_ref[...]).astype(o_ref.dtype)


def _merge(a, b, pm):
    t, d = a.shape
    tm, tn = _pick(t, 256), _pick(d, 512)
    assert OFF_GM % tn == 0
    blk = pl.BlockSpec((tm, tn), lambda i, j: (i, j))
    return pl.pallas_call(
        _merge_kernel, grid=(t // tm, d // tn),
        in_specs=[blk, blk, pl.BlockSpec((tm, tn), lambda i, j: (i, OFF_GM // tn + j)),
                  pl.BlockSpec((tm, tn), lambda i, j: (i, (OFF_GM + d) // tn + j))],
        out_specs=blk, out_shape=jax.ShapeDtypeStruct((t, d), BF16),
        compiler_params=_cparams(("parallel", "parallel")), name="merge")(a, b, pm, pm)


_CAND_FULL = PEER_TOPK // 2


def _peer_cands(a1, a2, op):
    pieces = [op(a1[0:1], a2)]
    pieces += [op(a1[k:k + 1], a2[0:_CAND_FULL]) for k in range(1, _CAND_FULL)]
    pieces.append(op(a1[_CAND_FULL:], a2[0:1]))
    return jnp.concatenate(pieces, axis=0)


def _peer_score_kernel(q_ref, sk_ref, s1_o, e1_o, s2_o, e2_o, thr_o, *, tm):
    row = lax.broadcasted_iota(jnp.int32, (PEER_NKEYS, tm), 0).astype(F32)
    for h in range(PEER_HEADS):
        svals, tops = [], []
        for c in range(2):
            hc = 2 * h + c
            s = _dot_nt(sk_ref[hc], q_ref[:, hc * 128:(hc + 1) * 128].astype(BF16))
            svals.append(s)
            work, top = s, []
            for _ in range(PEER_TOPK):
                mx = jnp.max(work, axis=0, keepdims=True)
                first = jnp.min(jnp.where(work == mx, row, float(PEER_NKEYS)), axis=0, keepdims=True)
                work = jnp.where(row == first, -jnp.inf, work)
                top.append(mx)
            tops.append(jnp.concatenate(top, axis=0))
        a1, a2 = tops
        cand = _peer_cands(a1, a2, jnp.add)
        thr = _from_key(_kth_key_val(_to_key(cand), PEER_TOPK, axis=0))
        m1, m2 = a1[0:1], a2[0:1]
        ec = _peer_cands(jnp.exp(a1 - m1), jnp.exp(a2 - m2), jnp.multiply)
        z = jnp.sum(jnp.where(cand >= thr, ec, 0.0), axis=0, keepdims=True)
        s1_o[h] = svals[0]
        s2_o[h] = svals[1]
        e1_o[h] = jnp.exp(svals[0] - m1)
        e2_o[h] = jnp.exp(svals[1] - m2) / z
        thr_o[h:h + 1, :] = thr


def _peer_score(qp, sub_keys):
    t = qp.shape[0]
    tm = _pick(t, 256)
    sk = sub_keys.reshape(2 * PEER_HEADS, PEER_NKEYS, PEER_QDIM // 2).astype(BF16)
    tr = pl.BlockSpec((PEER_HEADS, PEER_NKEYS, tm), lambda i: (0, 0, i))
    shp = jax.ShapeDtypeStruct((PEER_HEADS, PEER_NKEYS, t), F32)
    return pl.pallas_call(
        functools.partial(_peer_score_kernel, tm=tm), grid=(t // tm,),
        in_specs=[pl.BlockSpec((tm, 2 * PEER_HEADS * 128), lambda i: (i, 0)),
                  pl.BlockSpec((2 * PEER_HEADS, PEER_NKEYS, PEER_QDIM // 2), lambda i: (0, 0, 0))],
        out_specs=[tr, tr, tr, tr, pl.BlockSpec((PEER_HEADS, tm), lambda i: (0, i))],
        out_shape=[shp, shp, shp, shp, jax.ShapeDtypeStruct((PEER_HEADS, t), F32)],
        compiler_params=_cparams(("parallel",)), name="peer_score")(qp, sk)


def _peer_act_kernel(h_ref, u_ref, s1_ref, e1_ref, s2_ref, e2_ref, thr_ref, o_ref, w_ref, a_ref, *, tm, te, sub):
    j = pl.program_id(1)
    n1 = te // PEER_NKEYS
    a_ref[...] = _dot_nt(h_ref[...], u_ref[...])
    for ai in range(n1):
        i1 = j * n1 + ai
        s1rows = [s1_ref[h, pl.ds(i1, 1), :] for h in range(PEER_HEADS)]
        e1rows = [e1_ref[h, pl.ds(i1, 1), :] for h in range(PEER_HEADS)]
        for ts in range(tm // LANE):
            tok = slice(ts * LANE, (ts + 1) * LANE)
            wt = jnp.zeros((PEER_NKEYS, LANE), F32)
            for h in range(PEER_HEADS):
                hit = (s1rows[h][:, tok] + s2_ref[h, :, tok]) >= thr_ref[h:h + 1, tok]
                wt = wt + jnp.where(hit, e1rows[h][:, tok] * e2_ref[h, :, tok], 0.0)
            w_ref[tok, ai * PEER_NKEYS:(ai + 1) * PEER_NKEYS] = wt.T
    for r in range(tm // sub):
        rows = slice(r * sub, (r + 1) * sub)
        o_ref[rows, :] = (_gelu(a_ref[rows, :]) * w_ref[rows, :]).astype(o_ref.dtype)


def _peer_act(h2, u, s1, e1, s2, e2, thr):
    t, d = h2.shape
    ne = u.shape[0]
    tm, te, sub = _pick(t, 1024), 512, 256
    once = dict(pipeline_mode=pl.Buffered(1))
    tok = pl.BlockSpec((PEER_HEADS, PEER_NKEYS, tm), lambda i, j: (0, 0, i), **once)
    return pl.pallas_call(
        functools.partial(_peer_act_kernel, tm=tm, te=te, sub=sub), grid=(t // tm, ne // te),
        in_specs=[pl.BlockSpec((tm, d), lambda i, j: (i, 0), **once),
                  pl.BlockSpec((te, d), lambda i, j: (j, 0)),
                  tok, tok, tok, tok, pl.BlockSpec((PEER_HEADS, tm), lambda i, j: (0, i), **once)],
        out_specs=pl.BlockSpec((tm, te), lambda i, j: (i, j)),
        out_shape=jax.ShapeDtypeStruct((t, ne), BF16),
        scratch_shapes=[pltpu.VMEM((tm, te), F32), pltpu.VMEM((tm, te), F32)],
        compiler_params=_cparams(("parallel", "arbitrary")), name="peer_act")(h2, u, s1, e1, s2, e2, thr)


def _pack_w_in(w_in, d):
    offs = np.cumsum([0, DSA_HEADS * 128, DSA_KV_HEADS * 128, DSA_KV_HEADS * 128, IDX_HEADS * IDX_DIM, IDX_DIM,
                      IDX_HEADS, NSA_HEADS * 128, 6 * NSA_KV_HEADS * 128, 3 * NSA_HEADS, 2 * d])
    qa, ka, va, qi, ki, wi, qb, kvb, gb, gm = [w_in[:, int(offs[n]):int(offs[n + 1])] for n in range(10)]
    main = jnp.concatenate([qa, qi, qb, kvb, ka, va, gm], axis=1).astype(BF16)
    pad = lambda n: jnp.zeros((d, n), w_in.dtype)
    tail = jnp.concatenate([ki, wi, pad(128 - IDX_DIM - IDX_HEADS), gb, pad(128 - 3 * NSA_HEADS)], axis=1).astype(BF16)
    return main, tail


def _token_mixer(h, table, w_in, gq_a, gk_a, gq_b, gk_cmp, gk_slc, gk_win, cmp_pe_k, cmp_w1_k, cmp_w2_k,
                 cmp_pe_v, cmp_w1_v, cmp_w2_v, w_branch_a, w_branch_b, w_out):
    t, d = h.shape
    w_main, w_tail = _pack_w_in(w_in, d)
    pm = _matmul(h, w_main, name="proj_main")
    pt = _matmul(h, w_tail, name="proj_tail")
    qa_h, qi_h, qb_h, kvb_h, ka_h, va_h, ki, wi, gsig = _prep(pm, pt, gq_a, gk_a, gq_b, gk_slc, gk_win)
    tq, tk = _pick(t, 256), _pick(t, 512)
    keys, thr = _dsa_index(qi_h, ki.T, wi, min(DSA_TOPK, t // 4))
    bias_a = _bias_tiles(table[:, :DSA_HEADS], tq, tk)
    y_a = _flash("dsa", qa_h, ka_h, va_h, bias_a, (keys, thr), BF16)
    g = NSA_KV_HEADS
    kc = _compress(kvb_h, 0, cmp_pe_k, cmp_w1_k, cmp_w2_k, gk_cmp, True)
    vc = _compress(kvb_h, g, cmp_pe_v, cmp_w1_v, cmp_w2_v, gk_cmp, False)
    o_c, sel = _cmp_attn(table[:, DSA_HEADS:], qb_h, kc, vc)
    bias_b = _bias_tiles(table[:, DSA_HEADS:], tq, tk)
    ns = t // SLC_LEN
    expand = jnp.asarray(np.arange(ns)[:, None] == (np.arange(t)[None, :] // SLC_LEN), BF16)
    o_s = _flash("slc", qb_h, kvb_h[2 * g:3 * g], kvb_h[3 * g:4 * g], bias_b, (sel, expand), F32)
    o_w = _flash("win", qb_h, kvb_h[4 * g:5 * g], kvb_h[5 * g:6 * g], bias_b, (), F32)
    y_b = _combine(o_c, o_s, o_w, gsig)
    za = _matmul(y_a, w_branch_a.astype(BF16), name="branch_a")
    zb = _matmul(y_b, w_branch_b.astype(BF16), name="branch_b")
    merged = _merge(za, zb, pm)
    return _matmul(merged, w_out.astype(BF16), name="w_out")


def _peer_ffn(h2, x1, gt2, w_q, sub_keys, u, v):
    qp = _matmul(h2, w_q.astype(BF16), name="peer_q")
    s1, e1, s2, e2, thr = _peer_score(qp, sub_keys)
    act = _peer_act(h2, u.astype(BF16), s1, e1, s2, e2, thr)
    return _matmul_resid(act, v.astype(BF16), x1, gt2, name="peer_out")


def kernel(x, c, rel_bias, w_ada, b_ada, g_mix, w_in, gq_a, gk_a, gq_b, gk_cmp, gk_slc, gk_win, cmp_pe_k, cmp_w1_k,
           cmp_w2_k, cmp_pe_v, cmp_w1_v, cmp_w2_v, w_branch_a, w_branch_b, w_out, g_ffn, w_peer_q, peer_sub_keys,
           peer_u, peer_v):
    bsz, t, d = x.shape
    assert bsz == 1 and t % 512 == 0 and d % 128 == 0
    xs = x[0]
    for i in range(w_ada.shape[0]):
        mod = _ada(c, w_ada[i], b_ada[i])
        sh1, sc1, gt1, sh2, sc2, gt2 = [mod[:, n * d:(n + 1) * d] for n in range(6)]
        h = _normmod(xs, g_mix[i], sc1, sh1)
        mix = _token_mixer(h, rel_bias, w_in[i], gq_a[i], gk_a[i], gq_b[i], gk_cmp[i], gk_slc[i], gk_win[i],
                           cmp_pe_k[i], cmp_w1_k[i], cmp_w2_k[i], cmp_pe_v[i], cmp_w1_v[i], cmp_w2_v[i],
                           w_branch_a[i], w_branch_b[i], w_out[i])
        x1, h2 = _resid_normmod(xs, mix, gt1, g_ffn[i], sc2, sh2)
        xs = _peer_ffn(h2, x1, gt2, w_peer_q[i], peer_sub_keys[i], peer_u[i], peer_v[i])
    return xs[None]
```

```python
import functools
import math

import numpy as np
import jax
import jax.numpy as jnp
from jax import lax
from jax.experimental import pallas as pl
from jax.experimental.pallas import tpu as pltpu

HEAD_DIM = 128
DSA_HEADS = 16
DSA_KV_HEADS = 4
IDX_HEADS = 32
IDX_DIM = 64
DSA_TOPK = 256
NSA_HEADS = 16
NSA_KV_HEADS = 4
CMP_LEN = 32
CMP_STRIDE = 16
CMP_HIDDEN = 256
SLC_LEN = 64
SLC_TOP = 16
WIN = 512
FORCE_SCORE = 1e9
PEER_HEADS = 8
PEER_NKEYS = 128
PEER_QDIM = 256
PEER_TOPK = 16
REL_BUCKETS = 32
REL_MAX_DIST = 2048
EPS = 1e-6

LANE = 128
VMEM_LIMIT = 56 * 1024 * 1024
NEG = -1e30
LOG2E = math.log2(math.e)
INT_MIN = -2 ** 31
KEY_NEG_INF = -2 ** 31 + 0x007FFFFF

OFF_QA, OFF_QI, OFF_QB, OFF_KVB, OFF_KA, OFF_VA, OFF_GM = 0, 2048, 4096, 6144, 9216, 9728, 10240

F32 = jnp.float32
BF16 = jnp.bfloat16


def _bucket_starts():
    n = np.arange(2 * REL_MAX_DIST)
    exact = REL_BUCKETS // 2
    nf = np.maximum(n, 1).astype(np.float32)
    lb = exact + (np.log(nf / np.float32(exact)) / np.float32(math.log(REL_MAX_DIST / exact))
                  * np.float32(REL_BUCKETS - exact)).astype(np.int32)
    bucket = np.where(n < exact, n, np.minimum(lb, REL_BUCKETS - 1))
    return [int(np.argmax(bucket >= b)) for b in range(REL_BUCKETS)]


BUCKET_STARTS = _bucket_starts()
FAR_DIST = BUCKET_STARTS[REL_BUCKETS - 1]


def _cparams(sem):
    return pltpu.CompilerParams(dimension_semantics=sem, vmem_limit_bytes=VMEM_LIMIT)


def _pick(n, pref):
    t = pref
    while n % t:
        t //= 2
    return t


def _to_key(x):
    b = lax.bitcast_convert_type(x, jnp.int32)
    return b ^ ((b >> 31) & 0x7FFFFFFF)


def _from_key(k):
    return lax.bitcast_convert_type(k ^ ((k >> 31) & 0x7FFFFFFF), F32)


def _kth_key(count_ge, shape, kk):
    def bit_pass(b, x):
        cand = x + lax.shift_left(jnp.int32(1), 31 - b)
        return jnp.where(count_ge(cand) >= kk, cand, x)
    x = lax.fori_loop(0, 32, bit_pass, jnp.full(shape, INT_MIN, jnp.int32))
    return jnp.maximum(x, KEY_NEG_INF)


def _kth_key_bracket(count_ge, kmin, kmax, kk):
    c_min = count_ge(kmin)
    few = c_min < kk
    lo0 = jnp.where(few, KEY_NEG_INF, kmin)
    hi0 = kmax + 1
    done0 = jnp.where(few | (c_min == kk) | (hi0 - 1 <= lo0), 1.0, 0.0)

    def cond(st):
        p, _, _, _, _, done = st
        return jnp.logical_and(p < 72, jnp.min(done) < 0.5)

    def body(st):
        p, lo, hi, clo, chi, done = st
        mid = (lo >> 1) + (hi >> 1) + (lo & hi & 1)
        flo, fhi = _from_key(lo), _from_key(hi)
        frac = (jnp.log(clo) - math.log(kk)) / (jnp.log(clo) - jnp.log(jnp.maximum(chi, 0.5)))
        cand = jnp.where(p % 2 == 1, mid, _to_key(flo + frac * (fhi - flo)))
        cand = jnp.minimum(jnp.maximum(cand, lo + 1), hi - 1)
        active = done < 0.5
        cand = jnp.where(active, cand, lo)
        cnt = count_ge(cand)
        up = active & (cnt >= kk)
        dn = active & (cnt < kk)
        lo, clo = jnp.where(up, cand, lo), jnp.where(up, cnt, clo)
        hi, chi = jnp.where(dn, cand, hi), jnp.where(dn, cnt, chi)
        done = jnp.where((clo == kk) | (hi - 1 <= lo), 1.0, done)
        return p + 1, lo, hi, clo, chi, done

    st = lax.while_loop(cond, body, (jnp.int32(0), lo0, hi0, c_min, jnp.zeros_like(c_min), done0))
    return st[1]


def _kth_key_val(keys, kk, axis=-1):
    def count_ge(cand):
        return jnp.sum(jnp.where(keys >= cand, 1.0, 0.0), axis=axis, keepdims=True)
    shape = list(keys.shape)
    shape[axis] = 1
    return _kth_key(count_ge, tuple(shape), float(kk))


def _gelu(x):
    return 0.5 * x * (1.0 + jnp.tanh(math.sqrt(2.0 / math.pi) * (x + 0.044715 * (x * x * x))))


def _dot_nt(a, b):
    return lax.dot_general(a, b, (((1,), (1,)), ((), ())), preferred_element_type=F32)


def _head_norm(x, g, scale):
    ms = jnp.mean(x * x, axis=-1, keepdims=True)
    return x * lax.rsqrt(ms + EPS) * g * scale


def _mm_kernel(a_ref, b_ref, o_ref):
    o_ref[...] = jnp.dot(a_ref[...], b_ref[...], preferred_element_type=F32).astype(o_ref.dtype)


def _matmul(a, b, out_dtype=F32, tm=1024, tn=512, name="matmul"):
    m, k = a.shape
    n = b.shape[1]
    tm, tn = _pick(m, tm), _pick(n, tn)
    return pl.pallas_call(
        _mm_kernel, grid=(m // tm, n // tn),
        in_specs=[pl.BlockSpec((tm, k), lambda i, j: (i, 0)), pl.BlockSpec((k, tn), lambda i, j: (0, j))],
        out_specs=pl.BlockSpec((tm, tn), lambda i, j: (i, j)),
        out_shape=jax.ShapeDtypeStruct((m, n), out_dtype),
        compiler_params=_cparams(("parallel", "arbitrary")), name=name)(a, b)


def _mm_resid_kernel(a_ref, b_ref, x_ref, gt_ref, o_ref):
    k = pl.program_id(2)
    part = jnp.dot(a_ref[...], b_ref[...], preferred_element_type=F32)

    @pl.when(k == 0)
    def _():
        o_ref[...] = part

    @pl.when(k > 0)
    def _():
        o_ref[...] += part

    @pl.when(k == pl.num_programs(2) - 1)
    def _():
        o_ref[...] = x_ref[...] + gt_ref[...] * o_ref[...]


def _matmul_resid(a, b, x, gt, name):
    m, kd = a.shape
    n = b.shape[1]
    tm, tn, tk = _pick(m, 1024), _pick(n, 1024), _pick(kd, 2048)
    return pl.pallas_call(
        _mm_resid_kernel, grid=(m // tm, n // tn, kd // tk),
        in_specs=[pl.BlockSpec((tm, tk), lambda i, j, k: (i, k)), pl.BlockSpec((tk, tn), lambda i, j, k: (k, j)),
                  pl.BlockSpec((tm, tn), lambda i, j, k: (i, j)), pl.BlockSpec((1, tn), lambda i, j, k: (0, j))],
        out_specs=pl.BlockSpec((tm, tn), lambda i, j, k: (i, j)),
        out_shape=jax.ShapeDtypeStruct((m, n), F32),
        compiler_params=_cparams(("parallel", "parallel", "arbitrary")), name=name)(a, b, x, gt)


def _ada_kernel(c_ref, w_ref, b_ref, o_ref):
    o_ref[...] = jnp.dot(c_ref[...], w_ref[...].astype(BF16), preferred_element_type=F32) + b_ref[...]


def _ada(c, w, b):
    d, n = w.shape
    tn = _pick(n, 512)
    c8 = jnp.broadcast_to(c.astype(BF16), (8, d))
    out = pl.pallas_call(
        _ada_kernel, grid=(n // tn,),
        in_specs=[pl.BlockSpec((8, d), lambda j: (0, 0)), pl.BlockSpec((d, tn), lambda j: (0, j)),
                  pl.BlockSpec((1, tn), lambda j: (0, j))],
        out_specs=pl.BlockSpec((8, tn), lambda j: (0, j)),
        out_shape=jax.ShapeDtypeStruct((8, n), F32),
        compiler_params=_cparams(("arbitrary",)), name="ada")(c8, w, b.reshape(1, n))
    return out[0:1]


def _normmod_kernel(x_ref, g_ref, sc_ref, sh_ref, o_ref):
    x = x_ref[...]
    y = x * lax.rsqrt(jnp.mean(x * x, axis=-1, keepdims=True) + EPS) * g_ref[...]
    o_ref[...] = (y * (1.0 + sc_ref[...]) + sh_ref[...]).astype(o_ref.dtype)


def _normmod(x, g, sc, sh):
    t, d = x.shape
    tm = _pick(t, 256)
    row = pl.BlockSpec((1, d), lambda i: (0, 0))
    return pl.pallas_call(
        _normmod_kernel, grid=(t // tm,),
        in_specs=[pl.BlockSpec((tm, d), lambda i: (i, 0)), row, row, row],
        out_specs=pl.BlockSpec((tm, d), lambda i: (i, 0)),
        out_shape=jax.ShapeDtypeStruct((t, d), BF16),
        compiler_params=_cparams(("parallel",)), name="normmod")(x, g.reshape(1, d), sc, sh)


def _prep_kernel(qa_ref, qi_ref, qb_ref, kvb_ref, ka_ref, va_ref, tail_ref,
                 gqa_ref, gka_ref, gqb_ref, gslc_ref, gwin_ref,
                 qa_o, qi_o, qb_o, kvb_o, ka_o, va_o, ki_o, wi_o, gs_o):
    qscale = HEAD_DIM ** -0.5 * LOG2E
    for h in range(DSA_HEADS):
        qa_o[h] = _head_norm(qa_ref[:, h * 128:(h + 1) * 128], gqa_ref[...], qscale).astype(BF16)
    for h in range(NSA_HEADS):
        qb_o[h] = _head_norm(qb_ref[:, h * 128:(h + 1) * 128], gqb_ref[...], qscale).astype(BF16)
    for h in range(IDX_HEADS):
        qi_o[h] = (qi_ref[:, h * IDX_DIM:(h + 1) * IDX_DIM] * IDX_DIM ** -0.5).astype(BF16)
    for g in range(DSA_KV_HEADS):
        ka_o[g] = _head_norm(ka_ref[:, g * 128:(g + 1) * 128], gka_ref[...], 1.0).astype(BF16)
        va_o[g] = va_ref[:, g * 128:(g + 1) * 128].astype(BF16)
    for s in range(6 * NSA_KV_HEADS):
        xs = kvb_ref[:, s * 128:(s + 1) * 128]
        part = s // NSA_KV_HEADS
        if part == 2:
            xs = _head_norm(xs, gslc_ref[...], 1.0)
        elif part == 4:
            xs = _head_norm(xs, gwin_ref[...], 1.0)
        kvb_o[s] = xs.astype(BF16)
    tail = tail_ref[...]
    ki_o[...] = tail[:, 0:IDX_DIM].astype(BF16)
    wi_o[...] = tail[:, 0:128] * IDX_HEADS ** -0.5
    gs_o[...] = jax.nn.sigmoid(tail[:, 128:256])


def _prep(pm, pt, gq_a, gk_a, gq_b, gk_slc, gk_win):
    t = pm.shape[0]
    tm = _pick(t, 128)
    g128 = pl.BlockSpec((1, 128), lambda i: (0, 0))

    def col(w, off):
        assert off % w == 0
        return pl.BlockSpec((tm, w), lambda i: (i, off // w))

    def hm(n, w):
        return pl.BlockSpec((n, tm, w), lambda i: (0, i, 0))

    row = lambda w: pl.BlockSpec((tm, w), lambda i: (i, 0))
    outs = pl.pallas_call(
        _prep_kernel, grid=(t // tm,),
        in_specs=[col(2048, OFF_QA), col(2048, OFF_QI), col(2048, OFF_QB), col(3072, OFF_KVB),
                  col(512, OFF_KA), col(512, OFF_VA), row(256), g128, g128, g128, g128, g128],
        out_specs=[hm(DSA_HEADS, 128), hm(IDX_HEADS, IDX_DIM), hm(NSA_HEADS, 128), hm(24, 128),
                   hm(DSA_KV_HEADS, 128), hm(DSA_KV_HEADS, 128), row(IDX_DIM), row(128), row(128)],
        out_shape=[jax.ShapeDtypeStruct((DSA_HEADS, t, 128), BF16),
                   jax.ShapeDtypeStruct((IDX_HEADS, t, IDX_DIM), BF16),
                   jax.ShapeDtypeStruct((NSA_HEADS, t, 128), BF16),
                   jax.ShapeDtypeStruct((24, t, 128), BF16),
                   jax.ShapeDtypeStruct((DSA_KV_HEADS, t, 128), BF16),
                   jax.ShapeDtypeStruct((DSA_KV_HEADS, t, 128), BF16),
                   jax.ShapeDtypeStruct((t, IDX_DIM), BF16),
                   jax.ShapeDtypeStruct((t, 128), F32),
                   jax.ShapeDtypeStruct((t, 128), F32)],
        compiler_params=_cparams(("parallel",)), name="prep")(
            pm, pm, pm, pm, pm, pm, pt,
            gq_a.reshape(1, 128), gk_a.reshape(1, 128), gq_b.reshape(1, 128),
            gk_slc.reshape(1, 128), gk_win.reshape(1, 128))
    return outs


def _bias_from_dist(dist, tbl_ref, h, nheads):
    val = jnp.full(dist.shape, tbl_ref[h], F32)
    for b in range(1, REL_BUCKETS):
        val = jnp.where(dist >= BUCKET_STARTS[b], tbl_ref[b * nheads + h], val)
    return (val - tbl_ref[(REL_BUCKETS - 1) * nheads + h]) * LOG2E


def _bias_tiles_kernel(tbl_ref, o_ref, *, tq, tk, nheads):
    h, d = pl.program_id(0), pl.program_id(1)
    dist = d * tq + lax.broadcasted_iota(jnp.int32, (tq, tk), 0) - lax.broadcasted_iota(jnp.int32, (tq, tk), 1)
    o_ref[0, 0] = _bias_from_dist(dist, tbl_ref, h, nheads)


def _n_near(tq, tk):
    return -(-(FAR_DIST + tk - 1) // tq)


def _bias_tiles(table, tq, tk):
    nheads = table.shape[1]
    nd = _n_near(tq, tk)
    return pl.pallas_call(
        functools.partial(_bias_tiles_kernel, tq=tq, tk=tk, nheads=nheads), grid=(nheads, nd),
        in_specs=[pl.BlockSpec(memory_space=pltpu.SMEM)],
        out_specs=pl.BlockSpec((1, 1, tq, tk), lambda h, d: (h, d, 0, 0)),
        out_shape=jax.ShapeDtypeStruct((nheads, nd, tq, tk), F32),
        compiler_params=_cparams(("parallel", "parallel")), name="bias_tiles")(table.reshape(-1))


def _idx_kernel(q_ref, kt_ref, w_ref, key_ref, thr_ref, s_ref, wb_ref, *, tq, tk, ts, t, kk):
    i = pl.program_id(0)
    nkt = (i * tq + tq + tk - 1) // tk
    w = w_ref[...]
    for h in range(IDX_HEADS):
        wb_ref[h] = jnp.broadcast_to(w[:, IDX_DIM + h:IDX_DIM + h + 1], (tq, LANE))
    qall = q_ref[...].reshape(IDX_HEADS * tq, IDX_DIM)
    rows = lax.broadcasted_iota(jnp.int32, (tq, ts), 0)
    cols = lax.broadcasted_iota(jnp.int32, (tq, ts), 1)

    def tile(j, carry):
        offs = [pl.multiple_of(j * tk + u * ts, ts) for u in range(tk // ts)]
        for u, off in enumerate(offs):
            s_ref[u] = jnp.dot(qall, kt_ref[:, pl.ds(off, ts)], preferred_element_type=F32)
        for u, off in enumerate(offs):
            acc = jnp.zeros((tq, ts), F32)
            for h in range(IDX_HEADS):
                wbh = jnp.concatenate([wb_ref[h]] * (ts // LANE), axis=1)
                acc = acc + jnp.maximum(s_ref[u, h * tq:(h + 1) * tq, :], 0.0) * wbh
            acc = jnp.where(off + cols <= i * tq + rows, acc, -jnp.inf)
            key_ref[:, pl.ds(off, ts)] = _to_key(acc)
        return carry

    lax.fori_loop(0, nkt, tile, 0)

    def fill(j, carry):
        key_ref[:, pl.ds(pl.multiple_of(j * tk, tk), tk)] = jnp.full((tq, tk), KEY_NEG_INF, jnp.int32)
        return carry

    lax.fori_loop(nkt, t // tk, fill, 0)

    def count_ge(cand):
        def body(j, c):
            kt = key_ref[:, pl.ds(pl.multiple_of(j * tk, tk), tk)]
            ge = jnp.where(kt >= cand, 1.0, 0.0)
            for a in range(tk // LANE):
                c = c + ge[:, a * LANE:(a + 1) * LANE]
            return c
        c = lax.fori_loop(0, nkt, body, jnp.zeros((tq, LANE), F32))
        return jnp.sum(c, axis=-1, keepdims=True)

    def score_range(j, c):
        smin, smax = c
        st = _from_key(key_ref[:, pl.ds(pl.multiple_of(j * tk, tk), tk)])
        lo_t = jnp.where(st == -jnp.inf, jnp.inf, st)
        for a in range(tk // LANE):
            smin = jnp.minimum(smin, lo_t[:, a * LANE:(a + 1) * LANE])
            smax = jnp.maximum(smax, st[:, a * LANE:(a + 1) * LANE])
        return smin, smax

    smin, smax = lax.fori_loop(0, nkt, score_range, (jnp.full((tq, LANE), jnp.inf, F32),
                                                     jnp.full((tq, LANE), -jnp.inf, F32)))
    kmin = _to_key(jnp.min(smin, axis=-1, keepdims=True))
    kmax = _to_key(jnp.max(smax, axis=-1, keepdims=True))
    thr_ref[...] = _kth_key_bracket(count_ge, kmin, kmax, float(kk))


def _dsa_index(qi_h, ki_t, wi, kk):
    t = ki_t.shape[1]
    tq, tk = _pick(t, 128), _pick(t, 512)
    ts = tk // 2
    return pl.pallas_call(
        functools.partial(_idx_kernel, tq=tq, tk=tk, ts=ts, t=t, kk=kk), grid=(t // tq,),
        in_specs=[pl.BlockSpec((IDX_HEADS, tq, IDX_DIM), lambda i: (0, i, 0)),
                  pl.BlockSpec((IDX_DIM, t), lambda i: (0, 0)),
                  pl.BlockSpec((tq, 128), lambda i: (i, 0))],
        out_specs=[pl.BlockSpec((tq, t), lambda i: (i, 0)), pl.BlockSpec((tq, 1), lambda i: (i, 0))],
        out_shape=[jax.ShapeDtypeStruct((t, t), jnp.int32), jax.ShapeDtypeStruct((t, 1), jnp.int32)],
        scratch_shapes=[pltpu.VMEM((tk // ts, IDX_HEADS * tq, ts), F32), pltpu.VMEM((IDX_HEADS, tq, LANE), F32)],
        compiler_params=_cparams(("parallel",)), name="dsa_index")(qi_h, ki_t, wi)


def _kv_tile(mode, i, s, tq, tk):
    if mode == "win":
        return jnp.maximum(i * tq - (WIN - 1), 0) // tk + s
    return s


def _last_tile(i, tq, tk):
    return (i * tq + tq - 1) // tk


def _flash_kernel(*refs, mode, tq, tk, nheads, ngroups, nnear):
    refs, (m_ref, l_ref, acc_ref, s_ref, p_ref, madd_ref) = refs[:-6], refs[-6:]
    if mode == "dsa":
        q_ref, k_ref, v_ref, b_ref, key_ref, thr_ref, o_ref = refs
    elif mode == "slc":
        q_ref, k_ref, v_ref, b_ref, sel_ref, e_ref, o_ref = refs
    else:
        q_ref, k_ref, v_ref, b_ref, o_ref = refs
    rep = nheads // ngroups
    nch = tk // LANE
    rblk = min(tq, 64)
    i, s = pl.program_id(0), pl.program_id(1)
    j = _kv_tile(mode, i, s, tq, tk)
    jlast = _last_tile(i, tq, tk)
    d = i - (tk // tq) * j

    @pl.when(s == 0)
    def _():
        m_ref[...] = jnp.full(m_ref.shape, NEG, F32)
        l_ref[...] = jnp.zeros(l_ref.shape, F32)
        acc_ref[...] = jnp.zeros(acc_ref.shape, F32)

    def step(near):
        if near:
            dist = ((i * tq - j * tk) + lax.broadcasted_iota(jnp.int32, (tq, tk), 0)
                    - lax.broadcasted_iota(jnp.int32, (tq, tk), 1))
            causal = dist >= 0
            if mode == "win":
                causal = causal & (dist < WIN)
        if mode == "dsa":
            hit = key_ref[...] >= thr_ref[...]
            madd_ref[...] = jnp.where((hit & causal) if near else hit, 0.0, NEG)
        elif mode == "win":
            madd_ref[...] = jnp.where(causal, 0.0, NEG)
        else:
            hit = jnp.dot(sel_ref[...].reshape(ngroups * tq, sel_ref.shape[2]), e_ref[...],
                          preferred_element_type=F32) > 0.5
            for g in range(ngroups):
                hg = hit[g * tq:(g + 1) * tq]
                madd_ref[g * tq:(g + 1) * tq, :] = jnp.where((hg & causal) if near else hg, 0.0, NEG)
        for g in range(ngroups):
            s_ref[...] = _dot_nt(q_ref[g * rep:(g + 1) * rep].reshape(rep * tq, HEAD_DIM), k_ref[g])
            mrow = g * tq if mode == "slc" else 0
            for r in range(rep):
                h = g * rep + r
                for rb in range(tq // rblk):
                    rows = slice(rb * rblk, (rb + 1) * rblk)
                    srows = slice(r * tq + rb * rblk, r * tq + (rb + 1) * rblk)
                    sh = s_ref[srows, :] + madd_ref[mrow + rb * rblk:mrow + (rb + 1) * rblk, :]
                    if near:
                        sh = sh + b_ref[h, 0, rows, :]
                    chunks = [sh[:, c * LANE:(c + 1) * LANE] for c in range(nch)]
                    m_old = m_ref[h, rows, :]
                    tile_max = jnp.max(functools.reduce(jnp.maximum, chunks), axis=-1, keepdims=True)
                    m_new = jnp.maximum(m_old, tile_max)
                    alpha = jnp.exp2(m_old - m_new)
                    pcs = [jnp.exp2(c - m_new) for c in chunks]
                    l_ref[h, rows, :] = alpha * l_ref[h, rows, :] + functools.reduce(jnp.add, pcs)
                    m_ref[h, rows, :] = m_new
                    acc_ref[h, rows, :] = acc_ref[h, rows, :] * alpha
                    p_ref[srows, :] = jnp.concatenate(pcs, axis=1).astype(BF16)
            pv = jnp.dot(p_ref[...], v_ref[g], preferred_element_type=F32)
            for r in range(rep):
                h = g * rep + r
                acc_ref[h] = acc_ref[h] + pv[r * tq:(r + 1) * tq]

    valid = j <= jlast
    if mode == "win":
        pl.when(valid)(lambda: step(True))
    else:
        pl.when(jnp.logical_and(valid, d < nnear))(lambda: step(True))
        pl.when(jnp.logical_and(valid, d >= nnear))(lambda: step(False))

    @pl.when(j == jlast)
    def _():
        for h in range(nheads):
            l = jnp.sum(l_ref[h], axis=-1, keepdims=True)
            o_ref[:, h * HEAD_DIM:(h + 1) * HEAD_DIM] = (acc_ref[h] / l).astype(o_ref.dtype)


def _flash(mode, q_h, k_h, v_h, bias, extra, out_dtype):
    nheads, t, _ = q_h.shape
    ngroups = k_h.shape[0]
    rep = nheads // ngroups
    _, nnear, tq, tk = bias.shape
    nq = t // tq
    if mode == "win":
        nsteps = max((i * tq + tq - 1) // tk - max(i * tq - (WIN - 1), 0) // tk + 1 for i in range(nq))
    else:
        nsteps = t // tk

    def jmap(i, s):
        return jnp.minimum(_kv_tile(mode, i, s, tq, tk), _last_tile(i, tq, tk))

    def bmap(i, s):
        return jnp.minimum(i - (tk // tq) * jmap(i, s), nnear - 1)

    in_specs = [pl.BlockSpec((nheads, tq, HEAD_DIM), lambda i, s: (0, i, 0)),
                pl.BlockSpec((ngroups, tk, HEAD_DIM), lambda i, s: (0, jmap(i, s), 0)),
                pl.BlockSpec((ngroups, tk, HEAD_DIM), lambda i, s: (0, jmap(i, s), 0)),
                pl.BlockSpec((nheads, 1, tq, tk), lambda i, s: (0, bmap(i, s), 0, 0))]
    if mode == "dsa":
        in_specs += [pl.BlockSpec((tq, tk), lambda i, s: (i, jmap(i, s))),
                     pl.BlockSpec((tq, 1), lambda i, s: (i, 0))]
    elif mode == "slc":
        ns = extra[0].shape[2]
        in_specs += [pl.BlockSpec((ngroups, tq, ns), lambda i, s: (0, i, 0)),
                     pl.BlockSpec((ns, tk), lambda i, s: (0, jmap(i, s)))]
    return pl.pallas_call(
        functools.partial(_flash_kernel, mode=mode, tq=tq, tk=tk, nheads=nheads, ngroups=ngroups, nnear=nnear),
        grid=(nq, nsteps), in_specs=in_specs,
        out_specs=pl.BlockSpec((tq, nheads * HEAD_DIM), lambda i, s: (i, 0)),
        out_shape=jax.ShapeDtypeStruct((t, nheads * HEAD_DIM), out_dtype),
        scratch_shapes=[pltpu.VMEM((nheads, tq, LANE), F32), pltpu.VMEM((nheads, tq, LANE), F32),
                        pltpu.VMEM((nheads, tq, HEAD_DIM), F32),
                        pltpu.VMEM((rep * tq, tk), F32), pltpu.VMEM((rep * tq, tk), BF16),
                        pltpu.VMEM(((ngroups if mode == "slc" else 1) * tq, tk), F32)],
        compiler_params=_cparams(("parallel", "arbitrary")), name="flash_" + mode)(q_h, k_h, v_h, bias, *extra)


def _compress_kernel(c_ref, w1_ref, pe_ref, w2_ref, g_ref, o_ref, *, norm):
    c = c_ref[0]
    half = CMP_STRIDE * HEAD_DIM
    ncp = c.shape[0]
    a = jnp.dot(c, w1_ref[0:half], preferred_element_type=F32)
    b = jnp.dot(c, w1_ref[half:2 * half], preferred_element_type=F32)
    pet = jnp.dot(pe_ref[...], w1_ref[...], preferred_element_type=F32)[0:1]
    hid = _gelu(a + pltpu.roll(b, ncp - 1, 0) + pet)
    y = jnp.dot(hid.astype(BF16), w2_ref[...], preferred_element_type=F32)
    if norm:
        y = _head_norm(y, g_ref[...], 1.0)
    o_ref[0] = y.astype(BF16)


def _compress(slabs, base, pe, w1, w2, gain, norm):
    _, t, _ = slabs.shape
    ncp = t // CMP_STRIDE
    width = CMP_STRIDE * HEAD_DIM
    chunks = slabs.reshape(slabs.shape[0], ncp, width)
    w1f = w1.reshape(CMP_LEN * HEAD_DIM, CMP_HIDDEN).astype(BF16)
    pe8 = jnp.broadcast_to(pe.reshape(1, CMP_LEN * HEAD_DIM).astype(BF16), (8, CMP_LEN * HEAD_DIM))
    full = lambda shp: pl.BlockSpec(shp, lambda g: (0,) * len(shp))
    return pl.pallas_call(
        functools.partial(_compress_kernel, norm=norm), grid=(NSA_KV_HEADS,),
        in_specs=[pl.BlockSpec((1, ncp, width), lambda g: (base + g, 0, 0)),
                  full((CMP_LEN * HEAD_DIM, CMP_HIDDEN)), full((8, CMP_LEN * HEAD_DIM)),
                  full((CMP_HIDDEN, HEAD_DIM)), full((1, HEAD_DIM))],
        out_specs=pl.BlockSpec((1, ncp, HEAD_DIM), lambda g: (g, 0, 0)),
        out_shape=jax.ShapeDtypeStruct((NSA_KV_HEADS, ncp, HEAD_DIM), BF16),
        compiler_params=_cparams(("parallel",)), name="compress")(
            chunks, w1f, pe8, w2.astype(BF16), gain.reshape(1, HEAD_DIM))


def _cmp_attn_kernel(tbl_ref, q_ref, kc_ref, vc_ref, ov_ref, oc_ref, sel_ref, s_ref, key_ref,
                     *, tq, ncp, ns, nsel, ww):
    i = pl.program_id(0)
    rep = NSA_HEADS // NSA_KV_HEADS
    qpos = i * tq + lax.broadcasted_iota(jnp.int32, (tq, ncp), 0)
    dist = qpos - (lax.broadcasted_iota(jnp.int32, (tq, ncp), 1) * CMP_STRIDE + CMP_LEN - 1)
    mask = dist >= 0
    w0 = jnp.maximum(i * tq - (CMP_LEN - 1) - FAR_DIST + CMP_STRIDE, 0) // (CMP_STRIDE * LANE) * LANE
    w0 = pl.multiple_of(jnp.minimum(w0, ncp - ww), LANE)
    dist_w = (i * tq + lax.broadcasted_iota(jnp.int32, (tq, ww), 0)
              - ((w0 + lax.broadcasted_iota(jnp.int32, (tq, ww), 1)) * CMP_STRIDE + CMP_LEN - 1))
    bucket = jnp.zeros((tq, ww), jnp.int32)
    for b in range(1, REL_BUCKETS):
        bucket = jnp.where(dist_w >= BUCKET_STARTS[b], b, bucket)
    blk = lax.broadcasted_iota(jnp.int32, (tq, ns), 1)
    qp = i * tq + lax.broadcasted_iota(jnp.int32, (tq, ns), 0)
    cur = qp // SLC_LEN
    forced = (blk == 0) | (blk == cur) | (blk == cur - 1)
    admissible = blk * SLC_LEN <= qp
    for g in range(NSA_KV_HEADS):
        s_ref[...] = _dot_nt(q_ref[g * rep:(g + 1) * rep].reshape(rep * tq, HEAD_DIM), kc_ref[g])
        imp = jnp.zeros((tq, ncp), F32)
        ps = []
        for r in range(rep):
            h = g * rep + r
            trow = (tbl_ref[h:h + 1, :] - tbl_ref[h:h + 1, REL_BUCKETS - 1:REL_BUCKETS]) * LOG2E
            trow = jnp.broadcast_to(trow, (tq, LANE))
            bias = [jnp.take_along_axis(trow, bucket[:, c * LANE:(c + 1) * LANE], axis=1)
                    for c in range(ww // LANE)]
            s_ref[r * tq:(r + 1) * tq, pl.ds(w0, ww)] += jnp.concatenate(bias, axis=1)
            sh = jnp.where(mask, s_ref[r * tq:(r + 1) * tq, :], NEG)
            m = jnp.max(sh, axis=-1, keepdims=True)
            p = jnp.where(mask, jnp.exp2(sh - m), 0.0)
            pc = p / jnp.maximum(jnp.sum(p, axis=-1, keepdims=True), 1e-30)
            imp = imp + pc
            ps.append(pc.astype(BF16))
        o = jnp.dot(jnp.concatenate(ps, axis=0), vc_ref[g], preferred_element_type=F32)
        for r in range(rep):
            h = g * rep + r
            oc_ref[:, h * HEAD_DIM:(h + 1) * HEAD_DIM] = o[r * tq:(r + 1) * tq]
        hi = imp.astype(BF16)
        lo = (imp - hi.astype(F32)).astype(BF16)
        impb = (jnp.dot(hi, ov_ref[...], preferred_element_type=F32)
                + jnp.dot(lo, ov_ref[...], preferred_element_type=F32))
        impb = jnp.where(forced, FORCE_SCORE, impb)
        key_ref[g] = _to_key(jnp.where(admissible, impb, -jnp.inf))
    keys = key_ref[...].reshape(NSA_KV_HEADS * tq, ns)
    thr = _kth_key_val(keys, nsel)
    sel_ref[...] = jnp.where(keys >= thr, 1.0, 0.0).reshape(NSA_KV_HEADS, tq, ns).astype(BF16)


def _cmp_attn(table, q_h, kc, vc):
    nheads, t, _ = q_h.shape
    ncp = kc.shape[1]
    ns = t // SLC_LEN
    nsel = min(SLC_TOP, ns)
    tq = _pick(t, 128)
    ww = min(ncp, LANE * (-(-(FAR_DIST + tq + CMP_STRIDE * LANE) // (CMP_STRIDE * LANE))))
    rep = nheads // NSA_KV_HEADS
    c_start = np.arange(ncp)[:, None] * CMP_STRIDE
    s_start = np.arange(ns)[None, :] * SLC_LEN
    overlap = jnp.asarray((c_start < s_start + SLC_LEN) & (c_start + CMP_LEN > s_start), BF16)
    full = lambda shp: pl.BlockSpec(shp, lambda i: (0,) * len(shp))
    table_t = jnp.pad(table.T, ((0, 0), (0, LANE - REL_BUCKETS)))
    return pl.pallas_call(
        functools.partial(_cmp_attn_kernel, tq=tq, ncp=ncp, ns=ns, nsel=nsel, ww=ww), grid=(t // tq,),
        in_specs=[full((nheads, LANE)),
                  pl.BlockSpec((nheads, tq, HEAD_DIM), lambda i: (0, i, 0)),
                  full((NSA_KV_HEADS, ncp, HEAD_DIM)), full((NSA_KV_HEADS, ncp, HEAD_DIM)), full((ncp, ns))],
        out_specs=[pl.BlockSpec((tq, nheads * HEAD_DIM), lambda i: (i, 0)),
                   pl.BlockSpec((NSA_KV_HEADS, tq, ns), lambda i: (0, i, 0))],
        out_shape=[jax.ShapeDtypeStruct((t, nheads * HEAD_DIM), F32),
                   jax.ShapeDtypeStruct((NSA_KV_HEADS, t, ns), BF16)],
        scratch_shapes=[pltpu.VMEM((rep * tq, ncp), F32), pltpu.VMEM((NSA_KV_HEADS, tq, ns), jnp.int32)],
        compiler_params=_cparams(("parallel",)), name="cmp_attn")(table_t, q_h, kc, vc, overlap)


def _combine_kernel(oc_ref, os_ref, ow_ref, g_ref, y_ref):
    g = g_ref[...]
    for h in range(NSA_HEADS):
        sl = slice(h * HEAD_DIM, (h + 1) * HEAD_DIM)
        y = (g[:, 3 * h:3 * h + 1] * oc_ref[:, sl] + g[:, 3 * h + 1:3 * h + 2] * os_ref[:, sl]
             + g[:, 3 * h + 2:3 * h + 3] * ow_ref[:, sl])
        y_ref[:, sl] = y.astype(y_ref.dtype)


def _combine(oc, os_, ow, gsig):
    t, w = oc.shape
    tm = _pick(t, 256)
    blk = pl.BlockSpec((tm, w), lambda i: (i, 0))
    return pl.pallas_call(
        _combine_kernel, grid=(t // tm,),
        in_specs=[blk, blk, blk, pl.BlockSpec((tm, 128), lambda i: (i, 0))],
        out_specs=blk, out_shape=jax.ShapeDtypeStruct((t, w), BF16),
        compiler_params=_cparams(("parallel",)), name="combine")(oc, os_, ow, gsig)


def _merge_kernel(ya_ref, yb_ref, wa_ref, wb_ref, ga_ref, gb_ref, o_ref):
    za = jnp.dot(ya_ref[...], wa_ref[...], preferred_element_type=F32)
    zb = jnp.dot(yb_ref[...], wb_ref[...], preferred_element_type=F32)
    o_ref[...] = (jax.nn.sigmoid(ga_ref[...]) * za + jax.nn.sigmoid(gb_ref[...]) * zb).astype(o_ref.dtype)


def _merge(y_a, y_b, w_a, w_b, pm):
    t, k = y_a.shape
    d = w_a.shape[1]
    tm, tn = _pick(t, 1024), _pick(d, 512)
    assert OFF_GM % tn == 0
    lhs = pl.BlockSpec((tm, k), lambda i, j: (i, 0))
    rhs = pl.BlockSpec((k, tn), lambda i, j: (0, j))
    return pl.pallas_call(
        _merge_kernel, grid=(t // tm, d // tn),
        in_specs=[lhs, lhs, rhs, rhs, pl.BlockSpec((tm, tn), lambda i, j: (i, OFF_GM // tn + j)),
                  pl.BlockSpec((tm, tn), lambda i, j: (i, (OFF_GM + d) // tn + j))],
        out_specs=pl.BlockSpec((tm, tn), lambda i, j: (i, j)), out_shape=jax.ShapeDtypeStruct((t, d), BF16),
        compiler_params=_cparams(("parallel", "arbitrary")), name="merge")(y_a, y_b, w_a, w_b, pm, pm)


_CAND_FULL = PEER_TOPK // 2


def _peer_cands(a1, a2, op):
    pieces = [op(a1[0:1], a2)]
    pieces += [op(a1[k:k + 1], a2[0:_CAND_FULL]) for k in range(1, _CAND_FULL)]
    pieces.append(op(a1[_CAND_FULL:], a2[0:1]))
    return jnp.concatenate(pieces, axis=0)


def _peer_score_kernel(q_ref, sk_ref, s1_o, e1_o, s2_o, e2_o, thr_o, *, tm):
    row = lax.broadcasted_iota(jnp.int32, (PEER_NKEYS, tm), 0).astype(F32)
    for h in range(PEER_HEADS):
        svals, tops = [], []
        for c in range(2):
            hc = 2 * h + c
            s = _dot_nt(sk_ref[hc], q_ref[:, hc * 128:(hc + 1) * 128].astype(BF16))
            svals.append(s)
            work, top = s, []
            for _ in range(PEER_TOPK):
                mx = jnp.max(work, axis=0, keepdims=True)
                first = jnp.min(jnp.where(work == mx, row, float(PEER_NKEYS)), axis=0, keepdims=True)
                work = jnp.where(row == first, -jnp.inf, work)
                top.append(mx)
            tops.append(jnp.concatenate(top, axis=0))
        a1, a2 = tops
        cand = _peer_cands(a1, a2, jnp.add)
        thr = _from_key(_kth_key_val(_to_key(cand), PEER_TOPK, axis=0))
        m1, m2 = a1[0:1], a2[0:1]
        ec = _peer_cands(jnp.exp(a1 - m1), jnp.exp(a2 - m2), jnp.multiply)
        z = jnp.sum(jnp.where(cand >= thr, ec, 0.0), axis=0, keepdims=True)
        s1_o[h] = svals[0]
        s2_o[h] = svals[1]
        e1_o[h] = jnp.exp(svals[0] - m1)
        e2_o[h] = jnp.exp(svals[1] - m2) / z
        thr_o[h:h + 1, :] = thr


def _peer_score(qp, sub_keys):
    t = qp.shape[0]
    tm = _pick(t, 256)
    sk = sub_keys.reshape(2 * PEER_HEADS, PEER_NKEYS, PEER_QDIM // 2).astype(BF16)
    tr = pl.BlockSpec((PEER_HEADS, PEER_NKEYS, tm), lambda i: (0, 0, i))
    shp = jax.ShapeDtypeStruct((PEER_HEADS, PEER_NKEYS, t), F32)
    return pl.pallas_call(
        functools.partial(_peer_score_kernel, tm=tm), grid=(t // tm,),
        in_specs=[pl.BlockSpec((tm, 2 * PEER_HEADS * 128), lambda i: (i, 0)),
                  pl.BlockSpec((2 * PEER_HEADS, PEER_NKEYS, PEER_QDIM // 2), lambda i: (0, 0, 0))],
        out_specs=[tr, tr, tr, tr, pl.BlockSpec((PEER_HEADS, tm), lambda i: (0, i))],
        out_shape=[shp, shp, shp, shp, jax.ShapeDtypeStruct((PEER_HEADS, t), F32)],
        compiler_params=_cparams(("parallel",)), name="peer_score")(qp, sk)


def _peer_act_kernel(h_ref, u_ref, s1_ref, e1_ref, s2_ref, e2_ref, thr_ref, o_ref, w_ref, a_ref, *, tm, te, sub):
    j = pl.program_id(1)
    n1 = te // PEER_NKEYS
    a_ref[...] = _dot_nt(h_ref[...], u_ref[...])
    for ai in range(n1):
        i1 = j * n1 + ai
        s1rows = [s1_ref[h, pl.ds(i1, 1), :] for h in range(PEER_HEADS)]
        e1rows = [e1_ref[h, pl.ds(i1, 1), :] for h in range(PEER_HEADS)]
        for ts in range(tm // LANE):
            tok = slice(ts * LANE, (ts + 1) * LANE)
            wt = jnp.zeros((PEER_NKEYS, LANE), F32)
            for h in range(PEER_HEADS):
                hit = (s1rows[h][:, tok] + s2_ref[h, :, tok]) >= thr_ref[h:h + 1, tok]
                wt = wt + jnp.where(hit, e1rows[h][:, tok] * e2_ref[h, :, tok], 0.0)
            w_ref[tok, ai * PEER_NKEYS:(ai + 1) * PEER_NKEYS] = wt.T
    for r in range(tm // sub):
        rows = slice(r * sub, (r + 1) * sub)
        o_ref[rows, :] = (_gelu(a_ref[rows, :]) * w_ref[rows, :]).astype(o_ref.dtype)


def _peer_act(h2, u, s1, e1, s2, e2, thr):
    t, d = h2.shape
    ne = u.shape[0]
    tm, te, sub = _pick(t, 1024), 512, 256
    once = dict(pipeline_mode=pl.Buffered(1))
    tok = pl.BlockSpec((PEER_HEADS, PEER_NKEYS, tm), lambda i, j: (0, 0, i), **once)
    return pl.pallas_call(
        functools.partial(_peer_act_kernel, tm=tm, te=te, sub=sub), grid=(t // tm, ne // te),
        in_specs=[pl.BlockSpec((tm, d), lambda i, j: (i, 0), **once),
                  pl.BlockSpec((te, d), lambda i, j: (j, 0)),
                  tok, tok, tok, tok, pl.BlockSpec((PEER_HEADS, tm), lambda i, j: (0, i), **once)],
        out_specs=pl.BlockSpec((tm, te), lambda i, j: (i, j)),
        out_shape=jax.ShapeDtypeStruct((t, ne), BF16),
        scratch_shapes=[pltpu.VMEM((tm, te), F32), pltpu.VMEM((tm, te), F32)],
        compiler_params=_cparams(("parallel", "arbitrary")), name="peer_act")(h2, u, s1, e1, s2, e2, thr)


def _pack_w_in(w_in, d):
    offs = np.cumsum([0, DSA_HEADS * 128, DSA_KV_HEADS * 128, DSA_KV_HEADS * 128, IDX_HEADS * IDX_DIM, IDX_DIM,
                      IDX_HEADS, NSA_HEADS * 128, 6 * NSA_KV_HEADS * 128, 3 * NSA_HEADS, 2 * d])
    qa, ka, va, qi, ki, wi, qb, kvb, gb, gm = [w_in[:, int(offs[n]):int(offs[n + 1])] for n in range(10)]
    main = jnp.concatenate([qa, qi, qb, kvb, ka, va, gm], axis=1).astype(BF16)
    pad = lambda n: jnp.zeros((d, n), w_in.dtype)
    tail = jnp.concatenate([ki, wi, pad(128 - IDX_DIM - IDX_HEADS), gb, pad(128 - 3 * NSA_HEADS)], axis=1).astype(BF16)
    return main, tail


def _token_mixer(h, table, w_in, gq_a, gk_a, gq_b, gk_cmp, gk_slc, gk_win, cmp_pe_k, cmp_w1_k, cmp_w2_k,
                 cmp_pe_v, cmp_w1_v, cmp_w2_v, w_branch_a, w_branch_b):
    t, d = h.shape
    w_main, w_tail = _pack_w_in(w_in, d)
    pm = _matmul(h, w_main, name="proj_main")
    pt = _matmul(h, w_tail, name="proj_tail")
    qa_h, qi_h, qb_h, kvb_h, ka_h, va_h, ki, wi, gsig = _prep(pm, pt, gq_a, gk_a, gq_b, gk_slc, gk_win)
    tq, tk = _pick(t, 256), _pick(t, 512)
    keys, thr = _dsa_index(qi_h, ki.T, wi, min(DSA_TOPK, t // 4))
    bias_a = _bias_tiles(table[:, :DSA_HEADS], tq, tk)
    y_a = _flash("dsa", qa_h, ka_h, va_h, bias_a, (keys, thr), BF16)
    g = NSA_KV_HEADS
    kc = _compress(kvb_h, 0, cmp_pe_k, cmp_w1_k, cmp_w2_k, gk_cmp, True)
    vc = _compress(kvb_h, g, cmp_pe_v, cmp_w1_v, cmp_w2_v, gk_cmp, False)
    o_c, sel = _cmp_attn(table[:, DSA_HEADS:], qb_h, kc, vc)
    bias_b = _bias_tiles(table[:, DSA_HEADS:], tq, tk)
    ns = t // SLC_LEN
    expand = jnp.asarray(np.arange(ns)[:, None] == (np.arange(t)[None, :] // SLC_LEN), BF16)
    o_s = _flash("slc", qb_h, kvb_h[2 * g:3 * g], kvb_h[3 * g:4 * g], bias_b, (sel, expand), F32)
    o_w = _flash("win", qb_h, kvb_h[4 * g:5 * g], kvb_h[5 * g:6 * g], bias_b, (), F32)
    y_b = _combine(o_c, o_s, o_w, gsig)
    return _merge(y_a, y_b, w_branch_a.astype(BF16), w_branch_b.astype(BF16), pm)


def _peer_ffn(h2, x1, gt2, w_q, sub_keys, u, v):
    qp = _matmul(h2, w_q.astype(BF16), name="peer_q")
    s1, e1, s2, e2, thr = _peer_score(qp, sub_keys)
    act = _peer_act(h2, u.astype(BF16), s1, e1, s2, e2, thr)
    return _matmul_resid(act, v.astype(BF16), x1, gt2, name="peer_out")


def kernel(x, c, rel_bias, w_ada, b_ada, g_mix, w_in, gq_a, gk_a, gq_b, gk_cmp, gk_slc, gk_win, cmp_pe_k, cmp_w1_k,
           cmp_w2_k, cmp_pe_v, cmp_w1_v, cmp_w2_v, w_branch_a, w_branch_b, w_out, g_ffn, w_peer_q, peer_sub_keys,
           peer_u, peer_v):
    bsz, t, d = x.shape
    assert bsz == 1 and t % 512 == 0 and d % 128 == 0
    xs = x[0]
    for i in range(w_ada.shape[0]):
        mod = _ada(c, w_ada[i], b_ada[i])
        sh1, sc1, gt1, sh2, sc2, gt2 = [mod[:, n * d:(n + 1) * d] for n in range(6)]
        h = _normmod(xs, g_mix[i], sc1, sh1)
        merged = _token_mixer(h, rel_bias, w_in[i], gq_a[i], gk_a[i], gq_b[i], gk_cmp[i], gk_slc[i], gk_win[i],
                              cmp_pe_k[i], cmp_w1_k[i], cmp_w2_k[i], cmp_pe_v[i], cmp_w1_v[i], cmp_w2_v[i],
                              w_branch_a[i], w_branch_b[i])
        x1 = _matmul_resid(merged, w_out[i].astype(BF16), xs, gt1, name="w_out")
        h2 = _normmod(x1, g_ffn[i], sc2, sh2)
        xs = _peer_ffn(h2, x1, gt2, w_peer_q[i], peer_sub_keys[i], peer_u[i], peer_v[i])
    return xs[None]
```

```python
import functools
import math

import numpy as np
import jax
import jax.numpy as jnp
from jax import lax
from jax.experimental import pallas as pl
from jax.experimental.pallas import tpu as pltpu

HEAD_DIM = 128
DSA_HEADS = 16
DSA_KV_HEADS = 4
IDX_HEADS = 32
IDX_DIM = 64
DSA_TOPK = 256
NSA_HEADS = 16
NSA_KV_HEADS = 4
CMP_LEN = 32
CMP_STRIDE = 16
CMP_HIDDEN = 256
SLC_LEN = 64
SLC_TOP = 16
WIN = 512
FORCE_SCORE = 1e9
PEER_HEADS = 8
PEER_NKEYS = 128
PEER_QDIM = 256
PEER_TOPK = 16
REL_BUCKETS = 32
REL_MAX_DIST = 2048
EPS = 1e-6

LANE = 128
VMEM_LIMIT = 56 * 1024 * 1024
NEG = -1e30
LOG2E = math.log2(math.e)
INT_MIN = -2 ** 31
KEY_NEG_INF = -2 ** 31 + 0x007FFFFF

OFF_QA, OFF_QI, OFF_QB, OFF_KVB, OFF_KA, OFF_VA, OFF_GM = 0, 2048, 4096, 6144, 9216, 9728, 10240

F32 = jnp.float32
BF16 = jnp.bfloat16


def _bucket_starts():
    n = np.arange(2 * REL_MAX_DIST)
    exact = REL_BUCKETS // 2
    nf = np.maximum(n, 1).astype(np.float32)
    lb = exact + (np.log(nf / np.float32(exact)) / np.float32(math.log(REL_MAX_DIST / exact))
                  * np.float32(REL_BUCKETS - exact)).astype(np.int32)
    bucket = np.where(n < exact, n, np.minimum(lb, REL_BUCKETS - 1))
    return [int(np.argmax(bucket >= b)) for b in range(REL_BUCKETS)]


BUCKET_STARTS = _bucket_starts()
FAR_DIST = BUCKET_STARTS[REL_BUCKETS - 1]


def _cparams(sem):
    return pltpu.CompilerParams(dimension_semantics=sem, vmem_limit_bytes=VMEM_LIMIT)


def _pick(n, pref):
    t = pref
    while n % t:
        t //= 2
    return t


def _to_key(x):
    b = lax.bitcast_convert_type(x, jnp.int32)
    return b ^ ((b >> 31) & 0x7FFFFFFF)


def _from_key(k):
    return lax.bitcast_convert_type(k ^ ((k >> 31) & 0x7FFFFFFF), F32)


def _kth_key(count_ge, shape, kk):
    def bit_pass(b, x):
        cand = x + lax.shift_left(jnp.int32(1), 31 - b)
        return jnp.where(count_ge(cand) >= kk, cand, x)
    x = lax.fori_loop(0, 32, bit_pass, jnp.full(shape, INT_MIN, jnp.int32))
    return jnp.maximum(x, KEY_NEG_INF)


def _kth_key_bracket(count_ge, kmin, kmax, kk):
    c_min = count_ge(kmin)
    few = c_min < kk
    lo0 = jnp.where(few, KEY_NEG_INF, kmin)
    hi0 = kmax + 1
    done0 = jnp.where(few | (c_min == kk) | (hi0 - 1 <= lo0), 1.0, 0.0)

    def cond(st):
        p, _, _, _, _, done = st
        return jnp.logical_and(p < 72, jnp.min(done) < 0.5)

    def body(st):
        p, lo, hi, clo, chi, done = st
        mid = (lo >> 1) + (hi >> 1) + (lo & hi & 1)
        flo, fhi = _from_key(lo), _from_key(hi)
        frac = (jnp.log(clo) - math.log(kk)) / (jnp.log(clo) - jnp.log(jnp.maximum(chi, 0.5)))
        cand = jnp.where(p % 2 == 1, mid, _to_key(flo + frac * (fhi - flo)))
        cand = jnp.minimum(jnp.maximum(cand, lo + 1), hi - 1)
        active = done < 0.5
        cand = jnp.where(active, cand, lo)
        cnt = count_ge(cand)
        up = active & (cnt >= kk)
        dn = active & (cnt < kk)
        lo, clo = jnp.where(up, cand, lo), jnp.where(up, cnt, clo)
        hi, chi = jnp.where(dn, cand, hi), jnp.where(dn, cnt, chi)
        done = jnp.where((clo == kk) | (hi - 1 <= lo), 1.0, done)
        return p + 1, lo, hi, clo, chi, done

    st = lax.while_loop(cond, body, (jnp.int32(0), lo0, hi0, c_min, jnp.zeros_like(c_min), done0))
    return st[1]


def _kth_key_val(keys, kk, axis=-1):
    def count_ge(cand):
        return jnp.sum(jnp.where(keys >= cand, 1.0, 0.0), axis=axis, keepdims=True)
    shape = list(keys.shape)
    shape[axis] = 1
    return _kth_key(count_ge, tuple(shape), float(kk))


def _gelu(x):
    return 0.5 * x * (1.0 + jnp.tanh(math.sqrt(2.0 / math.pi) * (x + 0.044715 * (x * x * x))))


def _dot_nt(a, b):
    return lax.dot_general(a, b, (((1,), (1,)), ((), ())), preferred_element_type=F32)


def _head_norm(x, g, scale):
    ms = jnp.mean(x * x, axis=-1, keepdims=True)
    return x * lax.rsqrt(ms + EPS) * g * scale


def _mm_kernel(a_ref, b_ref, o_ref):
    o_ref[...] = jnp.dot(a_ref[...], b_ref[...], preferred_element_type=F32).astype(o_ref.dtype)


def _matmul(a, b, out_dtype=F32, tm=1024, tn=512, name="matmul"):
    m, k = a.shape
    n = b.shape[1]
    tm, tn = _pick(m, tm), _pick(n, tn)
    return pl.pallas_call(
        _mm_kernel, grid=(m // tm, n // tn),
        in_specs=[pl.BlockSpec((tm, k), lambda i, j: (i, 0)), pl.BlockSpec((k, tn), lambda i, j: (0, j))],
        out_specs=pl.BlockSpec((tm, tn), lambda i, j: (i, j)),
        out_shape=jax.ShapeDtypeStruct((m, n), out_dtype),
        compiler_params=_cparams(("parallel", "arbitrary")), name=name)(a, b)


def _mm_resid_kernel(a_ref, b_ref, x_ref, gt_ref, o_ref):
    k = pl.program_id(2)
    part = jnp.dot(a_ref[...], b_ref[...], preferred_element_type=F32)

    @pl.when(k == 0)
    def _():
        o_ref[...] = part

    @pl.when(k > 0)
    def _():
        o_ref[...] += part

    @pl.when(k == pl.num_programs(2) - 1)
    def _():
        o_ref[...] = x_ref[...] + gt_ref[...] * o_ref[...]


def _matmul_resid(a, b, x, gt, name):
    m, kd = a.shape
    n = b.shape[1]
    tm, tn, tk = _pick(m, 1024), _pick(n, 1024), _pick(kd, 2048)
    return pl.pallas_call(
        _mm_resid_kernel, grid=(m // tm, n // tn, kd // tk),
        in_specs=[pl.BlockSpec((tm, tk), lambda i, j, k: (i, k)), pl.BlockSpec((tk, tn), lambda i, j, k: (k, j)),
                  pl.BlockSpec((tm, tn), lambda i, j, k: (i, j)), pl.BlockSpec((1, tn), lambda i, j, k: (0, j))],
        out_specs=pl.BlockSpec((tm, tn), lambda i, j, k: (i, j)),
        out_shape=jax.ShapeDtypeStruct((m, n), F32),
        compiler_params=_cparams(("parallel", "parallel", "arbitrary")), name=name)(a, b, x, gt)


def _ada_kernel(c_ref, w_ref, b_ref, o_ref):
    o_ref[...] = jnp.dot(c_ref[...], w_ref[...].astype(BF16), preferred_element_type=F32) + b_ref[...]


def _ada(c, w, b):
    d, n = w.shape
    tn = _pick(n, 512)
    c8 = jnp.broadcast_to(c.astype(BF16), (8, d))
    out = pl.pallas_call(
        _ada_kernel, grid=(n // tn,),
        in_specs=[pl.BlockSpec((8, d), lambda j: (0, 0)), pl.BlockSpec((d, tn), lambda j: (0, j)),
                  pl.BlockSpec((1, tn), lambda j: (0, j))],
        out_specs=pl.BlockSpec((8, tn), lambda j: (0, j)),
        out_shape=jax.ShapeDtypeStruct((8, n), F32),
        compiler_params=_cparams(("arbitrary",)), name="ada")(c8, w, b.reshape(1, n))
    return out[0:1]


def _normmod_kernel(x_ref, g_ref, sc_ref, sh_ref, o_ref):
    x = x_ref[...]
    y = x * lax.rsqrt(jnp.mean(x * x, axis=-1, keepdims=True) + EPS) * g_ref[...]
    o_ref[...] = (y * (1.0 + sc_ref[...]) + sh_ref[...]).astype(o_ref.dtype)


def _normmod(x, g, sc, sh):
    t, d = x.shape
    tm = _pick(t, 256)
    row = pl.BlockSpec((1, d), lambda i: (0, 0))
    return pl.pallas_call(
        _normmod_kernel, grid=(t // tm,),
        in_specs=[pl.BlockSpec((tm, d), lambda i: (i, 0)), row, row, row],
        out_specs=pl.BlockSpec((tm, d), lambda i: (i, 0)),
        out_shape=jax.ShapeDtypeStruct((t, d), BF16),
        compiler_params=_cparams(("parallel",)), name="normmod")(x, g.reshape(1, d), sc, sh)


def _prep_kernel(qa_ref, qi_ref, qb_ref, kvb_ref, ka_ref, va_ref, tail_ref,
                 gqa_ref, gka_ref, gqb_ref, gslc_ref, gwin_ref,
                 qa_o, qi_o, qb_o, kvb_o, ka_o, va_o, ki_o, wi_o, gs_o):
    qscale = HEAD_DIM ** -0.5 * LOG2E
    for h in range(DSA_HEADS):
        qa_o[h] = _head_norm(qa_ref[:, h * 128:(h + 1) * 128], gqa_ref[...], qscale).astype(BF16)
    for h in range(NSA_HEADS):
        qb_o[h] = _head_norm(qb_ref[:, h * 128:(h + 1) * 128], gqb_ref[...], qscale).astype(BF16)
    for h in range(IDX_HEADS):
        qi_o[h] = (qi_ref[:, h * IDX_DIM:(h + 1) * IDX_DIM] * IDX_DIM ** -0.5).astype(BF16)
    for g in range(DSA_KV_HEADS):
        ka_o[g] = _head_norm(ka_ref[:, g * 128:(g + 1) * 128], gka_ref[...], 1.0).astype(BF16)
        va_o[g] = va_ref[:, g * 128:(g + 1) * 128].astype(BF16)
    for s in range(6 * NSA_KV_HEADS):
        xs = kvb_ref[:, s * 128:(s + 1) * 128]
        part = s // NSA_KV_HEADS
        if part == 2:
            xs = _head_norm(xs, gslc_ref[...], 1.0)
        elif part == 4:
            xs = _head_norm(xs, gwin_ref[...], 1.0)
        kvb_o[s] = xs.astype(BF16)
    tail = tail_ref[...]
    ki_o[...] = tail[:, 0:IDX_DIM].astype(BF16)
    wi_o[...] = tail[:, 0:128] * IDX_HEADS ** -0.5
    gs_o[...] = jax.nn.sigmoid(tail[:, 128:256])


def _prep(pm, pt, gq_a, gk_a, gq_b, gk_slc, gk_win):
    t = pm.shape[0]
    tm = _pick(t, 128)
    g128 = pl.BlockSpec((1, 128), lambda i: (0, 0))

    def col(w, off):
        assert off % w == 0
        return pl.BlockSpec((tm, w), lambda i: (i, off // w))

    def hm(n, w):
        return pl.BlockSpec((n, tm, w), lambda i: (0, i, 0))

    row = lambda w: pl.BlockSpec((tm, w), lambda i: (i, 0))
    outs = pl.pallas_call(
        _prep_kernel, grid=(t // tm,),
        in_specs=[col(2048, OFF_QA), col(2048, OFF_QI), col(2048, OFF_QB), col(3072, OFF_KVB),
                  col(512, OFF_KA), col(512, OFF_VA), row(256), g128, g128, g128, g128, g128],
        out_specs=[hm(DSA_HEADS, 128), hm(IDX_HEADS, IDX_DIM), hm(NSA_HEADS, 128), hm(24, 128),
                   hm(DSA_KV_HEADS, 128), hm(DSA_KV_HEADS, 128), row(IDX_DIM), row(128), row(128)],
        out_shape=[jax.ShapeDtypeStruct((DSA_HEADS, t, 128), BF16),
                   jax.ShapeDtypeStruct((IDX_HEADS, t, IDX_DIM), BF16),
                   jax.ShapeDtypeStruct((NSA_HEADS, t, 128), BF16),
                   jax.ShapeDtypeStruct((24, t, 128), BF16),
                   jax.ShapeDtypeStruct((DSA_KV_HEADS, t, 128), BF16),
                   jax.ShapeDtypeStruct((DSA_KV_HEADS, t, 128), BF16),
                   jax.ShapeDtypeStruct((t, IDX_DIM), BF16),
                   jax.ShapeDtypeStruct((t, 128), F32),
                   jax.ShapeDtypeStruct((t, 128), F32)],
        compiler_params=_cparams(("parallel",)), name="prep")(
            pm, pm, pm, pm, pm, pm, pt,
            gq_a.reshape(1, 128), gk_a.reshape(1, 128), gq_b.reshape(1, 128),
            gk_slc.reshape(1, 128), gk_win.reshape(1, 128))
    return outs


def _bucket_of(dist):
    bucket = jnp.zeros(dist.shape, jnp.int32)
    for b in range(1, REL_BUCKETS):
        bucket = jnp.where(dist >= BUCKET_STARTS[b], b, bucket)
    return bucket


def _bias_row(tbl_ref, h, rows):
    trow = (tbl_ref[h:h + 1, :] - tbl_ref[h:h + 1, REL_BUCKETS - 1:REL_BUCKETS]) * LOG2E
    return jnp.broadcast_to(trow, (rows, LANE))


def _bias_tiles_kernel(tbl_ref, o_ref, *, tq, tk, nheads):
    d = pl.program_id(0)
    dist = d * tq + lax.broadcasted_iota(jnp.int32, (tq, tk), 0) - lax.broadcasted_iota(jnp.int32, (tq, tk), 1)
    bucket = _bucket_of(dist)
    for h in range(nheads):
        trow = _bias_row(tbl_ref, h, tq)
        o_ref[h, 0] = jnp.concatenate(
            [jnp.take_along_axis(trow, bucket[:, c * LANE:(c + 1) * LANE], axis=1) for c in range(tk // LANE)], axis=1)


def _n_near(tq, tk):
    return -(-(FAR_DIST + tk - 1) // tq)


def _bias_tiles(table, tq, tk):
    nheads = table.shape[1]
    nd = _n_near(tq, tk)
    return pl.pallas_call(
        functools.partial(_bias_tiles_kernel, tq=tq, tk=tk, nheads=nheads), grid=(nd,),
        in_specs=[pl.BlockSpec((nheads, LANE), lambda d: (0, 0))],
        out_specs=pl.BlockSpec((nheads, 1, tq, tk), lambda d: (0, d, 0, 0)),
        out_shape=jax.ShapeDtypeStruct((nheads, nd, tq, tk), F32),
        compiler_params=_cparams(("parallel",)), name="bias_tiles")(_table_rows(table))


def _table_rows(table):
    return jnp.pad(table.T, ((0, 0), (0, LANE - REL_BUCKETS)))


def _idx_kernel(q_ref, kt_ref, w_ref, key_ref, thr_ref, s_ref, wb_ref, *, tq, tk, ts, t, kk):
    i = pl.program_id(0)
    nkt = (i * tq + tq + tk - 1) // tk
    w = w_ref[...]
    for h in range(IDX_HEADS):
        wb_ref[h] = jnp.broadcast_to(w[:, IDX_DIM + h:IDX_DIM + h + 1], (tq, LANE))
    qall = q_ref[...].reshape(IDX_HEADS * tq, IDX_DIM)
    rows = lax.broadcasted_iota(jnp.int32, (tq, ts), 0)
    cols = lax.broadcasted_iota(jnp.int32, (tq, ts), 1)

    def tile(j, carry):
        offs = [pl.multiple_of(j * tk + u * ts, ts) for u in range(tk // ts)]
        for u, off in enumerate(offs):
            s_ref[u] = jnp.dot(qall, kt_ref[:, pl.ds(off, ts)], preferred_element_type=F32)
        for u, off in enumerate(offs):
            acc = jnp.zeros((tq, ts), F32)
            for h in range(IDX_HEADS):
                wbh = jnp.concatenate([wb_ref[h]] * (ts // LANE), axis=1)
                acc = acc + jnp.maximum(s_ref[u, h * tq:(h + 1) * tq, :], 0.0) * wbh
            acc = jnp.where(off + cols <= i * tq + rows, acc, -jnp.inf)
            key_ref[:, pl.ds(off, ts)] = _to_key(acc)
        return carry

    lax.fori_loop(0, nkt, tile, 0)

    def fill(j, carry):
        key_ref[:, pl.ds(pl.multiple_of(j * tk, tk), tk)] = jnp.full((tq, tk), KEY_NEG_INF, jnp.int32)
        return carry

    lax.fori_loop(nkt, t // tk, fill, 0)

    def count_ge(cand):
        def body(j, c):
            kt = key_ref[:, pl.ds(pl.multiple_of(j * tk, tk), tk)]
            ge = jnp.where(kt >= cand, 1.0, 0.0)
            for a in range(tk // LANE):
                c = c + ge[:, a * LANE:(a + 1) * LANE]
            return c
        c = lax.fori_loop(0, nkt, body, jnp.zeros((tq, LANE), F32))
        return jnp.sum(c, axis=-1, keepdims=True)

    def score_range(j, c):
        smin, smax = c
        st = _from_key(key_ref[:, pl.ds(pl.multiple_of(j * tk, tk), tk)])
        lo_t = jnp.where(st == -jnp.inf, jnp.inf, st)
        for a in range(tk // LANE):
            smin = jnp.minimum(smin, lo_t[:, a * LANE:(a + 1) * LANE])
            smax = jnp.maximum(smax, st[:, a * LANE:(a + 1) * LANE])
        return smin, smax

    smin, smax = lax.fori_loop(0, nkt, score_range, (jnp.full((tq, LANE), jnp.inf, F32),
                                                     jnp.full((tq, LANE), -jnp.inf, F32)))
    kmin = _to_key(jnp.min(smin, axis=-1, keepdims=True))
    kmax = _to_key(jnp.max(smax, axis=-1, keepdims=True))
    thr_ref[...] = _kth_key_bracket(count_ge, kmin, kmax, float(kk))


def _dsa_index(qi_h, ki_t, wi, kk):
    t = ki_t.shape[1]
    tq, tk = _pick(t, 128), _pick(t, 512)
    ts = tk // 2
    return pl.pallas_call(
        functools.partial(_idx_kernel, tq=tq, tk=tk, ts=ts, t=t, kk=kk), grid=(t // tq,),
        in_specs=[pl.BlockSpec((IDX_HEADS, tq, IDX_DIM), lambda i: (0, i, 0)),
                  pl.BlockSpec((IDX_DIM, t), lambda i: (0, 0)),
                  pl.BlockSpec((tq, 128), lambda i: (i, 0))],
        out_specs=[pl.BlockSpec((tq, t), lambda i: (i, 0)), pl.BlockSpec((tq, 1), lambda i: (i, 0))],
        out_shape=[jax.ShapeDtypeStruct((t, t), jnp.int32), jax.ShapeDtypeStruct((t, 1), jnp.int32)],
        scratch_shapes=[pltpu.VMEM((tk // ts, IDX_HEADS * tq, ts), F32), pltpu.VMEM((IDX_HEADS, tq, LANE), F32)],
        compiler_params=_cparams(("parallel",)), name="dsa_index")(qi_h, ki_t, wi)


def _first_tile(mode, i, tq, tk):
    if mode != "win":
        return 0
    start = i * tq - (WIN - 1)
    return (max(start, 0) if isinstance(i, int) else jnp.maximum(start, 0)) // tk


def _last_tile(i, tq, tk):
    return (i * tq + tq - 1) // tk


def _flash_kernel(qi_ref, kj_ref, *refs, mode, tq, tk, nheads, ngroups, nnear):
    refs, (m_ref, l_ref, acc_ref, s_ref, p_ref, madd_ref) = refs[:-6], refs[-6:]
    if mode == "dsa":
        q_ref, k_ref, v_ref, b_ref, key_ref, thr_ref, o_ref = refs
    elif mode == "slc":
        q_ref, k_ref, v_ref, b_ref, sel_ref, e_ref, o_ref = refs
    else:
        q_ref, k_ref, v_ref, b_ref, o_ref = refs
    rep = nheads // ngroups
    nch = tk // LANE
    rblk = min(tq, 64)
    s = pl.program_id(0)
    i, j = qi_ref[s], kj_ref[s]
    jlast = _last_tile(i, tq, tk)
    d = i - (tk // tq) * j

    @pl.when(j == _first_tile(mode, i, tq, tk))
    def _():
        m_ref[...] = jnp.full(m_ref.shape, NEG, F32)
        l_ref[...] = jnp.zeros(l_ref.shape, F32)
        acc_ref[...] = jnp.zeros(acc_ref.shape, F32)

    def step(near):
        if near:
            dist = ((i * tq - j * tk) + lax.broadcasted_iota(jnp.int32, (tq, tk), 0)
                    - lax.broadcasted_iota(jnp.int32, (tq, tk), 1))
            causal = dist >= 0
            if mode == "win":
                causal = causal & (dist < WIN)
        if mode == "dsa":
            hit = key_ref[...] >= thr_ref[...]
            madd_ref[...] = jnp.where((hit & causal) if near else hit, 0.0, NEG)
        elif mode == "win":
            madd_ref[...] = jnp.where(causal, 0.0, NEG)
        else:
            hit = jnp.dot(sel_ref[...].reshape(ngroups * tq, sel_ref.shape[2]), e_ref[...],
                          preferred_element_type=F32) > 0.5
            for g in range(ngroups):
                hg = hit[g * tq:(g + 1) * tq]
                madd_ref[g * tq:(g + 1) * tq, :] = jnp.where((hg & causal) if near else hg, 0.0, NEG)
        for g in range(ngroups):
            s_ref[...] = _dot_nt(q_ref[g * rep:(g + 1) * rep].reshape(rep * tq, HEAD_DIM), k_ref[g])
            mrow = g * tq if mode == "slc" else 0
            for r in range(rep):
                h = g * rep + r
                for rb in range(tq // rblk):
                    rows = slice(rb * rblk, (rb + 1) * rblk)
                    srows = slice(r * tq + rb * rblk, r * tq + (rb + 1) * rblk)
                    sh = s_ref[srows, :] + madd_ref[mrow + rb * rblk:mrow + (rb + 1) * rblk, :]
                    if near:
                        sh = sh + b_ref[h, 0, rows, :]
                    chunks = [sh[:, c * LANE:(c + 1) * LANE] for c in range(nch)]
                    m_old = m_ref[h, rows, :]
                    tile_max = jnp.max(functools.reduce(jnp.maximum, chunks), axis=-1, keepdims=True)
                    m_new = jnp.maximum(m_old, tile_max)
                    alpha = jnp.exp2(m_old - m_new)
                    pcs = [jnp.exp2(c - m_new) for c in chunks]
                    l_ref[h, rows, :] = alpha * l_ref[h, rows, :] + functools.reduce(jnp.add, pcs)
                    m_ref[h, rows, :] = m_new
                    acc_ref[h, rows, :] = acc_ref[h, rows, :] * alpha
                    p_ref[srows, :] = jnp.concatenate(pcs, axis=1).astype(BF16)
            pv = jnp.dot(p_ref[...], v_ref[g], preferred_element_type=F32)
            for r in range(rep):
                h = g * rep + r
                acc_ref[h] = acc_ref[h] + pv[r * tq:(r + 1) * tq]

    if mode == "win":
        step(True)
    else:
        pl.when(d < nnear)(lambda: step(True))
        pl.when(d >= nnear)(lambda: step(False))

    @pl.when(j == jlast)
    def _():
        for h in range(nheads):
            l = jnp.sum(l_ref[h], axis=-1, keepdims=True)
            o_ref[:, h * HEAD_DIM:(h + 1) * HEAD_DIM] = (acc_ref[h] / l).astype(o_ref.dtype)


def _flash(mode, q_h, k_h, v_h, bias, extra, out_dtype):
    nheads, t, _ = q_h.shape
    ngroups = k_h.shape[0]
    rep = nheads // ngroups
    _, nnear, tq, tk = bias.shape
    nq = t // tq
    pairs = [(i, j) for i in range(nq) for j in range(_first_tile(mode, i, tq, tk), _last_tile(i, tq, tk) + 1)]
    qi = jnp.asarray([p[0] for p in pairs], jnp.int32)
    kj = jnp.asarray([p[1] for p in pairs], jnp.int32)

    def bmap(s, qi, kj):
        return jnp.minimum(qi[s] - (tk // tq) * kj[s], nnear - 1)

    in_specs = [pl.BlockSpec((nheads, tq, HEAD_DIM), lambda s, qi, kj: (0, qi[s], 0)),
                pl.BlockSpec((ngroups, tk, HEAD_DIM), lambda s, qi, kj: (0, kj[s], 0)),
                pl.BlockSpec((ngroups, tk, HEAD_DIM), lambda s, qi, kj: (0, kj[s], 0)),
                pl.BlockSpec((nheads, 1, tq, tk), lambda s, qi, kj: (0, bmap(s, qi, kj), 0, 0))]
    if mode == "dsa":
        in_specs += [pl.BlockSpec((tq, tk), lambda s, qi, kj: (qi[s], kj[s])),
                     pl.BlockSpec((tq, 1), lambda s, qi, kj: (qi[s], 0))]
    elif mode == "slc":
        ns = extra[0].shape[2]
        in_specs += [pl.BlockSpec((ngroups, tq, ns), lambda s, qi, kj: (0, qi[s], 0)),
                     pl.BlockSpec((ns, tk), lambda s, qi, kj: (0, kj[s]))]
    grid_spec = pltpu.PrefetchScalarGridSpec(
        num_scalar_prefetch=2, grid=(len(pairs),), in_specs=in_specs,
        out_specs=pl.BlockSpec((tq, nheads * HEAD_DIM), lambda s, qi, kj: (qi[s], 0)),
        scratch_shapes=[pltpu.VMEM((nheads, tq, LANE), F32), pltpu.VMEM((nheads, tq, LANE), F32),
                        pltpu.VMEM((nheads, tq, HEAD_DIM), F32),
                        pltpu.VMEM((rep * tq, tk), F32), pltpu.VMEM((rep * tq, tk), BF16),
                        pltpu.VMEM(((ngroups if mode == "slc" else 1) * tq, tk), F32)])
    return pl.pallas_call(
        functools.partial(_flash_kernel, mode=mode, tq=tq, tk=tk, nheads=nheads, ngroups=ngroups, nnear=nnear),
        grid_spec=grid_spec, out_shape=jax.ShapeDtypeStruct((t, nheads * HEAD_DIM), out_dtype),
        compiler_params=_cparams(("arbitrary",)), name="flash_" + mode)(qi, kj, q_h, k_h, v_h, bias, *extra)


def _compress_kernel(c_ref, w1_ref, pe_ref, w2_ref, g_ref, o_ref, *, norm):
    c = c_ref[0]
    half = CMP_STRIDE * HEAD_DIM
    ncp = c.shape[0]
    a = jnp.dot(c, w1_ref[0:half], preferred_element_type=F32)
    b = jnp.dot(c, w1_ref[half:2 * half], preferred_element_type=F32)
    pet = jnp.dot(pe_ref[...], w1_ref[...], preferred_element_type=F32)[0:1]
    hid = _gelu(a + pltpu.roll(b, ncp - 1, 0) + pet)
    y = jnp.dot(hid.astype(BF16), w2_ref[...], preferred_element_type=F32)
    if norm:
        y = _head_norm(y, g_ref[...], 1.0)
    o_ref[0] = y.astype(BF16)


def _compress(slabs, base, pe, w1, w2, gain, norm):
    _, t, _ = slabs.shape
    ncp = t // CMP_STRIDE
    width = CMP_STRIDE * HEAD_DIM
    chunks = slabs.reshape(slabs.shape[0], ncp, width)
    w1f = w1.reshape(CMP_LEN * HEAD_DIM, CMP_HIDDEN).astype(BF16)
    pe8 = jnp.broadcast_to(pe.reshape(1, CMP_LEN * HEAD_DIM).astype(BF16), (8, CMP_LEN * HEAD_DIM))
    full = lambda shp: pl.BlockSpec(shp, lambda g: (0,) * len(shp))
    return pl.pallas_call(
        functools.partial(_compress_kernel, norm=norm), grid=(NSA_KV_HEADS,),
        in_specs=[pl.BlockSpec((1, ncp, width), lambda g: (base + g, 0, 0)),
                  full((CMP_LEN * HEAD_DIM, CMP_HIDDEN)), full((8, CMP_LEN * HEAD_DIM)),
                  full((CMP_HIDDEN, HEAD_DIM)), full((1, HEAD_DIM))],
        out_specs=pl.BlockSpec((1, ncp, HEAD_DIM), lambda g: (g, 0, 0)),
        out_shape=jax.ShapeDtypeStruct((NSA_KV_HEADS, ncp, HEAD_DIM), BF16),
        compiler_params=_cparams(("parallel",)), name="compress")(
            chunks, w1f, pe8, w2.astype(BF16), gain.reshape(1, HEAD_DIM))


def _cmp_attn_kernel(tbl_ref, q_ref, kc_ref, vc_ref, ov_ref, oc_ref, sel_ref, s_ref, key_ref,
                     *, tq, ncp, ns, nsel, ww):
    i = pl.program_id(0)
    rep = NSA_HEADS // NSA_KV_HEADS
    qpos = i * tq + lax.broadcasted_iota(jnp.int32, (tq, ncp), 0)
    dist = qpos - (lax.broadcasted_iota(jnp.int32, (tq, ncp), 1) * CMP_STRIDE + CMP_LEN - 1)
    mask = dist >= 0
    w0 = jnp.maximum(i * tq - (CMP_LEN - 1) - FAR_DIST + CMP_STRIDE, 0) // (CMP_STRIDE * LANE) * LANE
    w0 = pl.multiple_of(jnp.minimum(w0, ncp - ww), LANE)
    dist_w = (i * tq + lax.broadcasted_iota(jnp.int32, (tq, ww), 0)
              - ((w0 + lax.broadcasted_iota(jnp.int32, (tq, ww), 1)) * CMP_STRIDE + CMP_LEN - 1))
    bucket = _bucket_of(dist_w)
    blk = lax.broadcasted_iota(jnp.int32, (tq, ns), 1)
    qp = i * tq + lax.broadcasted_iota(jnp.int32, (tq, ns), 0)
    cur = qp // SLC_LEN
    forced = (blk == 0) | (blk == cur) | (blk == cur - 1)
    admissible = blk * SLC_LEN <= qp
    for g in range(NSA_KV_HEADS):
        s_ref[...] = _dot_nt(q_ref[g * rep:(g + 1) * rep].reshape(rep * tq, HEAD_DIM), kc_ref[g])
        imp = jnp.zeros((tq, ncp), F32)
        ps = []
        for r in range(rep):
            h = g * rep + r
            trow = _bias_row(tbl_ref, h, tq)
            bias = [jnp.take_along_axis(trow, bucket[:, c * LANE:(c + 1) * LANE], axis=1)
                    for c in range(ww // LANE)]
            s_ref[r * tq:(r + 1) * tq, pl.ds(w0, ww)] += jnp.concatenate(bias, axis=1)
            sh = jnp.where(mask, s_ref[r * tq:(r + 1) * tq, :], NEG)
            m = jnp.max(sh, axis=-1, keepdims=True)
            p = jnp.where(mask, jnp.exp2(sh - m), 0.0)
            pc = p / jnp.maximum(jnp.sum(p, axis=-1, keepdims=True), 1e-30)
            imp = imp + pc
            ps.append(pc.astype(BF16))
        o = jnp.dot(jnp.concatenate(ps, axis=0), vc_ref[g], preferred_element_type=F32)
        for r in range(rep):
            h = g * rep + r
            oc_ref[:, h * HEAD_DIM:(h + 1) * HEAD_DIM] = o[r * tq:(r + 1) * tq]
        hi = imp.astype(BF16)
        lo = (imp - hi.astype(F32)).astype(BF16)
        impb = (jnp.dot(hi, ov_ref[...], preferred_element_type=F32)
                + jnp.dot(lo, ov_ref[...], preferred_element_type=F32))
        impb = jnp.where(forced, FORCE_SCORE, impb)
        key_ref[:, g * tq:(g + 1) * tq] = _to_key(jnp.where(admissible, impb, -jnp.inf).T)
    keys = key_ref[...]
    sel_t = jnp.where(keys >= _kth_key_val(keys, nsel, axis=0), 1.0, 0.0)
    for g in range(NSA_KV_HEADS):
        sel_ref[g] = sel_t[:, g * tq:(g + 1) * tq].T.astype(BF16)


def _cmp_attn(table, q_h, kc, vc):
    nheads, t, _ = q_h.shape
    ncp = kc.shape[1]
    ns = t // SLC_LEN
    nsel = min(SLC_TOP, ns)
    tq = _pick(t, 128)
    ww = min(ncp, LANE * (-(-(FAR_DIST + tq + CMP_STRIDE * LANE) // (CMP_STRIDE * LANE))))
    rep = nheads // NSA_KV_HEADS
    c_start = np.arange(ncp)[:, None] * CMP_STRIDE
    s_start = np.arange(ns)[None, :] * SLC_LEN
    overlap = jnp.asarray((c_start < s_start + SLC_LEN) & (c_start + CMP_LEN > s_start), BF16)
    full = lambda shp: pl.BlockSpec(shp, lambda i: (0,) * len(shp))
    return pl.pallas_call(
        functools.partial(_cmp_attn_kernel, tq=tq, ncp=ncp, ns=ns, nsel=nsel, ww=ww), grid=(t // tq,),
        in_specs=[full((nheads, LANE)),
                  pl.BlockSpec((nheads, tq, HEAD_DIM), lambda i: (0, i, 0)),
                  full((NSA_KV_HEADS, ncp, HEAD_DIM)), full((NSA_KV_HEADS, ncp, HEAD_DIM)), full((ncp, ns))],
        out_specs=[pl.BlockSpec((tq, nheads * HEAD_DIM), lambda i: (i, 0)),
                   pl.BlockSpec((NSA_KV_HEADS, tq, ns), lambda i: (0, i, 0))],
        out_shape=[jax.ShapeDtypeStruct((t, nheads * HEAD_DIM), F32),
                   jax.ShapeDtypeStruct((NSA_KV_HEADS, t, ns), BF16)],
        scratch_shapes=[pltpu.VMEM((rep * tq, ncp), F32), pltpu.VMEM((ns, NSA_KV_HEADS * tq), jnp.int32)],
        compiler_params=_cparams(("parallel",)), name="cmp_attn")(_table_rows(table), q_h, kc, vc, overlap)


def _combine_kernel(oc_ref, os_ref, ow_ref, g_ref, y_ref):
    g = g_ref[...]
    for h in range(NSA_HEADS):
        sl = slice(h * HEAD_DIM, (h + 1) * HEAD_DIM)
        y = (g[:, 3 * h:3 * h + 1] * oc_ref[:, sl] + g[:, 3 * h + 1:3 * h + 2] * os_ref[:, sl]
             + g[:, 3 * h + 2:3 * h + 3] * ow_ref[:, sl])
        y_ref[:, sl] = y.astype(y_ref.dtype)


def _combine(oc, os_, ow, gsig):
    t, w = oc.shape
    tm = _pick(t, 256)
    blk = pl.BlockSpec((tm, w), lambda i: (i, 0))
    return pl.pallas_call(
        _combine_kernel, grid=(t // tm,),
        in_specs=[blk, blk, blk, pl.BlockSpec((tm, 128), lambda i: (i, 0))],
        out_specs=blk, out_shape=jax.ShapeDtypeStruct((t, w), BF16),
        compiler_params=_cparams(("parallel",)), name="combine")(oc, os_, ow, gsig)


def _merge_kernel(ya_ref, yb_ref, wa_ref, wb_ref, ga_ref, gb_ref, o_ref):
    za = jnp.dot(ya_ref[...], wa_ref[...], preferred_element_type=F32)
    zb = jnp.dot(yb_ref[...], wb_ref[...], preferred_element_type=F32)
    o_ref[...] = (jax.nn.sigmoid(ga_ref[...]) * za + jax.nn.sigmoid(gb_ref[...]) * zb).astype(o_ref.dtype)


def _merge(y_a, y_b, w_a, w_b, pm):
    t, k = y_a.shape
    d = w_a.shape[1]
    tm, tn = _pick(t, 1024), _pick(d, 512)
    assert OFF_GM % tn == 0
    lhs = pl.BlockSpec((tm, k), lambda i, j: (i, 0))
    rhs = pl.BlockSpec((k, tn), lambda i, j: (0, j))
    return pl.pallas_call(
        _merge_kernel, grid=(t // tm, d // tn),
        in_specs=[lhs, lhs, rhs, rhs, pl.BlockSpec((tm, tn), lambda i, j: (i, OFF_GM // tn + j)),
                  pl.BlockSpec((tm, tn), lambda i, j: (i, (OFF_GM + d) // tn + j))],
        out_specs=pl.BlockSpec((tm, tn), lambda i, j: (i, j)), out_shape=jax.ShapeDtypeStruct((t, d), BF16),
        compiler_params=_cparams(("parallel", "arbitrary")), name="merge")(y_a, y_b, w_a, w_b, pm, pm)


_CAND_FULL = PEER_TOPK // 2


def _peer_cands(a1, a2, op):
    pieces = [op(a1[0:1], a2)]
    pieces += [op(a1[k:k + 1], a2[0:_CAND_FULL]) for k in range(1, _CAND_FULL)]
    pieces.append(op(a1[_CAND_FULL:], a2[0:1]))
    return jnp.concatenate(pieces, axis=0)


def _peer_score_kernel(q_ref, sk_ref, s1_o, e1_o, s2_o, e2_o, thr_o, *, tm):
    row = lax.broadcasted_iota(jnp.int32, (PEER_NKEYS, tm), 0).astype(F32)
    for h in range(PEER_HEADS):
        svals, tops = [], []
        for c in range(2):
            hc = 2 * h + c
            s = _dot_nt(sk_ref[hc], q_ref[:, hc * 128:(hc + 1) * 128].astype(BF16))
            svals.append(s)
            work, top = s, []
            for _ in range(PEER_TOPK):
                mx = jnp.max(work, axis=0, keepdims=True)
                first = jnp.min(jnp.where(work == mx, row, float(PEER_NKEYS)), axis=0, keepdims=True)
                work = jnp.where(row == first, -jnp.inf, work)
                top.append(mx)
            tops.append(jnp.concatenate(top, axis=0))
        a1, a2 = tops
        cand = _peer_cands(a1, a2, jnp.add)
        thr = _from_key(_kth_key_val(_to_key(cand), PEER_TOPK, axis=0))
        m1, m2 = a1[0:1], a2[0:1]
        ec = _peer_cands(jnp.exp(a1 - m1), jnp.exp(a2 - m2), jnp.multiply)
        z = jnp.sum(jnp.where(cand >= thr, ec, 0.0), axis=0, keepdims=True)
        s1_o[h] = svals[0]
        s2_o[h] = svals[1]
        e1_o[h] = jnp.exp(svals[0] - m1)
        e2_o[h] = jnp.exp(svals[1] - m2) / z
        thr_o[h:h + 1, :] = thr


def _peer_score(qp, sub_keys):
    t = qp.shape[0]
    tm = _pick(t, 256)
    sk = sub_keys.reshape(2 * PEER_HEADS, PEER_NKEYS, PEER_QDIM // 2).astype(BF16)
    tr = pl.BlockSpec((PEER_HEADS, PEER_NKEYS, tm), lambda i: (0, 0, i))
    shp = jax.ShapeDtypeStruct((PEER_HEADS, PEER_NKEYS, t), F32)
    return pl.pallas_call(
        functools.partial(_peer_score_kernel, tm=tm), grid=(t // tm,),
        in_specs=[pl.BlockSpec((tm, 2 * PEER_HEADS * 128), lambda i: (i, 0)),
                  pl.BlockSpec((2 * PEER_HEADS, PEER_NKEYS, PEER_QDIM // 2), lambda i: (0, 0, 0))],
        out_specs=[tr, tr, tr, tr, pl.BlockSpec((PEER_HEADS, tm), lambda i: (0, i))],
        out_shape=[shp, shp, shp, shp, jax.ShapeDtypeStruct((PEER_HEADS, t), F32)],
        compiler_params=_cparams(("parallel",)), name="peer_score")(qp, sk)


def _peer_act_kernel(h_ref, u_ref, s1_ref, e1_ref, s2_ref, e2_ref, thr_ref, o_ref, w_ref, a_ref, *, tm, te, sub):
    j = pl.program_id(1)
    n1 = te // PEER_NKEYS
    a_ref[...] = _dot_nt(h_ref[...], u_ref[...])
    for ai in range(n1):
        i1 = j * n1 + ai
        s1rows = [s1_ref[h, pl.ds(i1, 1), :] for h in range(PEER_HEADS)]
        e1rows = [e1_ref[h, pl.ds(i1, 1), :] for h in range(PEER_HEADS)]
        for ts in range(tm // LANE):
            tok = slice(ts * LANE, (ts + 1) * LANE)
            wt = jnp.zeros((PEER_NKEYS, LANE), F32)
            for h in range(PEER_HEADS):
                hit = (s1rows[h][:, tok] + s2_ref[h, :, tok]) >= thr_ref[h:h + 1, tok]
                wt = wt + jnp.where(hit, e1rows[h][:, tok] * e2_ref[h, :, tok], 0.0)
            w_ref[tok, ai * PEER_NKEYS:(ai + 1) * PEER_NKEYS] = wt.T
    for r in range(tm // sub):
        rows = slice(r * sub, (r + 1) * sub)
        o_ref[rows, :] = (_gelu(a_ref[rows, :]) * w_ref[rows, :]).astype(o_ref.dtype)


def _peer_act(h2, u, s1, e1, s2, e2, thr):
    t, d = h2.shape
    ne = u.shape[0]
    tm, te, sub = _pick(t, 1024), 512, 256
    once = dict(pipeline_mode=pl.Buffered(1))
    tok = pl.BlockSpec((PEER_HEADS, PEER_NKEYS, tm), lambda i, j: (0, 0, i), **once)
    return pl.pallas_call(
        functools.partial(_peer_act_kernel, tm=tm, te=te, sub=sub), grid=(t // tm, ne // te),
        in_specs=[pl.BlockSpec((tm, d), lambda i, j: (i, 0), **once),
                  pl.BlockSpec((te, d), lambda i, j: (j, 0)),
                  tok, tok, tok, tok, pl.BlockSpec((PEER_HEADS, tm), lambda i, j: (0, i), **once)],
        out_specs=pl.BlockSpec((tm, te), lambda i, j: (i, j)),
        out_shape=jax.ShapeDtypeStruct((t, ne), BF16),
        scratch_shapes=[pltpu.VMEM((tm, te), F32), pltpu.VMEM((tm, te), F32)],
        compiler_params=_cparams(("parallel", "arbitrary")), name="peer_act")(h2, u, s1, e1, s2, e2, thr)


def _pack_w_in(w_in, d):
    offs = np.cumsum([0, DSA_HEADS * 128, DSA_KV_HEADS * 128, DSA_KV_HEADS * 128, IDX_HEADS * IDX_DIM, IDX_DIM,
                      IDX_HEADS, NSA_HEADS * 128, 6 * NSA_KV_HEADS * 128, 3 * NSA_HEADS, 2 * d])
    qa, ka, va, qi, ki, wi, qb, kvb, gb, gm = [w_in[:, int(offs[n]):int(offs[n + 1])] for n in range(10)]
    main = jnp.concatenate([qa, qi, qb, kvb, ka, va, gm], axis=1).astype(BF16)
    pad = lambda n: jnp.zeros((d, n), w_in.dtype)
    tail = jnp.concatenate([ki, wi, pad(128 - IDX_DIM - IDX_HEADS), gb, pad(128 - 3 * NSA_HEADS)], axis=1).astype(BF16)
    return main, tail


def _token_mixer(h, table, w_in, gq_a, gk_a, gq_b, gk_cmp, gk_slc, gk_win, cmp_pe_k, cmp_w1_k, cmp_w2_k,
                 cmp_pe_v, cmp_w1_v, cmp_w2_v, w_branch_a, w_branch_b):
    t, d = h.shape
    w_main, w_tail = _pack_w_in(w_in, d)
    pm = _matmul(h, w_main, name="proj_main")
    pt = _matmul(h, w_tail, name="proj_tail")
    qa_h, qi_h, qb_h, kvb_h, ka_h, va_h, ki, wi, gsig = _prep(pm, pt, gq_a, gk_a, gq_b, gk_slc, gk_win)
    tq, tk = _pick(t, 256), _pick(t, 512)
    keys, thr = _dsa_index(qi_h, ki.T, wi, min(DSA_TOPK, t // 4))
    bias_a = _bias_tiles(table[:, :DSA_HEADS], tq, tk)
    y_a = _flash("dsa", qa_h, ka_h, va_h, bias_a, (keys, thr), BF16)
    g = NSA_KV_HEADS
    kc = _compress(kvb_h, 0, cmp_pe_k, cmp_w1_k, cmp_w2_k, gk_cmp, True)
    vc = _compress(kvb_h, g, cmp_pe_v, cmp_w1_v, cmp_w2_v, gk_cmp, False)
    o_c, sel = _cmp_attn(table[:, DSA_HEADS:], qb_h, kc, vc)
    bias_b = _bias_tiles(table[:, DSA_HEADS:], tq, tk)
    ns = t // SLC_LEN
    expand = jnp.asarray(np.arange(ns)[:, None] == (np.arange(t)[None, :] // SLC_LEN), BF16)
    o_s = _flash("slc", qb_h, kvb_h[2 * g:3 * g], kvb_h[3 * g:4 * g], bias_b, (sel, expand), F32)
    o_w = _flash("win", qb_h, kvb_h[4 * g:5 * g], kvb_h[5 * g:6 * g], bias_b, (), F32)
    y_b = _combine(o_c, o_s, o_w, gsig)
    return _merge(y_a, y_b, w_branch_a.astype(BF16), w_branch_b.astype(BF16), pm)


def _peer_ffn(h2, x1, gt2, w_q, sub_keys, u, v):
    qp = _matmul(h2, w_q.astype(BF16), name="peer_q")
    s1, e1, s2, e2, thr = _peer_score(qp, sub_keys)
    act = _peer_act(h2, u.astype(BF16), s1, e1, s2, e2, thr)
    return _matmul_resid(act, v.astype(BF16), x1, gt2, name="peer_out")


def kernel(x, c, rel_bias, w_ada, b_ada, g_mix, w_in, gq_a, gk_a, gq_b, gk_cmp, gk_slc, gk_win, cmp_pe_k, cmp_w1_k,
           cmp_w2_k, cmp_pe_v, cmp_w1_v, cmp_w2_v, w_branch_a, w_branch_b, w_out, g_ffn, w_peer_q, peer_sub_keys,
           peer_u, peer_v):
    bsz, t, d = x.shape
    assert bsz == 1 and t % 512 == 0 and d % 128 == 0
    xs = x[0]
    for i in range(w_ada.shape[0]):
        mod = _ada(c, w_ada[i], b_ada[i])
        sh1, sc1, gt1, sh2, sc2, gt2 = [mod[:, n * d:(n + 1) * d] for n in range(6)]
        h = _normmod(xs, g_mix[i], sc1, sh1)
        merged = _token_mixer(h, rel_bias, w_in[i], gq_a[i], gk_a[i], gq_b[i], gk_cmp[i], gk_slc[i], gk_win[i],
                              cmp_pe_k[i], cmp_w1_k[i], cmp_w2_k[i], cmp_pe_v[i], cmp_w1_v[i], cmp_w2_v[i],
                              w_branch_a[i], w_branch_b[i])
        x1 = _matmul_resid(merged, w_out[i].astype(BF16), xs, gt1, name="w_out")
        h2 = _normmod(x1, g_ffn[i], sc2, sh2)
        xs = _peer_ffn(h2, x1, gt2, w_peer_q[i], peer_sub_keys[i], peer_u[i], peer_v[i])
    return xs[None]
```

```python
import functools
import math

import numpy as np
import jax
import jax.numpy as jnp
from jax import lax
from jax.experimental import pallas as pl
from jax.experimental.pallas import tpu as pltpu

HEAD_DIM = 128
DSA_HEADS = 16
DSA_KV_HEADS = 4
IDX_HEADS = 32
IDX_DIM = 64
DSA_TOPK = 256
NSA_HEADS = 16
NSA_KV_HEADS = 4
CMP_LEN = 32
CMP_STRIDE = 16
CMP_HIDDEN = 256
SLC_LEN = 64
SLC_TOP = 16
WIN = 512
FORCE_SCORE = 1e9
PEER_HEADS = 8
PEER_NKEYS = 128
PEER_QDIM = 256
PEER_TOPK = 16
REL_BUCKETS = 32
REL_MAX_DIST = 2048
EPS = 1e-6

LANE = 128
VMEM_LIMIT = 56 * 1024 * 1024
NEG = -1e30
LOG2E = math.log2(math.e)
INT_MIN = -2 ** 31
KEY_NEG_INF = -2 ** 31 + 0x007FFFFF

OFF_QA, OFF_QI, OFF_QB, OFF_KVB, OFF_KA, OFF_VA, OFF_GM = 0, 2048, 4096, 6144, 9216, 9728, 10240

F32 = jnp.float32
BF16 = jnp.bfloat16


def _bucket_starts():
    n = np.arange(2 * REL_MAX_DIST)
    exact = REL_BUCKETS // 2
    nf = np.maximum(n, 1).astype(np.float32)
    lb = exact + (np.log(nf / np.float32(exact)) / np.float32(math.log(REL_MAX_DIST / exact))
                  * np.float32(REL_BUCKETS - exact)).astype(np.int32)
    bucket = np.where(n < exact, n, np.minimum(lb, REL_BUCKETS - 1))
    return [int(np.argmax(bucket >= b)) for b in range(REL_BUCKETS)]


BUCKET_STARTS = _bucket_starts()
FAR_DIST = BUCKET_STARTS[REL_BUCKETS - 1]


def _cparams(sem):
    return pltpu.CompilerParams(dimension_semantics=sem, vmem_limit_bytes=VMEM_LIMIT)


def _pick(n, pref):
    t = pref
    while n % t:
        t //= 2
    return t


def _to_key(x):
    b = lax.bitcast_convert_type(x, jnp.int32)
    return b ^ ((b >> 31) & 0x7FFFFFFF)


def _from_key(k):
    return lax.bitcast_convert_type(k ^ ((k >> 31) & 0x7FFFFFFF), F32)


def _kth_key(count_ge, shape, kk):
    def bit_pass(b, x):
        cand = x + lax.shift_left(jnp.int32(1), 31 - b)
        return jnp.where(count_ge(cand) >= kk, cand, x)
    x = lax.fori_loop(0, 32, bit_pass, jnp.full(shape, INT_MIN, jnp.int32))
    return jnp.maximum(x, KEY_NEG_INF)


def _kth_key_bracket(count_ge, kmin, kmax, kk):
    c_min = count_ge(kmin)
    few = c_min < kk
    lo0 = jnp.where(few, KEY_NEG_INF, kmin)
    hi0 = kmax + 1
    done0 = jnp.where(few | (c_min == kk) | (hi0 - 1 <= lo0), 1.0, 0.0)

    def cond(st):
        p, _, _, _, _, done = st
        return jnp.logical_and(p < 72, jnp.min(done) < 0.5)

    def body(st):
        p, lo, hi, clo, chi, done = st
        mid = (lo >> 1) + (hi >> 1) + (lo & hi & 1)
        flo, fhi = _from_key(lo), _from_key(hi)
        frac = (jnp.log(clo) - math.log(kk)) / (jnp.log(clo) - jnp.log(jnp.maximum(chi, 0.5)))
        cand = jnp.where(p % 2 == 1, mid, _to_key(flo + frac * (fhi - flo)))
        cand = jnp.minimum(jnp.maximum(cand, lo + 1), hi - 1)
        active = done < 0.5
        cand = jnp.where(active, cand, lo)
        cnt = count_ge(cand)
        up = active & (cnt >= kk)
        dn = active & (cnt < kk)
        lo, clo = jnp.where(up, cand, lo), jnp.where(up, cnt, clo)
        hi, chi = jnp.where(dn, cand, hi), jnp.where(dn, cnt, chi)
        done = jnp.where((clo == kk) | (hi - 1 <= lo), 1.0, done)
        return p + 1, lo, hi, clo, chi, done

    st = lax.while_loop(cond, body, (jnp.int32(0), lo0, hi0, c_min, jnp.zeros_like(c_min), done0))
    return st[1]


def _kth_largest(vals, kk, axis):
    shape = list(vals.shape)
    shape[axis] = 1

    def body(_, st):
        work, taken, thr = st
        mx = jnp.max(work, axis=axis, keepdims=True)
        hit = work == mx
        thr = jnp.where(taken < kk, mx, thr)
        taken = taken + jnp.sum(jnp.where(hit, 1.0, 0.0), axis=axis, keepdims=True)
        return jnp.where(hit, -jnp.inf, work), taken, thr

    st = (vals, jnp.zeros(shape, F32), jnp.full(shape, -jnp.inf, F32))
    return lax.fori_loop(0, kk, body, st)[2]


def _kth_key_val(keys, kk, axis=-1):
    def count_ge(cand):
        return jnp.sum(jnp.where(keys >= cand, 1.0, 0.0), axis=axis, keepdims=True)
    shape = list(keys.shape)
    shape[axis] = 1
    return _kth_key(count_ge, tuple(shape), float(kk))


def _gelu(x):
    return 0.5 * x * (1.0 + jnp.tanh(math.sqrt(2.0 / math.pi) * (x + 0.044715 * (x * x * x))))


def _dot_nt(a, b):
    return lax.dot_general(a, b, (((1,), (1,)), ((), ())), preferred_element_type=F32)


def _head_norm(x, g, scale):
    ms = jnp.mean(x * x, axis=-1, keepdims=True)
    return x * lax.rsqrt(ms + EPS) * g * scale


def _mm_kernel(a_ref, b_ref, o_ref):
    o_ref[...] = jnp.dot(a_ref[...], b_ref[...], preferred_element_type=F32).astype(o_ref.dtype)


def _matmul(a, b, out_dtype=F32, tm=1024, tn=512, name="matmul"):
    m, k = a.shape
    n = b.shape[1]
    tm, tn = _pick(m, tm), _pick(n, tn)
    return pl.pallas_call(
        _mm_kernel, grid=(m // tm, n // tn),
        in_specs=[pl.BlockSpec((tm, k), lambda i, j: (i, 0)), pl.BlockSpec((k, tn), lambda i, j: (0, j))],
        out_specs=pl.BlockSpec((tm, tn), lambda i, j: (i, j)),
        out_shape=jax.ShapeDtypeStruct((m, n), out_dtype),
        compiler_params=_cparams(("parallel", "arbitrary")), name=name)(a, b)


def _mm_resid_kernel(a_ref, b_ref, x_ref, gt_ref, o_ref):
    k = pl.program_id(2)
    part = jnp.dot(a_ref[...], b_ref[...], preferred_element_type=F32)

    @pl.when(k == 0)
    def _():
        o_ref[...] = part

    @pl.when(k > 0)
    def _():
        o_ref[...] += part

    @pl.when(k == pl.num_programs(2) - 1)
    def _():
        o_ref[...] = x_ref[...] + gt_ref[...] * o_ref[...]


def _matmul_resid(a, b, x, gt, name):
    m, kd = a.shape
    n = b.shape[1]
    tm, tn, tk = _pick(m, 1024), _pick(n, 1024), _pick(kd, 2048)
    return pl.pallas_call(
        _mm_resid_kernel, grid=(m // tm, n // tn, kd // tk),
        in_specs=[pl.BlockSpec((tm, tk), lambda i, j, k: (i, k)), pl.BlockSpec((tk, tn), lambda i, j, k: (k, j)),
                  pl.BlockSpec((tm, tn), lambda i, j, k: (i, j)), pl.BlockSpec((1, tn), lambda i, j, k: (0, j))],
        out_specs=pl.BlockSpec((tm, tn), lambda i, j, k: (i, j)),
        out_shape=jax.ShapeDtypeStruct((m, n), F32),
        compiler_params=_cparams(("parallel", "parallel", "arbitrary")), name=name)(a, b, x, gt)


def _ada_kernel(c_ref, w_ref, b_ref, o_ref):
    o_ref[...] = jnp.dot(c_ref[...], w_ref[...].astype(BF16), preferred_element_type=F32) + b_ref[...]


def _ada(c, w, b):
    d, n = w.shape
    tn = _pick(n, 512)
    c8 = jnp.broadcast_to(c.astype(BF16), (8, d))
    out = pl.pallas_call(
        _ada_kernel, grid=(n // tn,),
        in_specs=[pl.BlockSpec((8, d), lambda j: (0, 0)), pl.BlockSpec((d, tn), lambda j: (0, j)),
                  pl.BlockSpec((1, tn), lambda j: (0, j))],
        out_specs=pl.BlockSpec((8, tn), lambda j: (0, j)),
        out_shape=jax.ShapeDtypeStruct((8, n), F32),
        compiler_params=_cparams(("arbitrary",)), name="ada")(c8, w, b.reshape(1, n))
    return out[0:1]


def _normmod_kernel(x_ref, g_ref, sc_ref, sh_ref, o_ref):
    x = x_ref[...]
    y = x * lax.rsqrt(jnp.mean(x * x, axis=-1, keepdims=True) + EPS) * g_ref[...]
    o_ref[...] = (y * (1.0 + sc_ref[...]) + sh_ref[...]).astype(o_ref.dtype)


def _normmod(x, g, sc, sh):
    t, d = x.shape
    tm = _pick(t, 256)
    row = pl.BlockSpec((1, d), lambda i: (0, 0))
    return pl.pallas_call(
        _normmod_kernel, grid=(t // tm,),
        in_specs=[pl.BlockSpec((tm, d), lambda i: (i, 0)), row, row, row],
        out_specs=pl.BlockSpec((tm, d), lambda i: (i, 0)),
        out_shape=jax.ShapeDtypeStruct((t, d), BF16),
        compiler_params=_cparams(("parallel",)), name="normmod")(x, g.reshape(1, d), sc, sh)


def _prep_kernel(qa_ref, qi_ref, qb_ref, kvb_ref, ka_ref, va_ref, tail_ref,
                 gqa_ref, gka_ref, gqb_ref, gslc_ref, gwin_ref,
                 qa_o, qi_o, qb_o, kvb_o, ka_o, va_o, ki_o, wi_o, gs_o):
    qscale = HEAD_DIM ** -0.5 * LOG2E
    for h in range(DSA_HEADS):
        qa_o[h] = _head_norm(qa_ref[:, h * 128:(h + 1) * 128], gqa_ref[...], qscale).astype(BF16)
    for h in range(NSA_HEADS):
        qb_o[h] = _head_norm(qb_ref[:, h * 128:(h + 1) * 128], gqb_ref[...], qscale).astype(BF16)
    for h in range(IDX_HEADS):
        qi_o[h] = (qi_ref[:, h * IDX_DIM:(h + 1) * IDX_DIM] * IDX_DIM ** -0.5).astype(BF16)
    for g in range(DSA_KV_HEADS):
        ka_o[g] = _head_norm(ka_ref[:, g * 128:(g + 1) * 128], gka_ref[...], 1.0).astype(BF16)
        va_o[g] = va_ref[:, g * 128:(g + 1) * 128].astype(BF16)
    for s in range(6 * NSA_KV_HEADS):
        xs = kvb_ref[:, s * 128:(s + 1) * 128]
        part = s // NSA_KV_HEADS
        if part == 2:
            xs = _head_norm(xs, gslc_ref[...], 1.0)
        elif part == 4:
            xs = _head_norm(xs, gwin_ref[...], 1.0)
        kvb_o[s] = xs.astype(BF16)
    tail = tail_ref[...]
    ki_o[...] = tail[:, 0:IDX_DIM].astype(BF16)
    wi_o[...] = tail[:, 0:128] * IDX_HEADS ** -0.5
    gs_o[...] = jax.nn.sigmoid(tail[:, 128:256])


def _prep(pm, pt, gq_a, gk_a, gq_b, gk_slc, gk_win):
    t = pm.shape[0]
    tm = _pick(t, 128)
    g128 = pl.BlockSpec((1, 128), lambda i: (0, 0))

    def col(w, off):
        assert off % w == 0
        return pl.BlockSpec((tm, w), lambda i: (i, off // w))

    def hm(n, w):
        return pl.BlockSpec((n, tm, w), lambda i: (0, i, 0))

    row = lambda w: pl.BlockSpec((tm, w), lambda i: (i, 0))
    outs = pl.pallas_call(
        _prep_kernel, grid=(t // tm,),
        in_specs=[col(2048, OFF_QA), col(2048, OFF_QI), col(2048, OFF_QB), col(3072, OFF_KVB),
                  col(512, OFF_KA), col(512, OFF_VA), row(256), g128, g128, g128, g128, g128],
        out_specs=[hm(DSA_HEADS, 128), hm(IDX_HEADS, IDX_DIM), hm(NSA_HEADS, 128), hm(24, 128),
                   hm(DSA_KV_HEADS, 128), hm(DSA_KV_HEADS, 128), row(IDX_DIM), row(128), row(128)],
        out_shape=[jax.ShapeDtypeStruct((DSA_HEADS, t, 128), BF16),
                   jax.ShapeDtypeStruct((IDX_HEADS, t, IDX_DIM), BF16),
                   jax.ShapeDtypeStruct((NSA_HEADS, t, 128), BF16),
                   jax.ShapeDtypeStruct((24, t, 128), BF16),
                   jax.ShapeDtypeStruct((DSA_KV_HEADS, t, 128), BF16),
                   jax.ShapeDtypeStruct((DSA_KV_HEADS, t, 128), BF16),
                   jax.ShapeDtypeStruct((t, IDX_DIM), BF16),
                   jax.ShapeDtypeStruct((t, 128), F32),
                   jax.ShapeDtypeStruct((t, 128), F32)],
        compiler_params=_cparams(("parallel",)), name="prep")(
            pm, pm, pm, pm, pm, pm, pt,
            gq_a.reshape(1, 128), gk_a.reshape(1, 128), gq_b.reshape(1, 128),
            gk_slc.reshape(1, 128), gk_win.reshape(1, 128))
    return outs


def _bucket_of(dist):
    bucket = jnp.zeros(dist.shape, jnp.int32)
    for b in range(1, REL_BUCKETS):
        bucket = jnp.where(dist >= BUCKET_STARTS[b], b, bucket)
    return bucket


def _bias_row(tbl_ref, h, rows):
    trow = (tbl_ref[h:h + 1, :] - tbl_ref[h:h + 1, REL_BUCKETS - 1:REL_BUCKETS]) * LOG2E
    return jnp.broadcast_to(trow, (rows, LANE))


def _bias_tiles_kernel(tbl_ref, o_ref, *, tq, tk, nheads):
    d = pl.program_id(0)
    dist = d * tq + lax.broadcasted_iota(jnp.int32, (tq, tk), 0) - lax.broadcasted_iota(jnp.int32, (tq, tk), 1)
    bucket = _bucket_of(dist)
    for h in range(nheads):
        trow = _bias_row(tbl_ref, h, tq)
        o_ref[h, 0] = jnp.concatenate(
            [jnp.take_along_axis(trow, bucket[:, c * LANE:(c + 1) * LANE], axis=1) for c in range(tk // LANE)], axis=1)


def _n_near(tq, tk):
    return -(-(FAR_DIST + tk - 1) // tq)


def _bias_tiles(table, tq, tk):
    nheads = table.shape[1]
    nd = _n_near(tq, tk)
    return pl.pallas_call(
        functools.partial(_bias_tiles_kernel, tq=tq, tk=tk, nheads=nheads), grid=(nd,),
        in_specs=[pl.BlockSpec((nheads, LANE), lambda d: (0, 0))],
        out_specs=pl.BlockSpec((nheads, 1, tq, tk), lambda d: (0, d, 0, 0)),
        out_shape=jax.ShapeDtypeStruct((nheads, nd, tq, tk), F32),
        compiler_params=_cparams(("parallel",)), name="bias_tiles")(_table_rows(table))


def _table_rows(table):
    return jnp.pad(table.T, ((0, 0), (0, LANE - REL_BUCKETS)))


def _idx_kernel(q_ref, kt_ref, w_ref, key_ref, thr_ref, s_ref, wb_ref, *, tq, tk, ts, t, kk):
    i = pl.program_id(0)
    nkt = (i * tq + tq + tk - 1) // tk
    w = w_ref[...]
    for h in range(IDX_HEADS):
        wb_ref[h] = jnp.broadcast_to(w[:, IDX_DIM + h:IDX_DIM + h + 1], (tq, LANE))
    qall = q_ref[...].reshape(IDX_HEADS * tq, IDX_DIM)
    rows = lax.broadcasted_iota(jnp.int32, (tq, ts), 0)
    cols = lax.broadcasted_iota(jnp.int32, (tq, ts), 1)

    def tile(j, carry):
        offs = [pl.multiple_of(j * tk + u * ts, ts) for u in range(tk // ts)]
        for u, off in enumerate(offs):
            s_ref[u] = jnp.dot(qall, kt_ref[:, pl.ds(off, ts)], preferred_element_type=F32)
        for u, off in enumerate(offs):
            acc = jnp.zeros((tq, ts), F32)
            for h in range(IDX_HEADS):
                wbh = jnp.concatenate([wb_ref[h]] * (ts // LANE), axis=1)
                acc = acc + jnp.maximum(s_ref[u, h * tq:(h + 1) * tq, :], 0.0) * wbh
            acc = jnp.where(off + cols <= i * tq + rows, acc, -jnp.inf)
            key_ref[:, pl.ds(off, ts)] = _to_key(acc)
        return carry

    lax.fori_loop(0, nkt, tile, 0)

    def fill(j, carry):
        key_ref[:, pl.ds(pl.multiple_of(j * tk, tk), tk)] = jnp.full((tq, tk), KEY_NEG_INF, jnp.int32)
        return carry

    lax.fori_loop(nkt, t // tk, fill, 0)

    def count_ge(cand):
        def body(j, c):
            kt = key_ref[:, pl.ds(pl.multiple_of(j * tk, tk), tk)]
            ge = jnp.where(kt >= cand, 1.0, 0.0)
            for a in range(tk // LANE):
                c = c + ge[:, a * LANE:(a + 1) * LANE]
            return c
        c = lax.fori_loop(0, nkt, body, jnp.zeros((tq, LANE), F32))
        return jnp.sum(c, axis=-1, keepdims=True)

    def score_range(j, c):
        smin, smax = c
        st = _from_key(key_ref[:, pl.ds(pl.multiple_of(j * tk, tk), tk)])
        lo_t = jnp.where(st == -jnp.inf, jnp.inf, st)
        for a in range(tk // LANE):
            smin = jnp.minimum(smin, lo_t[:, a * LANE:(a + 1) * LANE])
            smax = jnp.maximum(smax, st[:, a * LANE:(a + 1) * LANE])
        return smin, smax

    smin, smax = lax.fori_loop(0, nkt, score_range, (jnp.full((tq, LANE), jnp.inf, F32),
                                                     jnp.full((tq, LANE), -jnp.inf, F32)))
    kmin = _to_key(jnp.min(smin, axis=-1, keepdims=True))
    kmax = _to_key(jnp.max(smax, axis=-1, keepdims=True))
    thr_ref[...] = _kth_key_bracket(count_ge, kmin, kmax, float(kk))


def _dsa_index(qi_h, ki_t, wi, kk):
    t = ki_t.shape[1]
    tq, tk = _pick(t, 128), _pick(t, 512)
    ts = tk // 2
    return pl.pallas_call(
        functools.partial(_idx_kernel, tq=tq, tk=tk, ts=ts, t=t, kk=kk), grid=(t // tq,),
        in_specs=[pl.BlockSpec((IDX_HEADS, tq, IDX_DIM), lambda i: (0, i, 0)),
                  pl.BlockSpec((IDX_DIM, t), lambda i: (0, 0)),
                  pl.BlockSpec((tq, 128), lambda i: (i, 0))],
        out_specs=[pl.BlockSpec((tq, t), lambda i: (i, 0)), pl.BlockSpec((tq, 1), lambda i: (i, 0))],
        out_shape=[jax.ShapeDtypeStruct((t, t), jnp.int32), jax.ShapeDtypeStruct((t, 1), jnp.int32)],
        scratch_shapes=[pltpu.VMEM((tk // ts, IDX_HEADS * tq, ts), F32), pltpu.VMEM((IDX_HEADS, tq, LANE), F32)],
        compiler_params=_cparams(("parallel",)), name="dsa_index")(qi_h, ki_t, wi)


def _first_tile(mode, i, tq, tk):
    if mode != "win":
        return 0
    start = i * tq - (WIN - 1)
    return (max(start, 0) if isinstance(i, int) else jnp.maximum(start, 0)) // tk


def _last_tile(i, tq, tk):
    return (i * tq + tq - 1) // tk


def _flash_kernel(qi_ref, kj_ref, *refs, mode, tq, tk, nheads, ngroups, nnear):
    refs, (m_ref, l_ref, acc_ref, s_ref, p_ref, madd_ref) = refs[:-6], refs[-6:]
    if mode == "dsa":
        q_ref, k_ref, v_ref, b_ref, key_ref, thr_ref, o_ref = refs
    elif mode == "slc":
        q_ref, k_ref, v_ref, b_ref, sel_ref, e_ref, o_ref = refs
    else:
        q_ref, k_ref, v_ref, b_ref, oc_ref, os_ref, g_ref, o_ref = refs
    rep = nheads // ngroups
    nch = tk // LANE
    rblk = min(tq, 64)
    s = pl.program_id(0)
    i, j = qi_ref[s], kj_ref[s]
    jlast = _last_tile(i, tq, tk)
    d = i - (tk // tq) * j

    @pl.when(j == _first_tile(mode, i, tq, tk))
    def _():
        m_ref[...] = jnp.full(m_ref.shape, NEG, F32)
        l_ref[...] = jnp.zeros(l_ref.shape, F32)
        acc_ref[...] = jnp.zeros(acc_ref.shape, F32)

    def step(near):
        if near:
            dist = ((i * tq - j * tk) + lax.broadcasted_iota(jnp.int32, (tq, tk), 0)
                    - lax.broadcasted_iota(jnp.int32, (tq, tk), 1))
            causal = dist >= 0
            if mode == "win":
                causal = causal & (dist < WIN)
        if mode == "dsa":
            hit = key_ref[...] >= thr_ref[...]
            madd_ref[...] = jnp.where((hit & causal) if near else hit, 0.0, NEG)
        elif mode == "win":
            madd_ref[...] = jnp.where(causal, 0.0, NEG)
        else:
            hit = jnp.dot(sel_ref[...].reshape(ngroups * tq, sel_ref.shape[2]), e_ref[...],
                          preferred_element_type=F32) > 0.5
            for g in range(ngroups):
                hg = hit[g * tq:(g + 1) * tq]
                madd_ref[g * tq:(g + 1) * tq, :] = jnp.where((hg & causal) if near else hg, 0.0, NEG)
        for g in range(ngroups):
            s_ref[...] = _dot_nt(q_ref[g * rep:(g + 1) * rep].reshape(rep * tq, HEAD_DIM), k_ref[g])
            mrow = g * tq if mode == "slc" else 0
            for r in range(rep):
                h = g * rep + r
                for rb in range(tq // rblk):
                    rows = slice(rb * rblk, (rb + 1) * rblk)
                    srows = slice(r * tq + rb * rblk, r * tq + (rb + 1) * rblk)
                    sh = s_ref[srows, :] + madd_ref[mrow + rb * rblk:mrow + (rb + 1) * rblk, :]
                    if near:
                        sh = sh + b_ref[h, 0, rows, :]
                    chunks = [sh[:, c * LANE:(c + 1) * LANE] for c in range(nch)]
                    m_old = m_ref[h, rows, :]
                    tile_max = jnp.max(functools.reduce(jnp.maximum, chunks), axis=-1, keepdims=True)
                    m_new = jnp.maximum(m_old, tile_max)
                    alpha = jnp.exp2(m_old - m_new)
                    pcs = [jnp.exp2(c - m_new) for c in chunks]
                    l_ref[h, rows, :] = alpha * l_ref[h, rows, :] + functools.reduce(jnp.add, pcs)
                    m_ref[h, rows, :] = m_new
                    acc_ref[h, rows, :] = acc_ref[h, rows, :] * alpha
                    p_ref[srows, :] = jnp.concatenate(pcs, axis=1).astype(BF16)
            pv = jnp.dot(p_ref[...], v_ref[g], preferred_element_type=F32)
            for r in range(rep):
                h = g * rep + r
                acc_ref[h] = acc_ref[h] + pv[r * tq:(r + 1) * tq]

    if mode == "win":
        step(True)
    else:
        pl.when(d < nnear)(lambda: step(True))
        pl.when(d >= nnear)(lambda: step(False))

    @pl.when(j == jlast)
    def _():
        for h in range(nheads):
            cols = slice(h * HEAD_DIM, (h + 1) * HEAD_DIM)
            out = acc_ref[h] / jnp.sum(l_ref[h], axis=-1, keepdims=True)
            if mode == "win":
                g = g_ref[...]
                out = (g[:, 3 * h:3 * h + 1] * oc_ref[:, cols] + g[:, 3 * h + 1:3 * h + 2] * os_ref[:, cols]
                       + g[:, 3 * h + 2:3 * h + 3] * out)
            o_ref[:, cols] = out.astype(o_ref.dtype)


def _flash(mode, q_h, k_h, v_h, bias, extra, out_dtype):
    nheads, t, _ = q_h.shape
    ngroups = k_h.shape[0]
    rep = nheads // ngroups
    _, nnear, tq, tk = bias.shape
    nq = t // tq
    pairs = [(i, j) for i in range(nq) for j in range(_first_tile(mode, i, tq, tk), _last_tile(i, tq, tk) + 1)]
    qi = jnp.asarray([p[0] for p in pairs], jnp.int32)
    kj = jnp.asarray([p[1] for p in pairs], jnp.int32)

    def bmap(s, qi, kj):
        return jnp.minimum(qi[s] - (tk // tq) * kj[s], nnear - 1)

    in_specs = [pl.BlockSpec((nheads, tq, HEAD_DIM), lambda s, qi, kj: (0, qi[s], 0)),
                pl.BlockSpec((ngroups, tk, HEAD_DIM), lambda s, qi, kj: (0, kj[s], 0)),
                pl.BlockSpec((ngroups, tk, HEAD_DIM), lambda s, qi, kj: (0, kj[s], 0)),
                pl.BlockSpec((nheads, 1, tq, tk), lambda s, qi, kj: (0, bmap(s, qi, kj), 0, 0))]
    if mode == "dsa":
        in_specs += [pl.BlockSpec((tq, tk), lambda s, qi, kj: (qi[s], kj[s])),
                     pl.BlockSpec((tq, 1), lambda s, qi, kj: (qi[s], 0))]
    elif mode == "slc":
        ns = extra[0].shape[2]
        in_specs += [pl.BlockSpec((ngroups, tq, ns), lambda s, qi, kj: (0, qi[s], 0)),
                     pl.BlockSpec((ns, tk), lambda s, qi, kj: (0, kj[s]))]
    else:
        row = lambda w: pl.BlockSpec((tq, w), lambda s, qi, kj: (qi[s], 0))
        in_specs += [row(nheads * HEAD_DIM), row(nheads * HEAD_DIM), row(LANE)]
    grid_spec = pltpu.PrefetchScalarGridSpec(
        num_scalar_prefetch=2, grid=(len(pairs),), in_specs=in_specs,
        out_specs=pl.BlockSpec((tq, nheads * HEAD_DIM), lambda s, qi, kj: (qi[s], 0)),
        scratch_shapes=[pltpu.VMEM((nheads, tq, LANE), F32), pltpu.VMEM((nheads, tq, LANE), F32),
                        pltpu.VMEM((nheads, tq, HEAD_DIM), F32),
                        pltpu.VMEM((rep * tq, tk), F32), pltpu.VMEM((rep * tq, tk), BF16),
                        pltpu.VMEM(((ngroups if mode == "slc" else 1) * tq, tk), F32)])
    return pl.pallas_call(
        functools.partial(_flash_kernel, mode=mode, tq=tq, tk=tk, nheads=nheads, ngroups=ngroups, nnear=nnear),
        grid_spec=grid_spec, out_shape=jax.ShapeDtypeStruct((t, nheads * HEAD_DIM), out_dtype),
        compiler_params=_cparams(("arbitrary",)), name="flash_" + mode)(qi, kj, q_h, k_h, v_h, bias, *extra)


def _compress_kernel(c_ref, w1_ref, pe_ref, w2_ref, g_ref, o_ref, *, norm):
    c = c_ref[0]
    half = CMP_STRIDE * HEAD_DIM
    ncp = c.shape[0]
    a = jnp.dot(c, w1_ref[0:half], preferred_element_type=F32)
    b = jnp.dot(c, w1_ref[half:2 * half], preferred_element_type=F32)
    pet = jnp.dot(pe_ref[...], w1_ref[...], preferred_element_type=F32)[0:1]
    hid = _gelu(a + pltpu.roll(b, ncp - 1, 0) + pet)
    y = jnp.dot(hid.astype(BF16), w2_ref[...], preferred_element_type=F32)
    if norm:
        y = _head_norm(y, g_ref[...], 1.0)
    o_ref[0] = y.astype(BF16)


def _compress(slabs, base, pe, w1, w2, gain, norm):
    _, t, _ = slabs.shape
    ncp = t // CMP_STRIDE
    width = CMP_STRIDE * HEAD_DIM
    chunks = slabs[base:base + NSA_KV_HEADS].reshape(NSA_KV_HEADS, ncp, width)
    w1f = w1.reshape(CMP_LEN * HEAD_DIM, CMP_HIDDEN).astype(BF16)
    pe8 = jnp.broadcast_to(pe.reshape(1, CMP_LEN * HEAD_DIM).astype(BF16), (8, CMP_LEN * HEAD_DIM))
    full = lambda shp: pl.BlockSpec(shp, lambda g: (0,) * len(shp))
    return pl.pallas_call(
        functools.partial(_compress_kernel, norm=norm), grid=(NSA_KV_HEADS,),
        in_specs=[pl.BlockSpec((1, ncp, width), lambda g: (g, 0, 0)),
                  full((CMP_LEN * HEAD_DIM, CMP_HIDDEN)), full((8, CMP_LEN * HEAD_DIM)),
                  full((CMP_HIDDEN, HEAD_DIM)), full((1, HEAD_DIM))],
        out_specs=pl.BlockSpec((1, ncp, HEAD_DIM), lambda g: (g, 0, 0)),
        out_shape=jax.ShapeDtypeStruct((NSA_KV_HEADS, ncp, HEAD_DIM), BF16),
        compiler_params=_cparams(("parallel",)), name="compress")(
            chunks, w1f, pe8, w2.astype(BF16), gain.reshape(1, HEAD_DIM))


def _cmp_attn_kernel(tbl_ref, q_ref, kc_ref, vc_ref, ov_ref, oc_ref, sel_ref, s_ref, key_ref,
                     *, tq, ncp, ns, nsel, ww):
    i = pl.program_id(0)
    rep = NSA_HEADS // NSA_KV_HEADS
    qpos = i * tq + lax.broadcasted_iota(jnp.int32, (tq, ncp), 0)
    dist = qpos - (lax.broadcasted_iota(jnp.int32, (tq, ncp), 1) * CMP_STRIDE + CMP_LEN - 1)
    mask = dist >= 0
    w0 = jnp.maximum(i * tq - (CMP_LEN - 1) - FAR_DIST + CMP_STRIDE, 0) // (CMP_STRIDE * LANE) * LANE
    w0 = pl.multiple_of(jnp.minimum(w0, ncp - ww), LANE)
    dist_w = (i * tq + lax.broadcasted_iota(jnp.int32, (tq, ww), 0)
              - ((w0 + lax.broadcasted_iota(jnp.int32, (tq, ww), 1)) * CMP_STRIDE + CMP_LEN - 1))
    bucket = _bucket_of(dist_w)
    blk = lax.broadcasted_iota(jnp.int32, (tq, ns), 1)
    qp = i * tq + lax.broadcasted_iota(jnp.int32, (tq, ns), 0)
    cur = qp // SLC_LEN
    forced = (blk == 0) | (blk == cur) | (blk == cur - 1)
    admissible = blk * SLC_LEN <= qp
    for g in range(NSA_KV_HEADS):
        s_ref[...] = _dot_nt(q_ref[g * rep:(g + 1) * rep].reshape(rep * tq, HEAD_DIM), kc_ref[g])
        imp = jnp.zeros((tq, ncp), F32)
        ps = []
        for r in range(rep):
            h = g * rep + r
            trow = _bias_row(tbl_ref, h, tq)
            bias = [jnp.take_along_axis(trow, bucket[:, c * LANE:(c + 1) * LANE], axis=1)
                    for c in range(ww // LANE)]
            s_ref[r * tq:(r + 1) * tq, pl.ds(w0, ww)] += jnp.concatenate(bias, axis=1)
            sh = jnp.where(mask, s_ref[r * tq:(r + 1) * tq, :], NEG)
            m = jnp.max(sh, axis=-1, keepdims=True)
            p = jnp.where(mask, jnp.exp2(sh - m), 0.0)
            pc = p / jnp.maximum(jnp.sum(p, axis=-1, keepdims=True), 1e-30)
            imp = imp + pc
            ps.append(pc.astype(BF16))
        o = jnp.dot(jnp.concatenate(ps, axis=0), vc_ref[g], preferred_element_type=F32)
        for r in range(rep):
            h = g * rep + r
            oc_ref[:, h * HEAD_DIM:(h + 1) * HEAD_DIM] = o[r * tq:(r + 1) * tq]
        hi = imp.astype(BF16)
        lo = (imp - hi.astype(F32)).astype(BF16)
        impb = (jnp.dot(hi, ov_ref[...], preferred_element_type=F32)
                + jnp.dot(lo, ov_ref[...], preferred_element_type=F32))
        impb = jnp.where(forced, FORCE_SCORE, impb)
        key_ref[:, g * tq:(g + 1) * tq] = _to_key(jnp.where(admissible, impb, -jnp.inf).T)
    keys = key_ref[...]
    sel_t = jnp.where(keys >= _kth_key_val(keys, nsel, axis=0), 1.0, 0.0)
    for g in range(NSA_KV_HEADS):
        sel_ref[g] = sel_t[:, g * tq:(g + 1) * tq].T.astype(BF16)


def _cmp_attn(table, q_h, kc, vc):
    nheads, t, _ = q_h.shape
    ncp = kc.shape[1]
    ns = t // SLC_LEN
    nsel = min(SLC_TOP, ns)
    tq = _pick(t, 128)
    ww = min(ncp, LANE * (-(-(FAR_DIST + tq + CMP_STRIDE * LANE) // (CMP_STRIDE * LANE))))
    rep = nheads // NSA_KV_HEADS
    c_start = np.arange(ncp)[:, None] * CMP_STRIDE
    s_start = np.arange(ns)[None, :] * SLC_LEN
    overlap = jnp.asarray((c_start < s_start + SLC_LEN) & (c_start + CMP_LEN > s_start), BF16)
    full = lambda shp: pl.BlockSpec(shp, lambda i: (0,) * len(shp))
    return pl.pallas_call(
        functools.partial(_cmp_attn_kernel, tq=tq, ncp=ncp, ns=ns, nsel=nsel, ww=ww), grid=(t // tq,),
        in_specs=[full((nheads, LANE)),
                  pl.BlockSpec((nheads, tq, HEAD_DIM), lambda i: (0, i, 0)),
                  full((NSA_KV_HEADS, ncp, HEAD_DIM)), full((NSA_KV_HEADS, ncp, HEAD_DIM)), full((ncp, ns))],
        out_specs=[pl.BlockSpec((tq, nheads * HEAD_DIM), lambda i: (i, 0)),
                   pl.BlockSpec((NSA_KV_HEADS, tq, ns), lambda i: (0, i, 0))],
        out_shape=[jax.ShapeDtypeStruct((t, nheads * HEAD_DIM), F32),
                   jax.ShapeDtypeStruct((NSA_KV_HEADS, t, ns), BF16)],
        scratch_shapes=[pltpu.VMEM((rep * tq, ncp), F32), pltpu.VMEM((ns, NSA_KV_HEADS * tq), jnp.int32)],
        compiler_params=_cparams(("parallel",)), name="cmp_attn")(_table_rows(table), q_h, kc, vc, overlap)


def _merge_kernel(ya_ref, yb_ref, wa_ref, wb_ref, ga_ref, gb_ref, o_ref):
    za = jnp.dot(ya_ref[...], wa_ref[...], preferred_element_type=F32)
    zb = jnp.dot(yb_ref[...], wb_ref[...], preferred_element_type=F32)
    o_ref[...] = (jax.nn.sigmoid(ga_ref[...]) * za + jax.nn.sigmoid(gb_ref[...]) * zb).astype(o_ref.dtype)


def _merge(y_a, y_b, w_a, w_b, pm):
    t, k = y_a.shape
    d = w_a.shape[1]
    tm, tn = _pick(t, 1024), _pick(d, 512)
    assert OFF_GM % tn == 0
    lhs = pl.BlockSpec((tm, k), lambda i, j: (i, 0))
    rhs = pl.BlockSpec((k, tn), lambda i, j: (0, j))
    return pl.pallas_call(
        _merge_kernel, grid=(t // tm, d // tn),
        in_specs=[lhs, lhs, rhs, rhs, pl.BlockSpec((tm, tn), lambda i, j: (i, OFF_GM // tn + j)),
                  pl.BlockSpec((tm, tn), lambda i, j: (i, (OFF_GM + d) // tn + j))],
        out_specs=pl.BlockSpec((tm, tn), lambda i, j: (i, j)), out_shape=jax.ShapeDtypeStruct((t, d), BF16),
        compiler_params=_cparams(("parallel", "arbitrary")), name="merge")(y_a, y_b, w_a, w_b, pm, pm)


_CAND_FULL = PEER_TOPK // 2


def _peer_cands(a1, a2, op):
    pieces = [op(a1[0:1], a2)]
    pieces += [op(a1[k:k + 1], a2[0:_CAND_FULL]) for k in range(1, _CAND_FULL)]
    pieces.append(op(a1[_CAND_FULL:], a2[0:1]))
    return jnp.concatenate(pieces, axis=0)


def _peer_score_kernel(q_ref, sk_ref, s1_o, e1_o, s2_o, e2_o, thr_o, *, tm):
    row = lax.broadcasted_iota(jnp.int32, (PEER_NKEYS, tm), 0).astype(F32)
    for h in range(PEER_HEADS):
        svals, tops = [], []
        for c in range(2):
            hc = 2 * h + c
            s = _dot_nt(sk_ref[hc], q_ref[:, hc * 128:(hc + 1) * 128].astype(BF16))
            svals.append(s)
            work, top = s, []
            for _ in range(PEER_TOPK):
                mx = jnp.max(work, axis=0, keepdims=True)
                first = jnp.min(jnp.where(work == mx, row, float(PEER_NKEYS)), axis=0, keepdims=True)
                work = jnp.where(row == first, -jnp.inf, work)
                top.append(mx)
            tops.append(jnp.concatenate(top, axis=0))
        a1, a2 = tops
        cand = _peer_cands(a1, a2, jnp.add)
        thr = _kth_largest(cand, PEER_TOPK, 0)
        m1, m2 = a1[0:1], a2[0:1]
        ec = _peer_cands(jnp.exp(a1 - m1), jnp.exp(a2 - m2), jnp.multiply)
        z = jnp.sum(jnp.where(cand >= thr, ec, 0.0), axis=0, keepdims=True)
        s1_o[h] = svals[0]
        s2_o[h] = svals[1]
        e1_o[h] = jnp.exp(svals[0] - m1)
        e2_o[h] = jnp.exp(svals[1] - m2) / z
        thr_o[h:h + 1, :] = thr


def _peer_score(qp, sub_keys):
    t = qp.shape[0]
    tm = _pick(t, 256)
    sk = sub_keys.reshape(2 * PEER_HEADS, PEER_NKEYS, PEER_QDIM // 2).astype(BF16)
    tr = pl.BlockSpec((PEER_HEADS, PEER_NKEYS, tm), lambda i: (0, 0, i))
    shp = jax.ShapeDtypeStruct((PEER_HEADS, PEER_NKEYS, t), F32)
    return pl.pallas_call(
        functools.partial(_peer_score_kernel, tm=tm), grid=(t // tm,),
        in_specs=[pl.BlockSpec((tm, 2 * PEER_HEADS * 128), lambda i: (i, 0)),
                  pl.BlockSpec((2 * PEER_HEADS, PEER_NKEYS, PEER_QDIM // 2), lambda i: (0, 0, 0))],
        out_specs=[tr, tr, tr, tr, pl.BlockSpec((PEER_HEADS, tm), lambda i: (0, i))],
        out_shape=[shp, shp, shp, shp, jax.ShapeDtypeStruct((PEER_HEADS, t), F32)],
        compiler_params=_cparams(("parallel",)), name="peer_score")(qp, sk)


def _peer_act_kernel(h_ref, u_ref, s1_ref, e1_ref, s2_ref, e2_ref, thr_ref, o_ref, w_ref, a_ref, *, tm, te, sub):
    j = pl.program_id(1)
    n1 = te // PEER_NKEYS
    a_ref[...] = _dot_nt(h_ref[...], u_ref[...])
    for ai in range(n1):
        i1 = j * n1 + ai
        s1rows = [s1_ref[h, pl.ds(i1, 1), :] for h in range(PEER_HEADS)]
        e1rows = [e1_ref[h, pl.ds(i1, 1), :] for h in range(PEER_HEADS)]
        for ts in range(tm // LANE):
            tok = slice(ts * LANE, (ts + 1) * LANE)
            wt = jnp.zeros((PEER_NKEYS, LANE), F32)
            for h in range(PEER_HEADS):
                hit = (s1rows[h][:, tok] + s2_ref[h, :, tok]) >= thr_ref[h:h + 1, tok]
                wt = wt + jnp.where(hit, e1rows[h][:, tok] * e2_ref[h, :, tok], 0.0)
            w_ref[tok, ai * PEER_NKEYS:(ai + 1) * PEER_NKEYS] = wt.T
    for r in range(tm // sub):
        rows = slice(r * sub, (r + 1) * sub)
        o_ref[rows, :] = (_gelu(a_ref[rows, :]) * w_ref[rows, :]).astype(o_ref.dtype)


def _peer_act(h2, u, s1, e1, s2, e2, thr):
    t, d = h2.shape
    ne = u.shape[0]
    tm, te, sub = _pick(t, 1024), 512, 256
    once = dict(pipeline_mode=pl.Buffered(1))
    tok = pl.BlockSpec((PEER_HEADS, PEER_NKEYS, tm), lambda i, j: (0, 0, i), **once)
    return pl.pallas_call(
        functools.partial(_peer_act_kernel, tm=tm, te=te, sub=sub), grid=(t // tm, ne // te),
        in_specs=[pl.BlockSpec((tm, d), lambda i, j: (i, 0), **once),
                  pl.BlockSpec((te, d), lambda i, j: (j, 0)),
                  tok, tok, tok, tok, pl.BlockSpec((PEER_HEADS, tm), lambda i, j: (0, i), **once)],
        out_specs=pl.BlockSpec((tm, te), lambda i, j: (i, j)),
        out_shape=jax.ShapeDtypeStruct((t, ne), BF16),
        scratch_shapes=[pltpu.VMEM((tm, te), F32), pltpu.VMEM((tm, te), F32)],
        compiler_params=_cparams(("parallel", "arbitrary")), name="peer_act")(h2, u, s1, e1, s2, e2, thr)


def _pack_w_in(w_in, d):
    offs = np.cumsum([0, DSA_HEADS * 128, DSA_KV_HEADS * 128, DSA_KV_HEADS * 128, IDX_HEADS * IDX_DIM, IDX_DIM,
                      IDX_HEADS, NSA_HEADS * 128, 6 * NSA_KV_HEADS * 128, 3 * NSA_HEADS, 2 * d])
    qa, ka, va, qi, ki, wi, qb, kvb, gb, gm = [w_in[:, int(offs[n]):int(offs[n + 1])] for n in range(10)]
    main = jnp.concatenate([qa, qi, qb, kvb, ka, va, gm], axis=1).astype(BF16)
    pad = lambda n: jnp.zeros((d, n), w_in.dtype)
    tail = jnp.concatenate([ki, wi, pad(128 - IDX_DIM - IDX_HEADS), gb, pad(128 - 3 * NSA_HEADS)], axis=1).astype(BF16)
    return main, tail


def _token_mixer(h, table, w_in, gq_a, gk_a, gq_b, gk_cmp, gk_slc, gk_win, cmp_pe_k, cmp_w1_k, cmp_w2_k,
                 cmp_pe_v, cmp_w1_v, cmp_w2_v, w_branch_a, w_branch_b):
    t, d = h.shape
    w_main, w_tail = _pack_w_in(w_in, d)
    pm = _matmul(h, w_main, name="proj_main")
    pt = _matmul(h, w_tail, name="proj_tail")
    qa_h, qi_h, qb_h, kvb_h, ka_h, va_h, ki, wi, gsig = _prep(pm, pt, gq_a, gk_a, gq_b, gk_slc, gk_win)
    tq, tk = _pick(t, 256), _pick(t, 512)
    keys, thr = _dsa_index(qi_h, ki.T, wi, min(DSA_TOPK, t // 4))
    bias_a = _bias_tiles(table[:, :DSA_HEADS], tq, tk)
    y_a = _flash("dsa", qa_h, ka_h, va_h, bias_a, (keys, thr), BF16)
    g = NSA_KV_HEADS
    kc = _compress(kvb_h, 0, cmp_pe_k, cmp_w1_k, cmp_w2_k, gk_cmp, True)
    vc = _compress(kvb_h, g, cmp_pe_v, cmp_w1_v, cmp_w2_v, gk_cmp, False)
    o_c, sel = _cmp_attn(table[:, DSA_HEADS:], qb_h, kc, vc)
    bias_b = _bias_tiles(table[:, DSA_HEADS:], tq, tk)
    ns = t // SLC_LEN
    expand = jnp.asarray(np.arange(ns)[:, None] == (np.arange(t)[None, :] // SLC_LEN), BF16)
    o_s = _flash("slc", qb_h, kvb_h[2 * g:3 * g], kvb_h[3 * g:4 * g], bias_b, (sel, expand), F32)
    y_b = _flash("win", qb_h, kvb_h[4 * g:5 * g], kvb_h[5 * g:6 * g], bias_b, (o_c, o_s, gsig), BF16)
    return _merge(y_a, y_b, w_branch_a.astype(BF16), w_branch_b.astype(BF16), pm)


def _peer_ffn(h2, x1, gt2, w_q, sub_keys, u, v):
    qp = _matmul(h2, w_q.astype(BF16), name="peer_q")
    s1, e1, s2, e2, thr = _peer_score(qp, sub_keys)
    act = _peer_act(h2, u.astype(BF16), s1, e1, s2, e2, thr)
    return _matmul_resid(act, v.astype(BF16), x1, gt2, name="peer_out")


def kernel(x, c, rel_bias, w_ada, b_ada, g_mix, w_in, gq_a, gk_a, gq_b, gk_cmp, gk_slc, gk_win, cmp_pe_k, cmp_w1_k,
           cmp_w2_k, cmp_pe_v, cmp_w1_v, cmp_w2_v, w_branch_a, w_branch_b, w_out, g_ffn, w_peer_q, peer_sub_keys,
           peer_u, peer_v):
    bsz, t, d = x.shape
    assert bsz == 1 and t % 512 == 0 and d % 128 == 0
    xs = x[0]
    for i in range(w_ada.shape[0]):
        mod = _ada(c, w_ada[i], b_ada[i])
        sh1, sc1, gt1, sh2, sc2, gt2 = [mod[:, n * d:(n + 1) * d] for n in range(6)]
        h = _normmod(xs, g_mix[i], sc1, sh1)
        merged = _token_mixer(h, rel_bias, w_in[i], gq_a[i], gk_a[i], gq_b[i], gk_cmp[i], gk_slc[i], gk_win[i],
                              cmp_pe_k[i], cmp_w1_k[i], cmp_w2_k[i], cmp_pe_v[i], cmp_w1_v[i], cmp_w2_v[i],
                              w_branch_a[i], w_branch_b[i])
        x1 = _matmul_resid(merged, w_out[i].astype(BF16), xs, gt1, name="w_out")
        h2 = _normmod(x1, g_ffn[i], sc2, sh2)
        xs = _peer_ffn(h2, x1, gt2, w_peer_q[i], peer_sub_keys[i], peer_u[i], peer_v[i])
    return xs[None]
```

```python
import functools
import math

import numpy as np
import jax
import jax.numpy as jnp
from jax import lax
from jax.experimental import pallas as pl
from jax.experimental.pallas import tpu as pltpu

HEAD_DIM = 128
DSA_HEADS = 16
DSA_KV_HEADS = 4
IDX_HEADS = 32
IDX_DIM = 64
DSA_TOPK = 256
NSA_HEADS = 16
NSA_KV_HEADS = 4
CMP_LEN = 32
CMP_STRIDE = 16
CMP_HIDDEN = 256
SLC_LEN = 64
SLC_TOP = 16
WIN = 512
FORCE_SCORE = 1e9
PEER_HEADS = 8
PEER_NKEYS = 128
PEER_QDIM = 256
PEER_TOPK = 16
REL_BUCKETS = 32
REL_MAX_DIST = 2048
EPS = 1e-6

LANE = 128
VMEM_LIMIT = 56 * 1024 * 1024
NEG = -1e30
LOG2E = math.log2(math.e)
INT_MIN = -2 ** 31
KEY_NEG_INF = -2 ** 31 + 0x007FFFFF

OFF_QA, OFF_QI, OFF_QB, OFF_KVB, OFF_KA, OFF_VA, OFF_GM = 0, 2048, 4096, 6144, 9216, 9728, 10240

F32 = jnp.float32
BF16 = jnp.bfloat16


def _bucket_starts():
    n = np.arange(2 * REL_MAX_DIST)
    exact = REL_BUCKETS // 2
    nf = np.maximum(n, 1).astype(np.float32)
    lb = exact + (np.log(nf / np.float32(exact)) / np.float32(math.log(REL_MAX_DIST / exact))
                  * np.float32(REL_BUCKETS - exact)).astype(np.int32)
    bucket = np.where(n < exact, n, np.minimum(lb, REL_BUCKETS - 1))
    return [int(np.argmax(bucket >= b)) for b in range(REL_BUCKETS)]


BUCKET_STARTS = _bucket_starts()
FAR_DIST = BUCKET_STARTS[REL_BUCKETS - 1]


def _cparams(sem):
    return pltpu.CompilerParams(dimension_semantics=sem, vmem_limit_bytes=VMEM_LIMIT)


def _pick(n, pref):
    t = pref
    while n % t:
        t //= 2
    return t


def _to_key(x):
    b = lax.bitcast_convert_type(x, jnp.int32)
    return b ^ ((b >> 31) & 0x7FFFFFFF)


def _from_key(k):
    return lax.bitcast_convert_type(k ^ ((k >> 31) & 0x7FFFFFFF), F32)


def _kth_key(count_ge, shape, kk):
    def bit_pass(b, x):
        cand = x + lax.shift_left(jnp.int32(1), 31 - b)
        return jnp.where(count_ge(cand) >= kk, cand, x)
    x = lax.fori_loop(0, 32, bit_pass, jnp.full(shape, INT_MIN, jnp.int32))
    return jnp.maximum(x, KEY_NEG_INF)


def _kth_key_bracket(count_ge, kmin, kmax, kk):
    c_min = count_ge(kmin)
    few = c_min < kk
    lo0 = jnp.where(few, KEY_NEG_INF, kmin)
    hi0 = kmax + 1
    done0 = jnp.where(few | (c_min == kk) | (hi0 - 1 <= lo0), 1.0, 0.0)

    def cond(st):
        p, _, _, _, _, done = st
        return jnp.logical_and(p < 72, jnp.min(done) < 0.5)

    def body(st):
        p, lo, hi, clo, chi, done = st
        mid = (lo >> 1) + (hi >> 1) + (lo & hi & 1)
        flo, fhi = _from_key(lo), _from_key(hi)
        frac = (jnp.log(clo) - math.log(kk)) / (jnp.log(clo) - jnp.log(jnp.maximum(chi, 0.5)))
        cand = jnp.where(p % 2 == 1, mid, _to_key(flo + frac * (fhi - flo)))
        cand = jnp.minimum(jnp.maximum(cand, lo + 1), hi - 1)
        active = done < 0.5
        cand = jnp.where(active, cand, lo)
        cnt = count_ge(cand)
        up = active & (cnt >= kk)
        dn = active & (cnt < kk)
        lo, clo = jnp.where(up, cand, lo), jnp.where(up, cnt, clo)
        hi, chi = jnp.where(dn, cand, hi), jnp.where(dn, cnt, chi)
        done = jnp.where((clo == kk) | (hi - 1 <= lo), 1.0, done)
        return p + 1, lo, hi, clo, chi, done

    st = lax.while_loop(cond, body, (jnp.int32(0), lo0, hi0, c_min, jnp.zeros_like(c_min), done0))
    return st[1], st[3]


def _kth_largest(vals, kk, axis):
    shape = list(vals.shape)
    shape[axis] = 1

    def body(_, st):
        work, taken, thr = st
        mx = jnp.max(work, axis=axis, keepdims=True)
        hit = work == mx
        thr = jnp.where(taken < kk, mx, thr)
        taken = taken + jnp.sum(jnp.where(hit, 1.0, 0.0), axis=axis, keepdims=True)
        return jnp.where(hit, -jnp.inf, work), taken, thr

    st = (vals, jnp.zeros(shape, F32), jnp.full(shape, -jnp.inf, F32))
    return lax.fori_loop(0, kk, body, st)[2]


def _kth_key_val(keys, kk, axis=-1):
    def count_ge(cand):
        return jnp.sum(jnp.where(keys >= cand, 1.0, 0.0), axis=axis, keepdims=True)
    shape = list(keys.shape)
    shape[axis] = 1
    return _kth_key(count_ge, tuple(shape), float(kk))


def _gelu(x):
    return 0.5 * x * (1.0 + jnp.tanh(math.sqrt(2.0 / math.pi) * (x + 0.044715 * (x * x * x))))


def _dot_nt(a, b):
    return lax.dot_general(a, b, (((1,), (1,)), ((), ())), preferred_element_type=F32)


def _head_norm(x, g, scale):
    ms = jnp.mean(x * x, axis=-1, keepdims=True)
    return x * lax.rsqrt(ms + EPS) * g * scale


def _mm_kernel(a_ref, b_ref, o_ref):
    o_ref[...] = jnp.dot(a_ref[...], b_ref[...], preferred_element_type=F32).astype(o_ref.dtype)


def _matmul(a, b, out_dtype=F32, tm=1024, tn=512, name="matmul"):
    m, k = a.shape
    n = b.shape[1]
    tm, tn = _pick(m, tm), _pick(n, tn)
    return pl.pallas_call(
        _mm_kernel, grid=(m // tm, n // tn),
        in_specs=[pl.BlockSpec((tm, k), lambda i, j: (i, 0)), pl.BlockSpec((k, tn), lambda i, j: (0, j))],
        out_specs=pl.BlockSpec((tm, tn), lambda i, j: (i, j)),
        out_shape=jax.ShapeDtypeStruct((m, n), out_dtype),
        compiler_params=_cparams(("parallel", "arbitrary")), name=name)(a, b)


def _mm_resid_kernel(a_ref, b_ref, x_ref, gt_ref, o_ref):
    k = pl.program_id(2)
    part = jnp.dot(a_ref[...], b_ref[...], preferred_element_type=F32)

    @pl.when(k == 0)
    def _():
        o_ref[...] = part

    @pl.when(k > 0)
    def _():
        o_ref[...] += part

    @pl.when(k == pl.num_programs(2) - 1)
    def _():
        o_ref[...] = x_ref[...] + gt_ref[...] * o_ref[...]


def _matmul_resid(a, b, x, gt, name):
    m, kd = a.shape
    n = b.shape[1]
    tm, tn, tk = _pick(m, 1024), _pick(n, 1024), _pick(kd, 2048)
    return pl.pallas_call(
        _mm_resid_kernel, grid=(m // tm, n // tn, kd // tk),
        in_specs=[pl.BlockSpec((tm, tk), lambda i, j, k: (i, k)), pl.BlockSpec((tk, tn), lambda i, j, k: (k, j)),
                  pl.BlockSpec((tm, tn), lambda i, j, k: (i, j)), pl.BlockSpec((1, tn), lambda i, j, k: (0, j))],
        out_specs=pl.BlockSpec((tm, tn), lambda i, j, k: (i, j)),
        out_shape=jax.ShapeDtypeStruct((m, n), F32),
        compiler_params=_cparams(("parallel", "parallel", "arbitrary")), name=name)(a, b, x, gt)


def _ada_kernel(c_ref, w_ref, b_ref, o_ref):
    o_ref[...] = jnp.dot(c_ref[...], w_ref[...].astype(BF16), preferred_element_type=F32) + b_ref[...]


def _ada(c, w, b):
    d, n = w.shape
    tn = _pick(n, 512)
    c8 = jnp.broadcast_to(c.astype(BF16), (8, d))
    out = pl.pallas_call(
        _ada_kernel, grid=(n // tn,),
        in_specs=[pl.BlockSpec((8, d), lambda j: (0, 0)), pl.BlockSpec((d, tn), lambda j: (0, j)),
                  pl.BlockSpec((1, tn), lambda j: (0, j))],
        out_specs=pl.BlockSpec((8, tn), lambda j: (0, j)),
        out_shape=jax.ShapeDtypeStruct((8, n), F32),
        compiler_params=_cparams(("arbitrary",)), name="ada")(c8, w, b.reshape(1, n))
    return out[0:1]


def _normmod_kernel(x_ref, g_ref, sc_ref, sh_ref, o_ref):
    x = x_ref[...]
    y = x * lax.rsqrt(jnp.mean(x * x, axis=-1, keepdims=True) + EPS) * g_ref[...]
    o_ref[...] = (y * (1.0 + sc_ref[...]) + sh_ref[...]).astype(o_ref.dtype)


def _normmod(x, g, sc, sh):
    t, d = x.shape
    tm = _pick(t, 256)
    row = pl.BlockSpec((1, d), lambda i: (0, 0))
    return pl.pallas_call(
        _normmod_kernel, grid=(t // tm,),
        in_specs=[pl.BlockSpec((tm, d), lambda i: (i, 0)), row, row, row],
        out_specs=pl.BlockSpec((tm, d), lambda i: (i, 0)),
        out_shape=jax.ShapeDtypeStruct((t, d), BF16),
        compiler_params=_cparams(("parallel",)), name="normmod")(x, g.reshape(1, d), sc, sh)


def _prep_kernel(qa_ref, qi_ref, qb_ref, kvb_ref, ka_ref, va_ref, tail_ref,
                 gqa_ref, gka_ref, gqb_ref, gslc_ref, gwin_ref,
                 qa_o, qi_o, qb_o, kvb_o, ka_o, va_o, ki_o, wi_o, gs_o):
    qscale = HEAD_DIM ** -0.5 * LOG2E
    for h in range(DSA_HEADS):
        qa_o[h] = _head_norm(qa_ref[:, h * 128:(h + 1) * 128], gqa_ref[...], qscale).astype(BF16)
    for h in range(NSA_HEADS):
        qb_o[h] = _head_norm(qb_ref[:, h * 128:(h + 1) * 128], gqb_ref[...], qscale).astype(BF16)
    for h in range(IDX_HEADS):
        qi_o[h] = (qi_ref[:, h * IDX_DIM:(h + 1) * IDX_DIM] * IDX_DIM ** -0.5).astype(BF16)
    for g in range(DSA_KV_HEADS):
        ka_o[g] = _head_norm(ka_ref[:, g * 128:(g + 1) * 128], gka_ref[...], 1.0).astype(BF16)
        va_o[g] = va_ref[:, g * 128:(g + 1) * 128].astype(BF16)
    for s in range(6 * NSA_KV_HEADS):
        xs = kvb_ref[:, s * 128:(s + 1) * 128]
        part = s // NSA_KV_HEADS
        if part == 2:
            xs = _head_norm(xs, gslc_ref[...], 1.0)
        elif part == 4:
            xs = _head_norm(xs, gwin_ref[...], 1.0)
        kvb_o[s] = xs.astype(BF16)
    tail = tail_ref[...]
    ki_o[...] = tail[:, 0:IDX_DIM].astype(BF16)
    wi_o[...] = tail[:, 0:128] * IDX_HEADS ** -0.5
    gs_o[...] = jax.nn.sigmoid(tail[:, 128:256])


def _prep(pm, pt, gq_a, gk_a, gq_b, gk_slc, gk_win):
    t = pm.shape[0]
    tm = _pick(t, 128)
    g128 = pl.BlockSpec((1, 128), lambda i: (0, 0))

    def col(w, off):
        assert off % w == 0
        return pl.BlockSpec((tm, w), lambda i: (i, off // w))

    def hm(n, w):
        return pl.BlockSpec((n, tm, w), lambda i: (0, i, 0))

    row = lambda w: pl.BlockSpec((tm, w), lambda i: (i, 0))
    outs = pl.pallas_call(
        _prep_kernel, grid=(t // tm,),
        in_specs=[col(2048, OFF_QA), col(2048, OFF_QI), col(2048, OFF_QB), col(3072, OFF_KVB),
                  col(512, OFF_KA), col(512, OFF_VA), row(256), g128, g128, g128, g128, g128],
        out_specs=[hm(DSA_HEADS, 128), hm(IDX_HEADS, IDX_DIM), hm(NSA_HEADS, 128), hm(24, 128),
                   hm(DSA_KV_HEADS, 128), hm(DSA_KV_HEADS, 128), row(IDX_DIM), row(128), row(128)],
        out_shape=[jax.ShapeDtypeStruct((DSA_HEADS, t, 128), BF16),
                   jax.ShapeDtypeStruct((IDX_HEADS, t, IDX_DIM), BF16),
                   jax.ShapeDtypeStruct((NSA_HEADS, t, 128), BF16),
                   jax.ShapeDtypeStruct((24, t, 128), BF16),
                   jax.ShapeDtypeStruct((DSA_KV_HEADS, t, 128), BF16),
                   jax.ShapeDtypeStruct((DSA_KV_HEADS, t, 128), BF16),
                   jax.ShapeDtypeStruct((t, IDX_DIM), BF16),
                   jax.ShapeDtypeStruct((t, 128), F32),
                   jax.ShapeDtypeStruct((t, 128), F32)],
        compiler_params=_cparams(("parallel",)), name="prep")(
            pm, pm, pm, pm, pm, pm, pt,
            gq_a.reshape(1, 128), gk_a.reshape(1, 128), gq_b.reshape(1, 128),
            gk_slc.reshape(1, 128), gk_win.reshape(1, 128))
    return outs


def _bucket_of(dist):
    bucket = jnp.zeros(dist.shape, jnp.int32)
    for b in range(1, REL_BUCKETS):
        bucket = jnp.where(dist >= BUCKET_STARTS[b], b, bucket)
    return bucket


def _bias_row(tbl_ref, h, rows):
    trow = (tbl_ref[h:h + 1, :] - tbl_ref[h:h + 1, REL_BUCKETS - 1:REL_BUCKETS]) * LOG2E
    return jnp.broadcast_to(trow, (rows, LANE))


def _bias_tiles_kernel(tbl_ref, o_ref, *, tq, tk, nheads):
    d = pl.program_id(0)
    dist = d * tq + lax.broadcasted_iota(jnp.int32, (tq, tk), 0) - lax.broadcasted_iota(jnp.int32, (tq, tk), 1)
    bucket = _bucket_of(dist)
    for h in range(nheads):
        trow = _bias_row(tbl_ref, h, tq)
        o_ref[h, 0] = jnp.concatenate(
            [jnp.take_along_axis(trow, bucket[:, c * LANE:(c + 1) * LANE], axis=1) for c in range(tk // LANE)], axis=1)


def _n_near(tq, tk):
    return -(-(FAR_DIST + tk - 1) // tq)


def _bias_tiles(table, tq, tk):
    nheads = table.shape[1]
    nd = _n_near(tq, tk)
    return pl.pallas_call(
        functools.partial(_bias_tiles_kernel, tq=tq, tk=tk, nheads=nheads), grid=(nd,),
        in_specs=[pl.BlockSpec((nheads, LANE), lambda d: (0, 0))],
        out_specs=pl.BlockSpec((nheads, 1, tq, tk), lambda d: (0, d, 0, 0)),
        out_shape=jax.ShapeDtypeStruct((nheads, nd, tq, tk), F32),
        compiler_params=_cparams(("parallel",)), name="bias_tiles")(_table_rows(table))


def _table_rows(table):
    return jnp.pad(table.T, ((0, 0), (0, LANE - REL_BUCKETS)))


def _idx_kernel(q_ref, kt_ref, w_ref, key_ref, thr_ref, s_ref, wb_ref, *, tq, tk, ts, t, kk):
    i = pl.program_id(0)
    nkt = (i * tq + tq + tk - 1) // tk
    w = w_ref[...]
    for h in range(IDX_HEADS):
        wb_ref[h] = jnp.broadcast_to(w[:, IDX_DIM + h:IDX_DIM + h + 1], (tq, LANE))
    qall = q_ref[...].reshape(IDX_HEADS * tq, IDX_DIM)
    rows = lax.broadcasted_iota(jnp.int32, (tq, ts), 0)
    cols = lax.broadcasted_iota(jnp.int32, (tq, ts), 1)

    def tile(j, carry):
        offs = [pl.multiple_of(j * tk + u * ts, ts) for u in range(tk // ts)]
        for u, off in enumerate(offs):
            s_ref[u] = jnp.dot(qall, kt_ref[:, pl.ds(off, ts)], preferred_element_type=F32)
        for u, off in enumerate(offs):
            acc = jnp.zeros((tq, ts), F32)
            for h in range(IDX_HEADS):
                wbh = jnp.concatenate([wb_ref[h]] * (ts // LANE), axis=1)
                acc = acc + jnp.maximum(s_ref[u, h * tq:(h + 1) * tq, :], 0.0) * wbh
            acc = jnp.where(off + cols <= i * tq + rows, acc, -jnp.inf)
            key_ref[:, pl.ds(off, ts)] = _to_key(acc)
        return carry

    lax.fori_loop(0, nkt, tile, 0)

    def fill(j, carry):
        key_ref[:, pl.ds(pl.multiple_of(j * tk, tk), tk)] = jnp.full((tq, tk), KEY_NEG_INF, jnp.int32)
        return carry

    lax.fori_loop(nkt, t // tk, fill, 0)

    def count_ge(cand):
        def body(j, c):
            kt = key_ref[:, pl.ds(pl.multiple_of(j * tk, tk), tk)]
            ge = jnp.where(kt >= cand, 1.0, 0.0)
            for a in range(tk // LANE):
                c = c + ge[:, a * LANE:(a + 1) * LANE]
            return c
        c = lax.fori_loop(0, nkt, body, jnp.zeros((tq, LANE), F32))
        return jnp.sum(c, axis=-1, keepdims=True)

    def score_range(j, c):
        smin, smax = c
        st = _from_key(key_ref[:, pl.ds(pl.multiple_of(j * tk, tk), tk)])
        lo_t = jnp.where(st == -jnp.inf, jnp.inf, st)
        for a in range(tk // LANE):
            smin = jnp.minimum(smin, lo_t[:, a * LANE:(a + 1) * LANE])
            smax = jnp.maximum(smax, st[:, a * LANE:(a + 1) * LANE])
        return smin, smax

    smin, smax = lax.fori_loop(0, nkt, score_range, (jnp.full((tq, LANE), jnp.inf, F32),
                                                     jnp.full((tq, LANE), -jnp.inf, F32)))
    kmin = _to_key(jnp.min(smin, axis=-1, keepdims=True))
    kmax = _to_key(jnp.max(smax, axis=-1, keepdims=True))
    thr, reach = _kth_key_bracket(count_ge, kmin, kmax, float(kk))
    thr_ref[...] = thr
    tied = reach > kk

    @pl.when(jnp.max(jnp.where(tied, 1.0, 0.0)) > 0.5)
    def _():
        need = kk - count_ge(thr + 1)

        def count_tied_before(cut):
            def body(j, c):
                off = pl.multiple_of(j * tk, tk)
                kt = key_ref[:, pl.ds(off, tk)]
                pos = off + lax.broadcasted_iota(jnp.int32, (tq, tk), 1)
                m = jnp.where(kt == thr, jnp.where(pos < cut, 1.0, 0.0), 0.0)
                for a in range(tk // LANE):
                    c = c + m[:, a * LANE:(a + 1) * LANE]
                return c
            c = lax.fori_loop(0, nkt, body, jnp.zeros((tq, LANE), F32))
            return jnp.sum(c, axis=-1, keepdims=True)

        def halve(_, st):
            lo_c, hi_c = st
            mid = (lo_c + hi_c) >> 1
            enough = count_tied_before(mid) >= need
            return jnp.where(enough, lo_c, mid), jnp.where(enough, mid, hi_c)

        steps = max(t, 2).bit_length()
        _, cut = lax.fori_loop(0, steps, halve, (jnp.zeros((tq, 1), jnp.int32), jnp.full((tq, 1), t, jnp.int32)))

        def demote(j, carry):
            off = pl.multiple_of(j * tk, tk)
            kt = key_ref[:, pl.ds(off, tk)]
            pos = off + lax.broadcasted_iota(jnp.int32, (tq, tk), 1)
            late = jnp.where(tied, jnp.where(kt == thr, jnp.where(pos >= cut, 1.0, 0.0), 0.0), 0.0)
            key_ref[:, pl.ds(off, tk)] = jnp.where(late > 0.5, thr - 1, kt)
            return carry

        lax.fori_loop(0, nkt, demote, 0)


def _dsa_index(qi_h, ki_t, wi, kk):
    t = ki_t.shape[1]
    tq, tk = _pick(t, 128), _pick(t, 512)
    ts = tk // 2
    return pl.pallas_call(
        functools.partial(_idx_kernel, tq=tq, tk=tk, ts=ts, t=t, kk=kk), grid=(t // tq,),
        in_specs=[pl.BlockSpec((IDX_HEADS, tq, IDX_DIM), lambda i: (0, i, 0)),
                  pl.BlockSpec((IDX_DIM, t), lambda i: (0, 0)),
                  pl.BlockSpec((tq, 128), lambda i: (i, 0))],
        out_specs=[pl.BlockSpec((tq, t), lambda i: (i, 0)), pl.BlockSpec((tq, 1), lambda i: (i, 0))],
        out_shape=[jax.ShapeDtypeStruct((t, t), jnp.int32), jax.ShapeDtypeStruct((t, 1), jnp.int32)],
        scratch_shapes=[pltpu.VMEM((tk // ts, IDX_HEADS * tq, ts), F32), pltpu.VMEM((IDX_HEADS, tq, LANE), F32)],
        compiler_params=_cparams(("parallel",)), name="dsa_index")(qi_h, ki_t, wi)


def _first_tile(mode, i, tq, tk):
    if mode != "win":
        return 0
    start = i * tq - (WIN - 1)
    return (max(start, 0) if isinstance(i, int) else jnp.maximum(start, 0)) // tk


def _last_tile(i, tq, tk):
    return (i * tq + tq - 1) // tk


def _flash_kernel(qi_ref, kj_ref, *refs, mode, tq, tk, nheads, ngroups, nnear):
    refs, (m_ref, l_ref, acc_ref, s_ref, p_ref, madd_ref) = refs[:-6], refs[-6:]
    if mode == "dsa":
        q_ref, k_ref, v_ref, b_ref, key_ref, thr_ref, o_ref = refs
    elif mode == "slc":
        q_ref, k_ref, v_ref, b_ref, sel_ref, e_ref, o_ref = refs
    else:
        q_ref, k_ref, v_ref, b_ref, oc_ref, os_ref, g_ref, o_ref = refs
    rep = nheads // ngroups
    nch = tk // LANE
    rblk = min(tq, 64)
    s = pl.program_id(0)
    i, j = qi_ref[s], kj_ref[s]
    jlast = _last_tile(i, tq, tk)
    d = i - (tk // tq) * j

    @pl.when(j == _first_tile(mode, i, tq, tk))
    def _():
        m_ref[...] = jnp.full(m_ref.shape, NEG, F32)
        l_ref[...] = jnp.zeros(l_ref.shape, F32)
        acc_ref[...] = jnp.zeros(acc_ref.shape, F32)

    def step(near):
        if near:
            dist = ((i * tq - j * tk) + lax.broadcasted_iota(jnp.int32, (tq, tk), 0)
                    - lax.broadcasted_iota(jnp.int32, (tq, tk), 1))
            causal = dist >= 0
            if mode == "win":
                causal = causal & (dist < WIN)
        if mode == "dsa":
            hit = key_ref[...] >= thr_ref[...]
            madd_ref[...] = jnp.where((hit & causal) if near else hit, 0.0, NEG)
        elif mode == "win":
            madd_ref[...] = jnp.where(causal, 0.0, NEG)
        else:
            hit = jnp.dot(sel_ref[...].reshape(ngroups * tq, sel_ref.shape[2]), e_ref[...],
                          preferred_element_type=F32) > 0.5
            for g in range(ngroups):
                hg = hit[g * tq:(g + 1) * tq]
                madd_ref[g * tq:(g + 1) * tq, :] = jnp.where((hg & causal) if near else hg, 0.0, NEG)
        for g in range(ngroups):
            s_ref[...] = _dot_nt(q_ref[g * rep:(g + 1) * rep].reshape(rep * tq, HEAD_DIM), k_ref[g])
            mrow = g * tq if mode == "slc" else 0
            for r in range(rep):
                h = g * rep + r
                for rb in range(tq // rblk):
                    rows = slice(rb * rblk, (rb + 1) * rblk)
                    srows = slice(r * tq + rb * rblk, r * tq + (rb + 1) * rblk)
                    sh = s_ref[srows, :] + madd_ref[mrow + rb * rblk:mrow + (rb + 1) * rblk, :]
                    if near:
                        sh = sh + b_ref[h, 0, rows, :]
                    chunks = [sh[:, c * LANE:(c + 1) * LANE] for c in range(nch)]
                    m_old = m_ref[h, rows, :]
                    tile_max = jnp.max(functools.reduce(jnp.maximum, chunks), axis=-1, keepdims=True)
                    m_new = jnp.maximum(m_old, tile_max)
                    alpha = jnp.exp2(m_old - m_new)
                    pcs = [jnp.exp2(c - m_new) for c in chunks]
                    l_ref[h, rows, :] = alpha * l_ref[h, rows, :] + functools.reduce(jnp.add, pcs)
                    m_ref[h, rows, :] = m_new
                    acc_ref[h, rows, :] = acc_ref[h, rows, :] * alpha
                    p_ref[srows, :] = jnp.concatenate(pcs, axis=1).astype(BF16)
            pv = jnp.dot(p_ref[...], v_ref[g], preferred_element_type=F32)
            for r in range(rep):
                h = g * rep + r
                acc_ref[h] = acc_ref[h] + pv[r * tq:(r + 1) * tq]

    if mode == "win":
        step(True)
    else:
        pl.when(d < nnear)(lambda: step(True))
        pl.when(d >= nnear)(lambda: step(False))

    @pl.when(j == jlast)
    def _():
        for h in range(nheads):
            cols = slice(h * HEAD_DIM, (h + 1) * HEAD_DIM)
            out = acc_ref[h] / jnp.sum(l_ref[h], axis=-1, keepdims=True)
            if mode == "win":
                g = g_ref[...]
                out = (g[:, 3 * h:3 * h + 1] * oc_ref[:, cols] + g[:, 3 * h + 1:3 * h + 2] * os_ref[:, cols]
                       + g[:, 3 * h + 2:3 * h + 3] * out)
            o_ref[:, cols] = out.astype(o_ref.dtype)


def _flash(mode, q_h, k_h, v_h, bias, extra, out_dtype):
    nheads, t, _ = q_h.shape
    ngroups = k_h.shape[0]
    rep = nheads // ngroups
    _, nnear, tq, tk = bias.shape
    nq = t // tq
    pairs = [(i, j) for i in range(nq) for j in range(_first_tile(mode, i, tq, tk), _last_tile(i, tq, tk) + 1)]
    qi = jnp.asarray([p[0] for p in pairs], jnp.int32)
    kj = jnp.asarray([p[1] for p in pairs], jnp.int32)

    def bmap(s, qi, kj):
        return jnp.minimum(qi[s] - (tk // tq) * kj[s], nnear - 1)

    in_specs = [pl.BlockSpec((nheads, tq, HEAD_DIM), lambda s, qi, kj: (0, qi[s], 0)),
                pl.BlockSpec((ngroups, tk, HEAD_DIM), lambda s, qi, kj: (0, kj[s], 0)),
                pl.BlockSpec((ngroups, tk, HEAD_DIM), lambda s, qi, kj: (0, kj[s], 0)),
                pl.BlockSpec((nheads, 1, tq, tk), lambda s, qi, kj: (0, bmap(s, qi, kj), 0, 0))]
    if mode == "dsa":
        in_specs += [pl.BlockSpec((tq, tk), lambda s, qi, kj: (qi[s], kj[s])),
                     pl.BlockSpec((tq, 1), lambda s, qi, kj: (qi[s], 0))]
    elif mode == "slc":
        ns = extra[0].shape[2]
        in_specs += [pl.BlockSpec((ngroups, tq, ns), lambda s, qi, kj: (0, qi[s], 0)),
                     pl.BlockSpec((ns, tk), lambda s, qi, kj: (0, kj[s]))]
    else:
        row = lambda w: pl.BlockSpec((tq, w), lambda s, qi, kj: (qi[s], 0))
        in_specs += [row(nheads * HEAD_DIM), row(nheads * HEAD_DIM), row(LANE)]
    grid_spec = pltpu.PrefetchScalarGridSpec(
        num_scalar_prefetch=2, grid=(len(pairs),), in_specs=in_specs,
        out_specs=pl.BlockSpec((tq, nheads * HEAD_DIM), lambda s, qi, kj: (qi[s], 0)),
        scratch_shapes=[pltpu.VMEM((nheads, tq, LANE), F32), pltpu.VMEM((nheads, tq, LANE), F32),
                        pltpu.VMEM((nheads, tq, HEAD_DIM), F32),
                        pltpu.VMEM((rep * tq, tk), F32), pltpu.VMEM((rep * tq, tk), BF16),
                        pltpu.VMEM(((ngroups if mode == "slc" else 1) * tq, tk), F32)])
    return pl.pallas_call(
        functools.partial(_flash_kernel, mode=mode, tq=tq, tk=tk, nheads=nheads, ngroups=ngroups, nnear=nnear),
        grid_spec=grid_spec, out_shape=jax.ShapeDtypeStruct((t, nheads * HEAD_DIM), out_dtype),
        compiler_params=_cparams(("arbitrary",)), name="flash_" + mode)(qi, kj, q_h, k_h, v_h, bias, *extra)


def _compress_kernel(c_ref, w1_ref, pe_ref, w2_ref, g_ref, o_ref, *, norm):
    c = c_ref[0]
    half = CMP_STRIDE * HEAD_DIM
    ncp = c.shape[0]
    a = jnp.dot(c, w1_ref[0:half], preferred_element_type=F32)
    b = jnp.dot(c, w1_ref[half:2 * half], preferred_element_type=F32)
    pet = jnp.dot(pe_ref[...], w1_ref[...], preferred_element_type=F32)[0:1]
    hid = _gelu(a + pltpu.roll(b, ncp - 1, 0) + pet)
    y = jnp.dot(hid.astype(BF16), w2_ref[...], preferred_element_type=F32)
    if norm:
        y = _head_norm(y, g_ref[...], 1.0)
    o_ref[0] = y.astype(BF16)


def _compress(slabs, base, pe, w1, w2, gain, norm):
    _, t, _ = slabs.shape
    ncp = t // CMP_STRIDE
    width = CMP_STRIDE * HEAD_DIM
    chunks = slabs[base:base + NSA_KV_HEADS].reshape(NSA_KV_HEADS, ncp, width)
    w1f = w1.reshape(CMP_LEN * HEAD_DIM, CMP_HIDDEN).astype(BF16)
    pe8 = jnp.broadcast_to(pe.reshape(1, CMP_LEN * HEAD_DIM).astype(BF16), (8, CMP_LEN * HEAD_DIM))
    full = lambda shp: pl.BlockSpec(shp, lambda g: (0,) * len(shp))
    return pl.pallas_call(
        functools.partial(_compress_kernel, norm=norm), grid=(NSA_KV_HEADS,),
        in_specs=[pl.BlockSpec((1, ncp, width), lambda g: (g, 0, 0)),
                  full((CMP_LEN * HEAD_DIM, CMP_HIDDEN)), full((8, CMP_LEN * HEAD_DIM)),
                  full((CMP_HIDDEN, HEAD_DIM)), full((1, HEAD_DIM))],
        out_specs=pl.BlockSpec((1, ncp, HEAD_DIM), lambda g: (g, 0, 0)),
        out_shape=jax.ShapeDtypeStruct((NSA_KV_HEADS, ncp, HEAD_DIM), BF16),
        compiler_params=_cparams(("parallel",)), name="compress")(
            chunks, w1f, pe8, w2.astype(BF16), gain.reshape(1, HEAD_DIM))


def _cmp_attn_kernel(tbl_ref, q_ref, kc_ref, vc_ref, ov_ref, oc_ref, sel_ref, s_ref, key_ref,
                     *, tq, ncp, ns, nsel, ww):
    i = pl.program_id(0)
    rep = NSA_HEADS // NSA_KV_HEADS
    qpos = i * tq + lax.broadcasted_iota(jnp.int32, (tq, ncp), 0)
    dist = qpos - (lax.broadcasted_iota(jnp.int32, (tq, ncp), 1) * CMP_STRIDE + CMP_LEN - 1)
    mask = dist >= 0
    w0 = jnp.maximum(i * tq - (CMP_LEN - 1) - FAR_DIST + CMP_STRIDE, 0) // (CMP_STRIDE * LANE) * LANE
    w0 = pl.multiple_of(jnp.minimum(w0, ncp - ww), LANE)
    dist_w = (i * tq + lax.broadcasted_iota(jnp.int32, (tq, ww), 0)
              - ((w0 + lax.broadcasted_iota(jnp.int32, (tq, ww), 1)) * CMP_STRIDE + CMP_LEN - 1))
    bucket = _bucket_of(dist_w)
    blk = lax.broadcasted_iota(jnp.int32, (tq, ns), 1)
    qp = i * tq + lax.broadcasted_iota(jnp.int32, (tq, ns), 0)
    cur = qp // SLC_LEN
    forced = (blk == 0) | (blk == cur) | (blk == cur - 1)
    admissible = blk * SLC_LEN <= qp
    for g in range(NSA_KV_HEADS):
        s_ref[...] = _dot_nt(q_ref[g * rep:(g + 1) * rep].reshape(rep * tq, HEAD_DIM), kc_ref[g])
        imp = jnp.zeros((tq, ncp), F32)
        ps = []
        for r in range(rep):
            h = g * rep + r
            trow = _bias_row(tbl_ref, h, tq)
            bias = [jnp.take_along_axis(trow, bucket[:, c * LANE:(c + 1) * LANE], axis=1)
                    for c in range(ww // LANE)]
            s_ref[r * tq:(r + 1) * tq, pl.ds(w0, ww)] += jnp.concatenate(bias, axis=1)
            sh = jnp.where(mask, s_ref[r * tq:(r + 1) * tq, :], NEG)
            m = jnp.max(sh, axis=-1, keepdims=True)
            p = jnp.where(mask, jnp.exp2(sh - m), 0.0)
            pc = p / jnp.maximum(jnp.sum(p, axis=-1, keepdims=True), 1e-30)
            imp = imp + pc
            ps.append(pc.astype(BF16))
        o = jnp.dot(jnp.concatenate(ps, axis=0), vc_ref[g], preferred_element_type=F32)
        for r in range(rep):
            h = g * rep + r
            oc_ref[:, h * HEAD_DIM:(h + 1) * HEAD_DIM] = o[r * tq:(r + 1) * tq]
        hi = imp.astype(BF16)
        lo = (imp - hi.astype(F32)).astype(BF16)
        impb = (jnp.dot(hi, ov_ref[...], preferred_element_type=F32)
                + jnp.dot(lo, ov_ref[...], preferred_element_type=F32))
        impb = jnp.where(forced, FORCE_SCORE, impb)
        key_ref[:, g * tq:(g + 1) * tq] = _to_key(jnp.where(admissible, impb, -jnp.inf).T)
    keys = key_ref[...]
    sel_t = jnp.where(keys >= _kth_key_val(keys, nsel, axis=0), 1.0, 0.0)
    for g in range(NSA_KV_HEADS):
        sel_ref[g] = sel_t[:, g * tq:(g + 1) * tq].T.astype(BF16)


def _cmp_attn(table, q_h, kc, vc):
    nheads, t, _ = q_h.shape
    ncp = kc.shape[1]
    ns = t // SLC_LEN
    nsel = min(SLC_TOP, ns)
    tq = _pick(t, 128)
    ww = min(ncp, LANE * (-(-(FAR_DIST + tq + CMP_STRIDE * LANE) // (CMP_STRIDE * LANE))))
    rep = nheads // NSA_KV_HEADS
    c_start = np.arange(ncp)[:, None] * CMP_STRIDE
    s_start = np.arange(ns)[None, :] * SLC_LEN
    overlap = jnp.asarray((c_start < s_start + SLC_LEN) & (c_start + CMP_LEN > s_start), BF16)
    full = lambda shp: pl.BlockSpec(shp, lambda i: (0,) * len(shp))
    return pl.pallas_call(
        functools.partial(_cmp_attn_kernel, tq=tq, ncp=ncp, ns=ns, nsel=nsel, ww=ww), grid=(t // tq,),
        in_specs=[full((nheads, LANE)),
                  pl.BlockSpec((nheads, tq, HEAD_DIM), lambda i: (0, i, 0)),
                  full((NSA_KV_HEADS, ncp, HEAD_DIM)), full((NSA_KV_HEADS, ncp, HEAD_DIM)), full((ncp, ns))],
        out_specs=[pl.BlockSpec((tq, nheads * HEAD_DIM), lambda i: (i, 0)),
                   pl.BlockSpec((NSA_KV_HEADS, tq, ns), lambda i: (0, i, 0))],
        out_shape=[jax.ShapeDtypeStruct((t, nheads * HEAD_DIM), F32),
                   jax.ShapeDtypeStruct((NSA_KV_HEADS, t, ns), BF16)],
        scratch_shapes=[pltpu.VMEM((rep * tq, ncp), F32), pltpu.VMEM((ns, NSA_KV_HEADS * tq), jnp.int32)],
        compiler_params=_cparams(("parallel",)), name="cmp_attn")(_table_rows(table), q_h, kc, vc, overlap)


def _merge_kernel(ya_ref, yb_ref, wa_ref, wb_ref, ga_ref, gb_ref, o_ref):
    za = jnp.dot(ya_ref[...], wa_ref[...], preferred_element_type=F32)
    zb = jnp.dot(yb_ref[...], wb_ref[...], preferred_element_type=F32)
    o_ref[...] = (jax.nn.sigmoid(ga_ref[...]) * za + jax.nn.sigmoid(gb_ref[...]) * zb).astype(o_ref.dtype)


def _merge(y_a, y_b, w_a, w_b, pm):
    t, k = y_a.shape
    d = w_a.shape[1]
    tm, tn = _pick(t, 1024), _pick(d, 512)
    assert OFF_GM % tn == 0
    lhs = pl.BlockSpec((tm, k), lambda i, j: (i, 0))
    rhs = pl.BlockSpec((k, tn), lambda i, j: (0, j))
    return pl.pallas_call(
        _merge_kernel, grid=(t // tm, d // tn),
        in_specs=[lhs, lhs, rhs, rhs, pl.BlockSpec((tm, tn), lambda i, j: (i, OFF_GM // tn + j)),
                  pl.BlockSpec((tm, tn), lambda i, j: (i, (OFF_GM + d) // tn + j))],
        out_specs=pl.BlockSpec((tm, tn), lambda i, j: (i, j)), out_shape=jax.ShapeDtypeStruct((t, d), BF16),
        compiler_params=_cparams(("parallel", "arbitrary")), name="merge")(y_a, y_b, w_a, w_b, pm, pm)


_CAND_FULL = PEER_TOPK // 2


def _peer_cands(a1, a2, op):
    pieces = [op(a1[0:1], a2)]
    pieces += [op(a1[k:k + 1], a2[0:_CAND_FULL]) for k in range(1, _CAND_FULL)]
    pieces.append(op(a1[_CAND_FULL:], a2[0:1]))
    return jnp.concatenate(pieces, axis=0)


def _peer_score_kernel(q_ref, sk_ref, s1_o, e1_o, s2_o, e2_o, thr_o, *, tm):
    row = lax.broadcasted_iota(jnp.int32, (PEER_NKEYS, tm), 0).astype(F32)
    for h in range(PEER_HEADS):
        svals, tops = [], []
        for c in range(2):
            hc = 2 * h + c
            s = _dot_nt(sk_ref[hc], q_ref[:, hc * 128:(hc + 1) * 128].astype(BF16))
            svals.append(s)
            work, top = s, []
            for _ in range(PEER_TOPK):
                mx = jnp.max(work, axis=0, keepdims=True)
                first = jnp.min(jnp.where(work == mx, row, float(PEER_NKEYS)), axis=0, keepdims=True)
                work = jnp.where(row == first, -jnp.inf, work)
                top.append(mx)
            tops.append(jnp.concatenate(top, axis=0))
        a1, a2 = tops
        cand = _peer_cands(a1, a2, jnp.add)
        thr = _kth_largest(cand, PEER_TOPK, 0)
        m1, m2 = a1[0:1], a2[0:1]
        ec = _peer_cands(jnp.exp(a1 - m1), jnp.exp(a2 - m2), jnp.multiply)
        z = jnp.sum(jnp.where(cand >= thr, ec, 0.0), axis=0, keepdims=True)
        s1_o[h] = svals[0]
        s2_o[h] = svals[1]
        e1_o[h] = jnp.exp(svals[0] - m1)
        e2_o[h] = jnp.exp(svals[1] - m2) / z
        thr_o[h:h + 1, :] = thr


def _peer_score(qp, sub_keys):
    t = qp.shape[0]
    tm = _pick(t, 256)
    sk = sub_keys.reshape(2 * PEER_HEADS, PEER_NKEYS, PEER_QDIM // 2).astype(BF16)
    tr = pl.BlockSpec((PEER_HEADS, PEER_NKEYS, tm), lambda i: (0, 0, i))
    shp = jax.ShapeDtypeStruct((PEER_HEADS, PEER_NKEYS, t), F32)
    return pl.pallas_call(
        functools.partial(_peer_score_kernel, tm=tm), grid=(t // tm,),
        in_specs=[pl.BlockSpec((tm, 2 * PEER_HEADS * 128), lambda i: (i, 0)),
                  pl.BlockSpec((2 * PEER_HEADS, PEER_NKEYS, PEER_QDIM // 2), lambda i: (0, 0, 0))],
        out_specs=[tr, tr, tr, tr, pl.BlockSpec((PEER_HEADS, tm), lambda i: (0, i))],
        out_shape=[shp, shp, shp, shp, jax.ShapeDtypeStruct((PEER_HEADS, t), F32)],
        compiler_params=_cparams(("parallel",)), name="peer_score")(qp, sk)


def _peer_act_kernel(h_ref, u_ref, s1_ref, e1_ref, s2_ref, e2_ref, thr_ref, o_ref, w_ref, a_ref, *, tm, te, sub):
    j = pl.program_id(1)
    n1 = te // PEER_NKEYS
    a_ref[...] = _dot_nt(h_ref[...], u_ref[...])
    for ai in range(n1):
        i1 = j * n1 + ai
        s1rows = [s1_ref[h, pl.ds(i1, 1), :] for h in range(PEER_HEADS)]
        e1rows = [e1_ref[h, pl.ds(i1, 1), :] for h in range(PEER_HEADS)]
        for ts in range(tm // LANE):
            tok = slice(ts * LANE, (ts + 1) * LANE)
            wt = jnp.zeros((PEER_NKEYS, LANE), F32)
            for h in range(PEER_HEADS):
                hit = (s1rows[h][:, tok] + s2_ref[h, :, tok]) >= thr_ref[h:h + 1, tok]
                wt = wt + jnp.where(hit, e1rows[h][:, tok] * e2_ref[h, :, tok], 0.0)
            w_ref[tok, ai * PEER_NKEYS:(ai + 1) * PEER_NKEYS] = wt.T
    for r in range(tm // sub):
        rows = slice(r * sub, (r + 1) * sub)
        o_ref[rows, :] = (_gelu(a_ref[rows, :]) * w_ref[rows, :]).astype(o_ref.dtype)


def _peer_act(h2, u, s1, e1, s2, e2, thr):
    t, d = h2.shape
    ne = u.shape[0]
    tm, te, sub = _pick(t, 1024), 512, 256
    once = dict(pipeline_mode=pl.Buffered(1))
    tok = pl.BlockSpec((PEER_HEADS, PEER_NKEYS, tm), lambda i, j: (0, 0, i), **once)
    return pl.pallas_call(
        functools.partial(_peer_act_kernel, tm=tm, te=te, sub=sub), grid=(t // tm, ne // te),
        in_specs=[pl.BlockSpec((tm, d), lambda i, j: (i, 0), **once),
                  pl.BlockSpec((te, d), lambda i, j: (j, 0)),
                  tok, tok, tok, tok, pl.BlockSpec((PEER_HEADS, tm), lambda i, j: (0, i), **once)],
        out_specs=pl.BlockSpec((tm, te), lambda i, j: (i, j)),
        out_shape=jax.ShapeDtypeStruct((t, ne), BF16),
        scratch_shapes=[pltpu.VMEM((tm, te), F32), pltpu.VMEM((tm, te), F32)],
        compiler_params=_cparams(("parallel", "arbitrary")), name="peer_act")(h2, u, s1, e1, s2, e2, thr)


def _pack_w_in(w_in, d):
    offs = np.cumsum([0, DSA_HEADS * 128, DSA_KV_HEADS * 128, DSA_KV_HEADS * 128, IDX_HEADS * IDX_DIM, IDX_DIM,
                      IDX_HEADS, NSA_HEADS * 128, 6 * NSA_KV_HEADS * 128, 3 * NSA_HEADS, 2 * d])
    qa, ka, va, qi, ki, wi, qb, kvb, gb, gm = [w_in[:, int(offs[n]):int(offs[n + 1])] for n in range(10)]
    main = jnp.concatenate([qa, qi, qb, kvb, ka, va, gm], axis=1).astype(BF16)
    pad = lambda n: jnp.zeros((d, n), w_in.dtype)
    tail = jnp.concatenate([ki, wi, pad(128 - IDX_DIM - IDX_HEADS), gb, pad(128 - 3 * NSA_HEADS)], axis=1).astype(BF16)
    return main, tail


def _token_mixer(h, table, w_in, gq_a, gk_a, gq_b, gk_cmp, gk_slc, gk_win, cmp_pe_k, cmp_w1_k, cmp_w2_k,
                 cmp_pe_v, cmp_w1_v, cmp_w2_v, w_branch_a, w_branch_b):
    t, d = h.shape
    w_main, w_tail = _pack_w_in(w_in, d)
    pm = _matmul(h, w_main, name="proj_main")
    pt = _matmul(h, w_tail, name="proj_tail")
    qa_h, qi_h, qb_h, kvb_h, ka_h, va_h, ki, wi, gsig = _prep(pm, pt, gq_a, gk_a, gq_b, gk_slc, gk_win)
    tq, tk = _pick(t, 256), _pick(t, 512)
    keys, thr = _dsa_index(qi_h, ki.T, wi, min(DSA_TOPK, t // 4))
    bias_a = _bias_tiles(table[:, :DSA_HEADS], tq, tk)
    y_a = _flash("dsa", qa_h, ka_h, va_h, bias_a, (keys, thr), BF16)
    g = NSA_KV_HEADS
    kc = _compress(kvb_h, 0, cmp_pe_k, cmp_w1_k, cmp_w2_k, gk_cmp, True)
    vc = _compress(kvb_h, g, cmp_pe_v, cmp_w1_v, cmp_w2_v, gk_cmp, False)
    o_c, sel = _cmp_attn(table[:, DSA_HEADS:], qb_h, kc, vc)
    bias_b = _bias_tiles(table[:, DSA_HEADS:], tq, tk)
    ns = t // SLC_LEN
    expand = jnp.asarray(np.arange(ns)[:, None] == (np.arange(t)[None, :] // SLC_LEN), BF16)
    o_s = _flash("slc", qb_h, kvb_h[2 * g:3 * g], kvb_h[3 * g:4 * g], bias_b, (sel, expand), F32)
    y_b = _flash("win", qb_h, kvb_h[4 * g:5 * g], kvb_h[5 * g:6 * g], bias_b, (o_c, o_s, gsig), BF16)
    return _merge(y_a, y_b, w_branch_a.astype(BF16), w_branch_b.astype(BF16), pm)


def _peer_ffn(h2, x1, gt2, w_q, sub_keys, u, v):
    qp = _matmul(h2, w_q.astype(BF16), name="peer_q")
    s1, e1, s2, e2, thr = _peer_score(qp, sub_keys)
    act = _peer_act(h2, u.astype(BF16), s1, e1, s2, e2, thr)
    return _matmul_resid(act, v.astype(BF16), x1, gt2, name="peer_out")


def kernel(x, c, rel_bias, w_ada, b_ada, g_mix, w_in, gq_a, gk_a, gq_b, gk_cmp, gk_slc, gk_win, cmp_pe_k, cmp_w1_k,
           cmp_w2_k, cmp_pe_v, cmp_w1_v, cmp_w2_v, w_branch_a, w_branch_b, w_out, g_ffn, w_peer_q, peer_sub_keys,
           peer_u, peer_v):
    bsz, t, d = x.shape
    assert bsz == 1 and t % 512 == 0 and d % 128 == 0
    xs = x[0]
    for i in range(w_ada.shape[0]):
        mod = _ada(c, w_ada[i], b_ada[i])
        sh1, sc1, gt1, sh2, sc2, gt2 = [mod[:, n * d:(n + 1) * d] for n in range(6)]
        h = _normmod(xs, g_mix[i], sc1, sh1)
        merged = _token_mixer(h, rel_bias, w_in[i], gq_a[i], gk_a[i], gq_b[i], gk_cmp[i], gk_slc[i], gk_win[i],
                              cmp_pe_k[i], cmp_w1_k[i], cmp_w2_k[i], cmp_pe_v[i], cmp_w1_v[i], cmp_w2_v[i],
                              w_branch_a[i], w_branch_b[i])
        x1 = _matmul_resid(merged, w_out[i].astype(BF16), xs, gt1, name="w_out")
        h2 = _normmod(x1, g_ffn[i], sc2, sh2)
        xs = _peer_ffn(h2, x1, gt2, w_peer_q[i], peer_sub_keys[i], peer_u[i], peer_v[i])
    return xs[None]
```

```python
import functools
import math

import numpy as np
import jax
import jax.numpy as jnp
from jax import lax
from jax.experimental import pallas as pl
from jax.experimental.pallas import tpu as pltpu

HEAD_DIM = 128
DSA_HEADS = 16
DSA_KV_HEADS = 4
IDX_HEADS = 32
IDX_DIM = 64
DSA_TOPK = 256
NSA_HEADS = 16
NSA_KV_HEADS = 4
CMP_LEN = 32
CMP_STRIDE = 16
CMP_HIDDEN = 256
SLC_LEN = 64
SLC_TOP = 16
WIN = 512
FORCE_SCORE = 1e9
PEER_HEADS = 8
PEER_NKEYS = 128
PEER_QDIM = 256
PEER_TOPK = 16
REL_BUCKETS = 32
REL_MAX_DIST = 2048
EPS = 1e-6

LANE = 128
VMEM_LIMIT = 56 * 1024 * 1024
NEG = -1e30
LOG2E = math.log2(math.e)
INT_MIN = -2 ** 31
KEY_NEG_INF = -2 ** 31 + 0x007FFFFF

OFF_QA, OFF_QI, OFF_QB, OFF_KVB, OFF_KA, OFF_VA, OFF_GM = 0, 2048, 4096, 6144, 9216, 9728, 10240

F32 = jnp.float32
BF16 = jnp.bfloat16


def _bucket_starts():
    n = np.arange(2 * REL_MAX_DIST)
    exact = REL_BUCKETS // 2
    nf = np.maximum(n, 1).astype(np.float32)
    lb = exact + (np.log(nf / np.float32(exact)) / np.float32(math.log(REL_MAX_DIST / exact))
                  * np.float32(REL_BUCKETS - exact)).astype(np.int32)
    bucket = np.where(n < exact, n, np.minimum(lb, REL_BUCKETS - 1))
    return [int(np.argmax(bucket >= b)) for b in range(REL_BUCKETS)]


BUCKET_STARTS = _bucket_starts()
FAR_DIST = BUCKET_STARTS[REL_BUCKETS - 1]


def _cparams(sem):
    return pltpu.CompilerParams(dimension_semantics=sem, vmem_limit_bytes=VMEM_LIMIT)


def _pick(n, pref):
    t = pref
    while n % t:
        t //= 2
    return t


def _to_key(x):
    b = lax.bitcast_convert_type(x, jnp.int32)
    return b ^ ((b >> 31) & 0x7FFFFFFF)


def _from_key(k):
    return lax.bitcast_convert_type(k ^ ((k >> 31) & 0x7FFFFFFF), F32)


def _kth_key(count_ge, shape, kk):
    def bit_pass(b, x):
        cand = x + lax.shift_left(jnp.int32(1), 31 - b)
        return jnp.where(count_ge(cand) >= kk, cand, x)
    x = lax.fori_loop(0, 32, bit_pass, jnp.full(shape, INT_MIN, jnp.int32))
    return jnp.maximum(x, KEY_NEG_INF)


def _kth_key_bracket(count_ge, kmin, kmax, kk):
    c_min = count_ge(kmin)
    few = c_min < kk
    lo0 = jnp.where(few, KEY_NEG_INF, kmin)
    hi0 = kmax + 1
    done0 = jnp.where(few | (c_min == kk) | (hi0 - 1 <= lo0), 1.0, 0.0)

    def cond(st):
        p, _, _, _, _, done = st
        return jnp.logical_and(p < 72, jnp.min(done) < 0.5)

    def body(st):
        p, lo, hi, clo, chi, done = st
        mid = (lo >> 1) + (hi >> 1) + (lo & hi & 1)
        flo, fhi = _from_key(lo), _from_key(hi)
        frac = (jnp.log(clo) - math.log(kk)) / (jnp.log(clo) - jnp.log(jnp.maximum(chi, 0.5)))
        cand = jnp.where(p % 2 == 1, mid, _to_key(flo + frac * (fhi - flo)))
        cand = jnp.minimum(jnp.maximum(cand, lo + 1), hi - 1)
        active = done < 0.5
        cand = jnp.where(active, cand, lo)
        cnt = count_ge(cand)
        up = active & (cnt >= kk)
        dn = active & (cnt < kk)
        lo, clo = jnp.where(up, cand, lo), jnp.where(up, cnt, clo)
        hi, chi = jnp.where(dn, cand, hi), jnp.where(dn, cnt, chi)
        done = jnp.where((clo == kk) | (hi - 1 <= lo), 1.0, done)
        return p + 1, lo, hi, clo, chi, done

    st = lax.while_loop(cond, body, (jnp.int32(0), lo0, hi0, c_min, jnp.zeros_like(c_min), done0))
    return st[1], st[3]


def _kth_largest(vals, kk, axis):
    shape = list(vals.shape)
    shape[axis] = 1

    def body(_, st):
        work, taken, thr = st
        mx = jnp.max(work, axis=axis, keepdims=True)
        hit = work == mx
        thr = jnp.where(taken < kk, mx, thr)
        taken = taken + jnp.sum(jnp.where(hit, 1.0, 0.0), axis=axis, keepdims=True)
        return jnp.where(hit, -jnp.inf, work), taken, thr

    st = (vals, jnp.zeros(shape, F32), jnp.full(shape, -jnp.inf, F32))
    return lax.fori_loop(0, kk, body, st)[2]


def _kth_key_val(keys, kk, axis=-1):
    def count_ge(cand):
        return jnp.sum(jnp.where(keys >= cand, 1.0, 0.0), axis=axis, keepdims=True)
    shape = list(keys.shape)
    shape[axis] = 1
    return _kth_key(count_ge, tuple(shape), float(kk))


def _gelu(x):
    return 0.5 * x * (1.0 + jnp.tanh(math.sqrt(2.0 / math.pi) * (x + 0.044715 * (x * x * x))))


def _dot_nt(a, b):
    return lax.dot_general(a, b, (((1,), (1,)), ((), ())), preferred_element_type=F32)


def _head_norm(x, g, scale):
    ms = jnp.mean(x * x, axis=-1, keepdims=True)
    return x * lax.rsqrt(ms + EPS) * g * scale


def _mm_kernel(a_ref, b_ref, o_ref):
    o_ref[...] = jnp.dot(a_ref[...], b_ref[...], preferred_element_type=F32).astype(o_ref.dtype)


def _matmul(a, b, out_dtype=F32, tm=1024, tn=512, name="matmul"):
    m, k = a.shape
    n = b.shape[1]
    tm, tn = _pick(m, tm), _pick(n, tn)
    return pl.pallas_call(
        _mm_kernel, grid=(m // tm, n // tn),
        in_specs=[pl.BlockSpec((tm, k), lambda i, j: (i, 0)), pl.BlockSpec((k, tn), lambda i, j: (0, j))],
        out_specs=pl.BlockSpec((tm, tn), lambda i, j: (i, j)),
        out_shape=jax.ShapeDtypeStruct((m, n), out_dtype),
        compiler_params=_cparams(("parallel", "arbitrary")), name=name)(a, b)


def _mm_resid_kernel(a_ref, b_ref, x_ref, gt_ref, o_ref):
    k = pl.program_id(2)
    part = jnp.dot(a_ref[...], b_ref[...], preferred_element_type=F32)

    @pl.when(k == 0)
    def _():
        o_ref[...] = part

    @pl.when(k > 0)
    def _():
        o_ref[...] += part

    @pl.when(k == pl.num_programs(2) - 1)
    def _():
        o_ref[...] = x_ref[...] + gt_ref[...] * o_ref[...]


def _matmul_resid(a, b, x, gt, name):
    m, kd = a.shape
    n = b.shape[1]
    tm, tn, tk = _pick(m, 1024), _pick(n, 1024), _pick(kd, 2048)
    return pl.pallas_call(
        _mm_resid_kernel, grid=(m // tm, n // tn, kd // tk),
        in_specs=[pl.BlockSpec((tm, tk), lambda i, j, k: (i, k)), pl.BlockSpec((tk, tn), lambda i, j, k: (k, j)),
                  pl.BlockSpec((tm, tn), lambda i, j, k: (i, j)), pl.BlockSpec((1, tn), lambda i, j, k: (0, j))],
        out_specs=pl.BlockSpec((tm, tn), lambda i, j, k: (i, j)),
        out_shape=jax.ShapeDtypeStruct((m, n), F32),
        compiler_params=_cparams(("parallel", "parallel", "arbitrary")), name=name)(a, b, x, gt)


def _ada_kernel(c_ref, w_ref, b_ref, o_ref):
    o_ref[...] = jnp.dot(c_ref[...], w_ref[...].astype(BF16), preferred_element_type=F32) + b_ref[...]


def _ada(c, w, b):
    d, n = w.shape
    tn = _pick(n, 512)
    c8 = jnp.broadcast_to(c.astype(BF16), (8, d))
    out = pl.pallas_call(
        _ada_kernel, grid=(n // tn,),
        in_specs=[pl.BlockSpec((8, d), lambda j: (0, 0)), pl.BlockSpec((d, tn), lambda j: (0, j)),
                  pl.BlockSpec((1, tn), lambda j: (0, j))],
        out_specs=pl.BlockSpec((8, tn), lambda j: (0, j)),
        out_shape=jax.ShapeDtypeStruct((8, n), F32),
        compiler_params=_cparams(("arbitrary",)), name="ada")(c8, w, b.reshape(1, n))
    return out[0:1]


def _normmod_kernel(x_ref, g_ref, sc_ref, sh_ref, o_ref):
    x = x_ref[...]
    y = x * lax.rsqrt(jnp.mean(x * x, axis=-1, keepdims=True) + EPS) * g_ref[...]
    o_ref[...] = (y * (1.0 + sc_ref[...]) + sh_ref[...]).astype(o_ref.dtype)


def _normmod(x, g, sc, sh):
    t, d = x.shape
    tm = _pick(t, 256)
    row = pl.BlockSpec((1, d), lambda i: (0, 0))
    return pl.pallas_call(
        _normmod_kernel, grid=(t // tm,),
        in_specs=[pl.BlockSpec((tm, d), lambda i: (i, 0)), row, row, row],
        out_specs=pl.BlockSpec((tm, d), lambda i: (i, 0)),
        out_shape=jax.ShapeDtypeStruct((t, d), BF16),
        compiler_params=_cparams(("parallel",)), name="normmod")(x, g.reshape(1, d), sc, sh)


def _prep_kernel(qa_ref, qi_ref, qb_ref, kvb_ref, ka_ref, va_ref, tail_ref,
                 gqa_ref, gka_ref, gqb_ref, gslc_ref, gwin_ref,
                 qa_o, qi_o, qb_o, kvb_o, ka_o, va_o, ki_o, wi_o, gs_o):
    qscale = HEAD_DIM ** -0.5 * LOG2E
    for h in range(DSA_HEADS):
        qa_o[h] = _head_norm(qa_ref[:, h * 128:(h + 1) * 128], gqa_ref[...], qscale).astype(BF16)
    for h in range(NSA_HEADS):
        qb_o[h] = _head_norm(qb_ref[:, h * 128:(h + 1) * 128], gqb_ref[...], qscale).astype(BF16)
    for h in range(IDX_HEADS):
        qi_o[h] = (qi_ref[:, h * IDX_DIM:(h + 1) * IDX_DIM] * IDX_DIM ** -0.5).astype(BF16)
    for g in range(DSA_KV_HEADS):
        ka_o[g] = _head_norm(ka_ref[:, g * 128:(g + 1) * 128], gka_ref[...], 1.0).astype(BF16)
        va_o[g] = va_ref[:, g * 128:(g + 1) * 128].astype(BF16)
    for s in range(6 * NSA_KV_HEADS):
        xs = kvb_ref[:, s * 128:(s + 1) * 128]
        part = s // NSA_KV_HEADS
        if part == 2:
            xs = _head_norm(xs, gslc_ref[...], 1.0)
        elif part == 4:
            xs = _head_norm(xs, gwin_ref[...], 1.0)
        kvb_o[s] = xs.astype(BF16)
    tail = tail_ref[...]
    ki_o[...] = tail[:, 0:IDX_DIM].astype(BF16)
    wi_o[...] = tail[:, 0:128] * IDX_HEADS ** -0.5
    gs_o[...] = jax.nn.sigmoid(tail[:, 128:256])


def _prep(pm, pt, gq_a, gk_a, gq_b, gk_slc, gk_win):
    t = pm.shape[0]
    tm = _pick(t, 128)
    g128 = pl.BlockSpec((1, 128), lambda i: (0, 0))

    def col(w, off):
        assert off % w == 0
        return pl.BlockSpec((tm, w), lambda i: (i, off // w))

    def hm(n, w):
        return pl.BlockSpec((n, tm, w), lambda i: (0, i, 0))

    row = lambda w: pl.BlockSpec((tm, w), lambda i: (i, 0))
    outs = pl.pallas_call(
        _prep_kernel, grid=(t // tm,),
        in_specs=[col(2048, OFF_QA), col(2048, OFF_QI), col(2048, OFF_QB), col(3072, OFF_KVB),
                  col(512, OFF_KA), col(512, OFF_VA), row(256), g128, g128, g128, g128, g128],
        out_specs=[hm(DSA_HEADS, 128), hm(IDX_HEADS, IDX_DIM), hm(NSA_HEADS, 128), hm(24, 128),
                   hm(DSA_KV_HEADS, 128), hm(DSA_KV_HEADS, 128), row(IDX_DIM), row(128), row(128)],
        out_shape=[jax.ShapeDtypeStruct((DSA_HEADS, t, 128), BF16),
                   jax.ShapeDtypeStruct((IDX_HEADS, t, IDX_DIM), BF16),
                   jax.ShapeDtypeStruct((NSA_HEADS, t, 128), BF16),
                   jax.ShapeDtypeStruct((24, t, 128), BF16),
                   jax.ShapeDtypeStruct((DSA_KV_HEADS, t, 128), BF16),
                   jax.ShapeDtypeStruct((DSA_KV_HEADS, t, 128), BF16),
                   jax.ShapeDtypeStruct((t, IDX_DIM), BF16),
                   jax.ShapeDtypeStruct((t, 128), F32),
                   jax.ShapeDtypeStruct((t, 128), F32)],
        compiler_params=_cparams(("parallel",)), name="prep")(
            pm, pm, pm, pm, pm, pm, pt,
            gq_a.reshape(1, 128), gk_a.reshape(1, 128), gq_b.reshape(1, 128),
            gk_slc.reshape(1, 128), gk_win.reshape(1, 128))
    return outs


def _bucket_of(dist):
    bucket = jnp.zeros(dist.shape, jnp.int32)
    for b in range(1, REL_BUCKETS):
        bucket = jnp.where(dist >= BUCKET_STARTS[b], b, bucket)
    return bucket


def _bias_row(tbl_ref, h, rows):
    trow = (tbl_ref[h:h + 1, :] - tbl_ref[h:h + 1, REL_BUCKETS - 1:REL_BUCKETS]) * LOG2E
    return jnp.broadcast_to(trow, (rows, LANE))


def _bias_tiles_kernel(tbl_ref, o_ref, *, tq, tk, nheads):
    d = pl.program_id(0)
    dist = d * tq + lax.broadcasted_iota(jnp.int32, (tq, tk), 0) - lax.broadcasted_iota(jnp.int32, (tq, tk), 1)
    bucket = _bucket_of(dist)
    for h in range(nheads):
        trow = _bias_row(tbl_ref, h, tq)
        o_ref[h, 0] = jnp.concatenate(
            [jnp.take_along_axis(trow, bucket[:, c * LANE:(c + 1) * LANE], axis=1) for c in range(tk // LANE)], axis=1)


def _n_near(tq, tk):
    return -(-(FAR_DIST + tk - 1) // tq)


def _bias_tiles(table, tq, tk):
    nheads = table.shape[1]
    nd = _n_near(tq, tk)
    return pl.pallas_call(
        functools.partial(_bias_tiles_kernel, tq=tq, tk=tk, nheads=nheads), grid=(nd,),
        in_specs=[pl.BlockSpec((nheads, LANE), lambda d: (0, 0))],
        out_specs=pl.BlockSpec((nheads, 1, tq, tk), lambda d: (0, d, 0, 0)),
        out_shape=jax.ShapeDtypeStruct((nheads, nd, tq, tk), F32),
        compiler_params=_cparams(("parallel",)), name="bias_tiles")(_table_rows(table))


def _table_rows(table):
    return jnp.pad(table.T, ((0, 0), (0, LANE - REL_BUCKETS)))


def _idx_kernel(q_ref, kt_ref, w_ref, key_ref, thr_ref, s_ref, wb_ref, *, tq, tk, ts, t, kk):
    i = pl.program_id(0)
    nkt = (i * tq + tq + tk - 1) // tk
    w = w_ref[...]
    for h in range(IDX_HEADS):
        wb_ref[h] = jnp.broadcast_to(w[:, IDX_DIM + h:IDX_DIM + h + 1], (tq, LANE))
    qall = q_ref[...].reshape(IDX_HEADS * tq, IDX_DIM)
    rows = lax.broadcasted_iota(jnp.int32, (tq, ts), 0)
    cols = lax.broadcasted_iota(jnp.int32, (tq, ts), 1)

    def tile(j, carry):
        offs = [pl.multiple_of(j * tk + u * ts, ts) for u in range(tk // ts)]
        for u, off in enumerate(offs):
            s_ref[u] = jnp.dot(qall, kt_ref[:, pl.ds(off, ts)], preferred_element_type=F32)
        for u, off in enumerate(offs):
            acc = jnp.zeros((tq, ts), F32)
            for h in range(IDX_HEADS):
                wbh = jnp.concatenate([wb_ref[h]] * (ts // LANE), axis=1)
                acc = acc + jnp.maximum(s_ref[u, h * tq:(h + 1) * tq, :], 0.0) * wbh
            acc = jnp.where(off + cols <= i * tq + rows, acc, -jnp.inf)
            key_ref[:, pl.ds(off, ts)] = _to_key(acc)
        return carry

    lax.fori_loop(0, nkt, tile, 0)

    def fill(j, carry):
        key_ref[:, pl.ds(pl.multiple_of(j * tk, tk), tk)] = jnp.full((tq, tk), KEY_NEG_INF, jnp.int32)
        return carry

    lax.fori_loop(nkt, t // tk, fill, 0)

    def count_ge(cand):
        def body(j, c):
            kt = key_ref[:, pl.ds(pl.multiple_of(j * tk, tk), tk)]
            ge = jnp.where(kt >= cand, 1.0, 0.0)
            for a in range(tk // LANE):
                c = c + ge[:, a * LANE:(a + 1) * LANE]
            return c
        c = lax.fori_loop(0, nkt, body, jnp.zeros((tq, LANE), F32))
        return jnp.sum(c, axis=-1, keepdims=True)

    def score_range(j, c):
        smin, smax = c
        st = _from_key(key_ref[:, pl.ds(pl.multiple_of(j * tk, tk), tk)])
        lo_t = jnp.where(st == -jnp.inf, jnp.inf, st)
        for a in range(tk // LANE):
            smin = jnp.minimum(smin, lo_t[:, a * LANE:(a + 1) * LANE])
            smax = jnp.maximum(smax, st[:, a * LANE:(a + 1) * LANE])
        return smin, smax

    smin, smax = lax.fori_loop(0, nkt, score_range, (jnp.full((tq, LANE), jnp.inf, F32),
                                                     jnp.full((tq, LANE), -jnp.inf, F32)))
    kmin = _to_key(jnp.min(smin, axis=-1, keepdims=True))
    kmax = _to_key(jnp.max(smax, axis=-1, keepdims=True))
    thr, reach = _kth_key_bracket(count_ge, kmin, kmax, float(kk))
    thr_ref[...] = thr
    tied = reach > kk

    @pl.when(jnp.max(jnp.where(tied, 1.0, 0.0)) > 0.5)
    def _():
        need = kk - count_ge(thr + 1)

        def count_tied_before(cut):
            def body(j, c):
                off = pl.multiple_of(j * tk, tk)
                kt = key_ref[:, pl.ds(off, tk)]
                pos = off + lax.broadcasted_iota(jnp.int32, (tq, tk), 1)
                m = jnp.where(kt == thr, jnp.where(pos < cut, 1.0, 0.0), 0.0)
                for a in range(tk // LANE):
                    c = c + m[:, a * LANE:(a + 1) * LANE]
                return c
            c = lax.fori_loop(0, nkt, body, jnp.zeros((tq, LANE), F32))
            return jnp.sum(c, axis=-1, keepdims=True)

        def halve(_, st):
            lo_c, hi_c = st
            mid = (lo_c + hi_c) >> 1
            enough = count_tied_before(mid) >= need
            return jnp.where(enough, lo_c, mid), jnp.where(enough, mid, hi_c)

        steps = max(t, 2).bit_length()
        _, cut = lax.fori_loop(0, steps, halve, (jnp.zeros((tq, 1), jnp.int32), jnp.full((tq, 1), t, jnp.int32)))

        def demote(j, carry):
            off = pl.multiple_of(j * tk, tk)
            kt = key_ref[:, pl.ds(off, tk)]
            pos = off + lax.broadcasted_iota(jnp.int32, (tq, tk), 1)
            late = jnp.where(tied, jnp.where(kt == thr, jnp.where(pos >= cut, 1.0, 0.0), 0.0), 0.0)
            key_ref[:, pl.ds(off, tk)] = jnp.where(late > 0.5, thr - 1, kt)
            return carry

        lax.fori_loop(0, nkt, demote, 0)


def _dsa_index(qi_h, ki_t, wi, kk):
    t = ki_t.shape[1]
    tq, tk = _pick(t, 128), _pick(t, 512)
    ts = tk // 2
    return pl.pallas_call(
        functools.partial(_idx_kernel, tq=tq, tk=tk, ts=ts, t=t, kk=kk), grid=(t // tq,),
        in_specs=[pl.BlockSpec((IDX_HEADS, tq, IDX_DIM), lambda i: (0, i, 0)),
                  pl.BlockSpec((IDX_DIM, t), lambda i: (0, 0)),
                  pl.BlockSpec((tq, 128), lambda i: (i, 0))],
        out_specs=[pl.BlockSpec((tq, t), lambda i: (i, 0)), pl.BlockSpec((tq, 1), lambda i: (i, 0))],
        out_shape=[jax.ShapeDtypeStruct((t, t), jnp.int32), jax.ShapeDtypeStruct((t, 1), jnp.int32)],
        scratch_shapes=[pltpu.VMEM((tk // ts, IDX_HEADS * tq, ts), F32), pltpu.VMEM((IDX_HEADS, tq, LANE), F32)],
        compiler_params=_cparams(("parallel",)), name="dsa_index")(qi_h, ki_t, wi)


def _first_tile(mode, i, tq, tk):
    if mode != "win":
        return 0
    start = i * tq - (WIN - 1)
    return (max(start, 0) if isinstance(i, int) else jnp.maximum(start, 0)) // tk


def _last_tile(i, tq, tk):
    return (i * tq + tq - 1) // tk


def _flash_kernel(qi_ref, kj_ref, *refs, mode, tq, tk, nheads, ngroups, nnear):
    refs, (m_ref, l_ref, acc_ref, s_ref, p_ref, madd_ref) = refs[:-6], refs[-6:]
    if mode == "dsa":
        q_ref, k_ref, v_ref, b_ref, key_ref, thr_ref, o_ref = refs
    elif mode == "slc":
        q_ref, k_ref, v_ref, b_ref, sel_ref, e_ref, o_ref = refs
    else:
        q_ref, k_ref, v_ref, b_ref, oc_ref, os_ref, g_ref, o_ref = refs
    rep = nheads // ngroups
    nch = tk // LANE
    rblk = min(tq, 64)
    s = pl.program_id(0)
    i, j = qi_ref[s], kj_ref[s]
    jlast = _last_tile(i, tq, tk)
    d = i - (tk // tq) * j

    @pl.when(j == _first_tile(mode, i, tq, tk))
    def _():
        m_ref[...] = jnp.full(m_ref.shape, NEG, F32)
        l_ref[...] = jnp.zeros(l_ref.shape, F32)
        acc_ref[...] = jnp.zeros(acc_ref.shape, F32)

    def step(near):
        if near:
            dist = ((i * tq - j * tk) + lax.broadcasted_iota(jnp.int32, (tq, tk), 0)
                    - lax.broadcasted_iota(jnp.int32, (tq, tk), 1))
            causal = dist >= 0
            if mode == "win":
                causal = causal & (dist < WIN)
        if mode == "dsa":
            hit = key_ref[...] >= thr_ref[...]
            madd_ref[...] = jnp.where((hit & causal) if near else hit, 0.0, NEG)
        elif mode == "win":
            madd_ref[...] = jnp.where(causal, 0.0, NEG)
        else:
            hit = jnp.dot(sel_ref[...].reshape(ngroups * tq, sel_ref.shape[2]), e_ref[...],
                          preferred_element_type=F32) > 0.5
            for g in range(ngroups):
                hg = hit[g * tq:(g + 1) * tq]
                madd_ref[g * tq:(g + 1) * tq, :] = jnp.where((hg & causal) if near else hg, 0.0, NEG)
        for g in range(ngroups):
            s_ref[...] = _dot_nt(q_ref[g * rep:(g + 1) * rep].reshape(rep * tq, HEAD_DIM), k_ref[g])
            mrow = g * tq if mode == "slc" else 0
            for r in range(rep):
                h = g * rep + r
                for rb in range(tq // rblk):
                    rows = slice(rb * rblk, (rb + 1) * rblk)
                    srows = slice(r * tq + rb * rblk, r * tq + (rb + 1) * rblk)
                    sh = s_ref[srows, :] + madd_ref[mrow + rb * rblk:mrow + (rb + 1) * rblk, :]
                    if near:
                        sh = sh + b_ref[h, 0, rows, :]
                    chunks = [sh[:, c * LANE:(c + 1) * LANE] for c in range(nch)]
                    m_old = m_ref[h, rows, :]
                    tile_max = jnp.max(functools.reduce(jnp.maximum, chunks), axis=-1, keepdims=True)
                    m_new = jnp.maximum(m_old, tile_max)
                    alpha = jnp.exp2(m_old - m_new)
                    pcs = [jnp.exp2(c - m_new) for c in chunks]
                    l_ref[h, rows, :] = alpha * l_ref[h, rows, :] + functools.reduce(jnp.add, pcs)
                    m_ref[h, rows, :] = m_new
                    acc_ref[h, rows, :] = acc_ref[h, rows, :] * alpha
                    p_ref[srows, :] = jnp.concatenate(pcs, axis=1).astype(BF16)
            pv = jnp.dot(p_ref[...], v_ref[g], preferred_element_type=F32)
            for r in range(rep):
                h = g * rep + r
                acc_ref[h] = acc_ref[h] + pv[r * tq:(r + 1) * tq]

    if mode == "win":
        step(True)
    else:
        pl.when(d < nnear)(lambda: step(True))
        pl.when(d >= nnear)(lambda: step(False))

    @pl.when(j == jlast)
    def _():
        for h in range(nheads):
            cols = slice(h * HEAD_DIM, (h + 1) * HEAD_DIM)
            out = acc_ref[h] / jnp.sum(l_ref[h], axis=-1, keepdims=True)
            if mode == "win":
                g = g_ref[...]
                out = (g[:, 3 * h:3 * h + 1] * oc_ref[:, cols] + g[:, 3 * h + 1:3 * h + 2] * os_ref[:, cols]
                       + g[:, 3 * h + 2:3 * h + 3] * out)
            o_ref[:, cols] = out.astype(o_ref.dtype)


def _flash(mode, q_h, k_h, v_h, bias, extra, out_dtype):
    nheads, t, _ = q_h.shape
    ngroups = k_h.shape[0]
    rep = nheads // ngroups
    _, nnear, tq, tk = bias.shape
    nq = t // tq
    pairs = [(i, j) for i in range(nq) for j in range(_first_tile(mode, i, tq, tk), _last_tile(i, tq, tk) + 1)]
    qi = jnp.asarray([p[0] for p in pairs], jnp.int32)
    kj = jnp.asarray([p[1] for p in pairs], jnp.int32)

    def bmap(s, qi, kj):
        return jnp.minimum(qi[s] - (tk // tq) * kj[s], nnear - 1)

    in_specs = [pl.BlockSpec((nheads, tq, HEAD_DIM), lambda s, qi, kj: (0, qi[s], 0)),
                pl.BlockSpec((ngroups, tk, HEAD_DIM), lambda s, qi, kj: (0, kj[s], 0)),
                pl.BlockSpec((ngroups, tk, HEAD_DIM), lambda s, qi, kj: (0, kj[s], 0)),
                pl.BlockSpec((nheads, 1, tq, tk), lambda s, qi, kj: (0, bmap(s, qi, kj), 0, 0))]
    if mode == "dsa":
        in_specs += [pl.BlockSpec((tq, tk), lambda s, qi, kj: (qi[s], kj[s])),
                     pl.BlockSpec((tq, 1), lambda s, qi, kj: (qi[s], 0))]
    elif mode == "slc":
        ns = extra[0].shape[2]
        in_specs += [pl.BlockSpec((ngroups, tq, ns), lambda s, qi, kj: (0, qi[s], 0)),
                     pl.BlockSpec((ns, tk), lambda s, qi, kj: (0, kj[s]))]
    else:
        row = lambda w: pl.BlockSpec((tq, w), lambda s, qi, kj: (qi[s], 0))
        in_specs += [row(nheads * HEAD_DIM), row(nheads * HEAD_DIM), row(LANE)]
    grid_spec = pltpu.PrefetchScalarGridSpec(
        num_scalar_prefetch=2, grid=(len(pairs),), in_specs=in_specs,
        out_specs=pl.BlockSpec((tq, nheads * HEAD_DIM), lambda s, qi, kj: (qi[s], 0)),
        scratch_shapes=[pltpu.VMEM((nheads, tq, LANE), F32), pltpu.VMEM((nheads, tq, LANE), F32),
                        pltpu.VMEM((nheads, tq, HEAD_DIM), F32),
                        pltpu.VMEM((rep * tq, tk), F32), pltpu.VMEM((rep * tq, tk), BF16),
                        pltpu.VMEM(((ngroups if mode == "slc" else 1) * tq, tk), F32)])
    return pl.pallas_call(
        functools.partial(_flash_kernel, mode=mode, tq=tq, tk=tk, nheads=nheads, ngroups=ngroups, nnear=nnear),
        grid_spec=grid_spec, out_shape=jax.ShapeDtypeStruct((t, nheads * HEAD_DIM), out_dtype),
        compiler_params=_cparams(("arbitrary",)), name="flash_" + mode)(qi, kj, q_h, k_h, v_h, bias, *extra)


def _compress_kernel(c_ref, w1_ref, pe_ref, w2_ref, g_ref, o_ref, *, norm):
    c = c_ref[0]
    half = CMP_STRIDE * HEAD_DIM
    ncp = c.shape[0]
    a = jnp.dot(c, w1_ref[0:half], preferred_element_type=F32)
    b = jnp.dot(c, w1_ref[half:2 * half], preferred_element_type=F32)
    pet = jnp.dot(pe_ref[...], w1_ref[...], preferred_element_type=F32)[0:1]
    hid = _gelu(a + pltpu.roll(b, ncp - 1, 0) + pet)
    y = jnp.dot(hid.astype(BF16), w2_ref[...], preferred_element_type=F32)
    if norm:
        y = _head_norm(y, g_ref[...], 1.0)
    o_ref[0] = y.astype(BF16)


def _compress(slabs, base, pe, w1, w2, gain, norm):
    _, t, _ = slabs.shape
    ncp = t // CMP_STRIDE
    width = CMP_STRIDE * HEAD_DIM
    chunks = slabs[base:base + NSA_KV_HEADS].reshape(NSA_KV_HEADS, ncp, width)
    w1f = w1.reshape(CMP_LEN * HEAD_DIM, CMP_HIDDEN).astype(BF16)
    pe8 = jnp.broadcast_to(pe.reshape(1, CMP_LEN * HEAD_DIM).astype(BF16), (8, CMP_LEN * HEAD_DIM))
    full = lambda shp: pl.BlockSpec(shp, lambda g: (0,) * len(shp))
    return pl.pallas_call(
        functools.partial(_compress_kernel, norm=norm), grid=(NSA_KV_HEADS,),
        in_specs=[pl.BlockSpec((1, ncp, width), lambda g: (g, 0, 0)),
                  full((CMP_LEN * HEAD_DIM, CMP_HIDDEN)), full((8, CMP_LEN * HEAD_DIM)),
                  full((CMP_HIDDEN, HEAD_DIM)), full((1, HEAD_DIM))],
        out_specs=pl.BlockSpec((1, ncp, HEAD_DIM), lambda g: (g, 0, 0)),
        out_shape=jax.ShapeDtypeStruct((NSA_KV_HEADS, ncp, HEAD_DIM), BF16),
        compiler_params=_cparams(("parallel",)), name="compress")(
            chunks, w1f, pe8, w2.astype(BF16), gain.reshape(1, HEAD_DIM))


def _cmp_attn_kernel(tbl_ref, q_ref, kc_ref, vc_ref, ov_ref, oc_ref, sel_ref, s_ref, key_ref,
                     *, tq, ncp, ns, nsel, ww):
    i = pl.program_id(0)
    rep = NSA_HEADS // NSA_KV_HEADS
    qpos = i * tq + lax.broadcasted_iota(jnp.int32, (tq, ncp), 0)
    dist = qpos - (lax.broadcasted_iota(jnp.int32, (tq, ncp), 1) * CMP_STRIDE + CMP_LEN - 1)
    mask = dist >= 0
    w0 = jnp.maximum(i * tq - (CMP_LEN - 1) - FAR_DIST + CMP_STRIDE, 0) // (CMP_STRIDE * LANE) * LANE
    w0 = pl.multiple_of(jnp.minimum(w0, ncp - ww), LANE)
    dist_w = (i * tq + lax.broadcasted_iota(jnp.int32, (tq, ww), 0)
              - ((w0 + lax.broadcasted_iota(jnp.int32, (tq, ww), 1)) * CMP_STRIDE + CMP_LEN - 1))
    bucket = _bucket_of(dist_w)
    blk = lax.broadcasted_iota(jnp.int32, (tq, ns), 1)
    qp = i * tq + lax.broadcasted_iota(jnp.int32, (tq, ns), 0)
    cur = qp // SLC_LEN
    forced = (blk == 0) | (blk == cur) | (blk == cur - 1)
    admissible = blk * SLC_LEN <= qp
    for g in range(NSA_KV_HEADS):
        s_ref[...] = _dot_nt(q_ref[g * rep:(g + 1) * rep].reshape(rep * tq, HEAD_DIM), kc_ref[g])
        imp = jnp.zeros((tq, ncp), F32)
        ps = []
        for r in range(rep):
            h = g * rep + r
            trow = _bias_row(tbl_ref, h, tq)
            bias = [jnp.take_along_axis(trow, bucket[:, c * LANE:(c + 1) * LANE], axis=1)
                    for c in range(ww // LANE)]
            s_ref[r * tq:(r + 1) * tq, pl.ds(w0, ww)] += jnp.concatenate(bias, axis=1)
            sh = jnp.where(mask, s_ref[r * tq:(r + 1) * tq, :], NEG)
            m = jnp.max(sh, axis=-1, keepdims=True)
            p = jnp.where(mask, jnp.exp2(sh - m), 0.0)
            pc = p / jnp.maximum(jnp.sum(p, axis=-1, keepdims=True), 1e-30)
            imp = imp + pc
            ps.append(pc.astype(BF16))
        o = jnp.dot(jnp.concatenate(ps, axis=0), vc_ref[g], preferred_element_type=F32)
        for r in range(rep):
            h = g * rep + r
            oc_ref[:, h * HEAD_DIM:(h + 1) * HEAD_DIM] = o[r * tq:(r + 1) * tq]
        hi = imp.astype(BF16)
        lo = (imp - hi.astype(F32)).astype(BF16)
        impb = (jnp.dot(hi, ov_ref[...], preferred_element_type=F32)
                + jnp.dot(lo, ov_ref[...], preferred_element_type=F32))
        impb = jnp.where(forced, FORCE_SCORE, impb)
        key_ref[:, g * tq:(g + 1) * tq] = _to_key(jnp.where(admissible, impb, -jnp.inf).T)
    keys = key_ref[...]
    thr = _kth_key_val(keys, nsel, axis=0)
    above = jnp.where(keys > thr, 1.0, 0.0)
    at = jnp.where(keys == thr, 1.0, 0.0)
    need = nsel - jnp.sum(above, axis=0, keepdims=True)
    blk_t = lax.broadcasted_iota(jnp.int32, keys.shape, 0)

    def halve(_, st):
        lo_c, hi_c = st
        mid = (lo_c + hi_c) >> 1
        enough = jnp.sum(jnp.where(blk_t < mid, at, 0.0), axis=0, keepdims=True) >= need
        return jnp.where(enough, lo_c, mid), jnp.where(enough, mid, hi_c)

    row = (1, keys.shape[1])
    surplus = jnp.max(jnp.sum(at, axis=0, keepdims=True) - need)
    steps = jnp.where(surplus > 0.5, max(ns, 2).bit_length(), 0)
    _, cut = lax.fori_loop(0, steps, halve, (jnp.zeros(row, jnp.int32), jnp.full(row, ns, jnp.int32)))
    sel_t = above + jnp.where(blk_t < cut, at, 0.0)
    for g in range(NSA_KV_HEADS):
        sel_ref[g] = sel_t[:, g * tq:(g + 1) * tq].T.astype(BF16)


def _cmp_attn(table, q_h, kc, vc):
    nheads, t, _ = q_h.shape
    ncp = kc.shape[1]
    ns = t // SLC_LEN
    nsel = min(SLC_TOP, ns)
    tq = _pick(t, 128)
    ww = min(ncp, LANE * (-(-(FAR_DIST + tq + CMP_STRIDE * LANE) // (CMP_STRIDE * LANE))))
    rep = nheads // NSA_KV_HEADS
    c_start = np.arange(ncp)[:, None] * CMP_STRIDE
    s_start = np.arange(ns)[None, :] * SLC_LEN
    overlap = jnp.asarray((c_start < s_start + SLC_LEN) & (c_start + CMP_LEN > s_start), BF16)
    full = lambda shp: pl.BlockSpec(shp, lambda i: (0,) * len(shp))
    return pl.pallas_call(
        functools.partial(_cmp_attn_kernel, tq=tq, ncp=ncp, ns=ns, nsel=nsel, ww=ww), grid=(t // tq,),
        in_specs=[full((nheads, LANE)),
                  pl.BlockSpec((nheads, tq, HEAD_DIM), lambda i: (0, i, 0)),
                  full((NSA_KV_HEADS, ncp, HEAD_DIM)), full((NSA_KV_HEADS, ncp, HEAD_DIM)), full((ncp, ns))],
        out_specs=[pl.BlockSpec((tq, nheads * HEAD_DIM), lambda i: (i, 0)),
                   pl.BlockSpec((NSA_KV_HEADS, tq, ns), lambda i: (0, i, 0))],
        out_shape=[jax.ShapeDtypeStruct((t, nheads * HEAD_DIM), F32),
                   jax.ShapeDtypeStruct((NSA_KV_HEADS, t, ns), BF16)],
        scratch_shapes=[pltpu.VMEM((rep * tq, ncp), F32), pltpu.VMEM((ns, NSA_KV_HEADS * tq), jnp.int32)],
        compiler_params=_cparams(("parallel",)), name="cmp_attn")(_table_rows(table), q_h, kc, vc, overlap)


def _merge_kernel(ya_ref, yb_ref, wa_ref, wb_ref, ga_ref, gb_ref, o_ref):
    za = jnp.dot(ya_ref[...], wa_ref[...], preferred_element_type=F32)
    zb = jnp.dot(yb_ref[...], wb_ref[...], preferred_element_type=F32)
    o_ref[...] = (jax.nn.sigmoid(ga_ref[...]) * za + jax.nn.sigmoid(gb_ref[...]) * zb).astype(o_ref.dtype)


def _merge(y_a, y_b, w_a, w_b, pm):
    t, k = y_a.shape
    d = w_a.shape[1]
    tm, tn = _pick(t, 1024), _pick(d, 512)
    assert OFF_GM % tn == 0
    lhs = pl.BlockSpec((tm, k), lambda i, j: (i, 0))
    rhs = pl.BlockSpec((k, tn), lambda i, j: (0, j))
    return pl.pallas_call(
        _merge_kernel, grid=(t // tm, d // tn),
        in_specs=[lhs, lhs, rhs, rhs, pl.BlockSpec((tm, tn), lambda i, j: (i, OFF_GM // tn + j)),
                  pl.BlockSpec((tm, tn), lambda i, j: (i, (OFF_GM + d) // tn + j))],
        out_specs=pl.BlockSpec((tm, tn), lambda i, j: (i, j)), out_shape=jax.ShapeDtypeStruct((t, d), BF16),
        compiler_params=_cparams(("parallel", "arbitrary")), name="merge")(y_a, y_b, w_a, w_b, pm, pm)


_CAND_FULL = PEER_TOPK // 2


def _peer_cands(a1, a2, op):
    pieces = [op(a1[0:1], a2)]
    pieces += [op(a1[k:k + 1], a2[0:_CAND_FULL]) for k in range(1, _CAND_FULL)]
    pieces.append(op(a1[_CAND_FULL:], a2[0:1]))
    return jnp.concatenate(pieces, axis=0)


def _peer_score_kernel(q_ref, sk_ref, s1_o, e1_o, s2_o, e2_o, thr_o, *, tm):
    row = lax.broadcasted_iota(jnp.int32, (PEER_NKEYS, tm), 0).astype(F32)
    for h in range(PEER_HEADS):
        svals, tops = [], []
        for c in range(2):
            hc = 2 * h + c
            s = _dot_nt(sk_ref[hc], q_ref[:, hc * 128:(hc + 1) * 128].astype(BF16))
            svals.append(s)
            work, top = s, []
            for _ in range(PEER_TOPK):
                mx = jnp.max(work, axis=0, keepdims=True)
                first = jnp.min(jnp.where(work == mx, row, float(PEER_NKEYS)), axis=0, keepdims=True)
                work = jnp.where(row == first, -jnp.inf, work)
                top.append(mx)
            tops.append(jnp.concatenate(top, axis=0))
        a1, a2 = tops
        cand = _peer_cands(a1, a2, jnp.add)
        thr = _kth_largest(cand, PEER_TOPK, 0)
        m1, m2 = a1[0:1], a2[0:1]
        ec = _peer_cands(jnp.exp(a1 - m1), jnp.exp(a2 - m2), jnp.multiply)
        z = jnp.sum(jnp.where(cand >= thr, ec, 0.0), axis=0, keepdims=True)
        s1_o[h] = svals[0]
        s2_o[h] = svals[1]
        e1_o[h] = jnp.exp(svals[0] - m1)
        e2_o[h] = jnp.exp(svals[1] - m2) / z
        thr_o[h:h + 1, :] = thr


def _peer_score(qp, sub_keys):
    t = qp.shape[0]
    tm = _pick(t, 256)
    sk = sub_keys.reshape(2 * PEER_HEADS, PEER_NKEYS, PEER_QDIM // 2).astype(BF16)
    tr = pl.BlockSpec((PEER_HEADS, PEER_NKEYS, tm), lambda i: (0, 0, i))
    shp = jax.ShapeDtypeStruct((PEER_HEADS, PEER_NKEYS, t), F32)
    return pl.pallas_call(
        functools.partial(_peer_score_kernel, tm=tm), grid=(t // tm,),
        in_specs=[pl.BlockSpec((tm, 2 * PEER_HEADS * 128), lambda i: (i, 0)),
                  pl.BlockSpec((2 * PEER_HEADS, PEER_NKEYS, PEER_QDIM // 2), lambda i: (0, 0, 0))],
        out_specs=[tr, tr, tr, tr, pl.BlockSpec((PEER_HEADS, tm), lambda i: (0, i))],
        out_shape=[shp, shp, shp, shp, jax.ShapeDtypeStruct((PEER_HEADS, t), F32)],
        compiler_params=_cparams(("parallel",)), name="peer_score")(qp, sk)


def _peer_act_kernel(h_ref, u_ref, s1_ref, e1_ref, s2_ref, e2_ref, thr_ref, o_ref, w_ref, a_ref, *, tm, te, sub):
    j = pl.program_id(1)
    n1 = te // PEER_NKEYS
    a_ref[...] = _dot_nt(h_ref[...], u_ref[...])
    for ai in range(n1):
        i1 = j * n1 + ai
        s1rows = [s1_ref[h, pl.ds(i1, 1), :] for h in range(PEER_HEADS)]
        e1rows = [e1_ref[h, pl.ds(i1, 1), :] for h in range(PEER_HEADS)]
        for ts in range(tm // LANE):
            tok = slice(ts * LANE, (ts + 1) * LANE)
            wt = jnp.zeros((PEER_NKEYS, LANE), F32)
            for h in range(PEER_HEADS):
                hit = (s1rows[h][:, tok] + s2_ref[h, :, tok]) >= thr_ref[h:h + 1, tok]
                wt = wt + jnp.where(hit, e1rows[h][:, tok] * e2_ref[h, :, tok], 0.0)
            w_ref[tok, ai * PEER_NKEYS:(ai + 1) * PEER_NKEYS] = wt.T
    for r in range(tm // sub):
        rows = slice(r * sub, (r + 1) * sub)
        o_ref[rows, :] = (_gelu(a_ref[rows, :]) * w_ref[rows, :]).astype(o_ref.dtype)


def _peer_act(h2, u, s1, e1, s2, e2, thr):
    t, d = h2.shape
    ne = u.shape[0]
    tm, te, sub = _pick(t, 1024), 512, 256
    once = dict(pipeline_mode=pl.Buffered(1))
    tok = pl.BlockSpec((PEER_HEADS, PEER_NKEYS, tm), lambda i, j: (0, 0, i), **once)
    return pl.pallas_call(
        functools.partial(_peer_act_kernel, tm=tm, te=te, sub=sub), grid=(t // tm, ne // te),
        in_specs=[pl.BlockSpec((tm, d), lambda i, j: (i, 0), **once),
                  pl.BlockSpec((te, d), lambda i, j: (j, 0)),
                  tok, tok, tok, tok, pl.BlockSpec((PEER_HEADS, tm), lambda i, j: (0, i), **once)],
        out_specs=pl.BlockSpec((tm, te), lambda i, j: (i, j)),
        out_shape=jax.ShapeDtypeStruct((t, ne), BF16),
        scratch_shapes=[pltpu.VMEM((tm, te), F32), pltpu.VMEM((tm, te), F32)],
        compiler_params=_cparams(("parallel", "arbitrary")), name="peer_act")(h2, u, s1, e1, s2, e2, thr)


def _pack_w_in(w_in, d):
    offs = np.cumsum([0, DSA_HEADS * 128, DSA_KV_HEADS * 128, DSA_KV_HEADS * 128, IDX_HEADS * IDX_DIM, IDX_DIM,
                      IDX_HEADS, NSA_HEADS * 128, 6 * NSA_KV_HEADS * 128, 3 * NSA_HEADS, 2 * d])
    qa, ka, va, qi, ki, wi, qb, kvb, gb, gm = [w_in[:, int(offs[n]):int(offs[n + 1])] for n in range(10)]
    main = jnp.concatenate([qa, qi, qb, kvb, ka, va, gm], axis=1).astype(BF16)
    pad = lambda n: jnp.zeros((d, n), w_in.dtype)
    tail = jnp.concatenate([ki, wi, pad(128 - IDX_DIM - IDX_HEADS), gb, pad(128 - 3 * NSA_HEADS)], axis=1).astype(BF16)
    return main, tail


def _token_mixer(h, table, w_in, gq_a, gk_a, gq_b, gk_cmp, gk_slc, gk_win, cmp_pe_k, cmp_w1_k, cmp_w2_k,
                 cmp_pe_v, cmp_w1_v, cmp_w2_v, w_branch_a, w_branch_b):
    t, d = h.shape
    w_main, w_tail = _pack_w_in(w_in, d)
    pm = _matmul(h, w_main, tn=1024, name="proj_main")
    pt = _matmul(h, w_tail, name="proj_tail")
    qa_h, qi_h, qb_h, kvb_h, ka_h, va_h, ki, wi, gsig = _prep(pm, pt, gq_a, gk_a, gq_b, gk_slc, gk_win)
    tq, tk = _pick(t, 256), _pick(t, 512)
    keys, thr = _dsa_index(qi_h, ki.T, wi, min(DSA_TOPK, t // 4))
    bias_a = _bias_tiles(table[:, :DSA_HEADS], tq, tk)
    y_a = _flash("dsa", qa_h, ka_h, va_h, bias_a, (keys, thr), BF16)
    g = NSA_KV_HEADS
    kc = _compress(kvb_h, 0, cmp_pe_k, cmp_w1_k, cmp_w2_k, gk_cmp, True)
    vc = _compress(kvb_h, g, cmp_pe_v, cmp_w1_v, cmp_w2_v, gk_cmp, False)
    o_c, sel = _cmp_attn(table[:, DSA_HEADS:], qb_h, kc, vc)
    bias_b = _bias_tiles(table[:, DSA_HEADS:], tq, tk)
    ns = t // SLC_LEN
    expand = jnp.asarray(np.arange(ns)[:, None] == (np.arange(t)[None, :] // SLC_LEN), BF16)
    o_s = _flash("slc", qb_h, kvb_h[2 * g:3 * g], kvb_h[3 * g:4 * g], bias_b, (sel, expand), F32)
    y_b = _flash("win", qb_h, kvb_h[4 * g:5 * g], kvb_h[5 * g:6 * g], bias_b, (o_c, o_s, gsig), BF16)
    return _merge(y_a, y_b, w_branch_a.astype(BF16), w_branch_b.astype(BF16), pm)


def _peer_ffn(h2, x1, gt2, w_q, sub_keys, u, v):
    qp = _matmul(h2, w_q.astype(BF16), name="peer_q")
    s1, e1, s2, e2, thr = _peer_score(qp, sub_keys)
    act = _peer_act(h2, u.astype(BF16), s1, e1, s2, e2, thr)
    return _matmul_resid(act, v.astype(BF16), x1, gt2, name="peer_out")


def kernel(x, c, rel_bias, w_ada, b_ada, g_mix, w_in, gq_a, gk_a, gq_b, gk_cmp, gk_slc, gk_win, cmp_pe_k, cmp_w1_k,
           cmp_w2_k, cmp_pe_v, cmp_w1_v, cmp_w2_v, w_branch_a, w_branch_b, w_out, g_ffn, w_peer_q, peer_sub_keys,
           peer_u, peer_v):
    bsz, t, d = x.shape
    assert bsz == 1 and t % 512 == 0 and d % 128 == 0
    xs = x[0]
    for i in range(w_ada.shape[0]):
        mod = _ada(c, w_ada[i], b_ada[i])
        sh1, sc1, gt1, sh2, sc2, gt2 = [mod[:, n * d:(n + 1) * d] for n in range(6)]
        h = _normmod(xs, g_mix[i], sc1, sh1)
        merged = _token_mixer(h, rel_bias, w_in[i], gq_a[i], gk_a[i], gq_b[i], gk_cmp[i], gk_slc[i], gk_win[i],
                              cmp_pe_k[i], cmp_w1_k[i], cmp_w2_k[i], cmp_pe_v[i], cmp_w1_v[i], cmp_w2_v[i],
                              w_branch_a[i], w_branch_b[i])
        x1 = _matmul_resid(merged, w_out[i].astype(BF16), xs, gt1, name="w_out")
        h2 = _normmod(x1, g_ffn[i], sc2, sh2)
        xs = _peer_ffn(h2, x1, gt2, w_peer_q[i], peer_sub_keys[i], peer_u[i], peer_v[i])
    return xs[None]
```

```python
import functools
import math

import numpy as np
import jax
import jax.numpy as jnp
from jax import lax
from jax.experimental import pallas as pl
from jax.experimental.pallas import tpu as pltpu

HEAD_DIM = 128
DSA_HEADS = 16
DSA_KV_HEADS = 4
IDX_HEADS = 32
IDX_DIM = 64
DSA_TOPK = 256
NSA_HEADS = 16
NSA_KV_HEADS = 4
CMP_LEN = 32
CMP_STRIDE = 16
CMP_HIDDEN = 256
SLC_LEN = 64
SLC_TOP = 16
WIN = 512
FORCE_SCORE = 1e9
PEER_HEADS = 8
PEER_NKEYS = 128
PEER_QDIM = 256
PEER_TOPK = 16
REL_BUCKETS = 32
REL_MAX_DIST = 2048
EPS = 1e-6

LANE = 128
VMEM_LIMIT = 56 * 1024 * 1024
NEG = -1e30
LOG2E = math.log2(math.e)
INT_MIN = -2 ** 31
KEY_NEG_INF = -2 ** 31 + 0x007FFFFF

OFF_QA, OFF_QI, OFF_QB, OFF_KVB, OFF_KA, OFF_VA, OFF_GM = 0, 2048, 4096, 6144, 9216, 9728, 10240

F32 = jnp.float32
BF16 = jnp.bfloat16


def _bucket_starts():
    n = np.arange(2 * REL_MAX_DIST)
    exact = REL_BUCKETS // 2
    nf = np.maximum(n, 1).astype(np.float32)
    lb = exact + (np.log(nf / np.float32(exact)) / np.float32(math.log(REL_MAX_DIST / exact))
                  * np.float32(REL_BUCKETS - exact)).astype(np.int32)
    bucket = np.where(n < exact, n, np.minimum(lb, REL_BUCKETS - 1))
    return [int(np.argmax(bucket >= b)) for b in range(REL_BUCKETS)]


BUCKET_STARTS = _bucket_starts()
FAR_DIST = BUCKET_STARTS[REL_BUCKETS - 1]


def _cparams(sem):
    return pltpu.CompilerParams(dimension_semantics=sem, vmem_limit_bytes=VMEM_LIMIT)


def _pick(n, pref):
    t = pref
    while n % t:
        t //= 2
    return t


def _to_key(x):
    b = lax.bitcast_convert_type(x, jnp.int32)
    return b ^ ((b >> 31) & 0x7FFFFFFF)


def _from_key(k):
    return lax.bitcast_convert_type(k ^ ((k >> 31) & 0x7FFFFFFF), F32)


def _kth_key(count_ge, shape, kk):
    def bit_pass(b, x):
        cand = x + lax.shift_left(jnp.int32(1), 31 - b)
        return jnp.where(count_ge(cand) >= kk, cand, x)
    x = lax.fori_loop(0, 32, bit_pass, jnp.full(shape, INT_MIN, jnp.int32))
    return jnp.maximum(x, KEY_NEG_INF)


def _kth_key_bracket(count_ge, kmin, kmax, kk):
    c_min = count_ge(kmin)
    few = c_min < kk
    lo0 = jnp.where(few, KEY_NEG_INF, kmin)
    hi0 = kmax + 1
    done0 = jnp.where(few | (c_min == kk) | (hi0 - 1 <= lo0), 1.0, 0.0)

    def cond(st):
        p, _, _, _, _, done = st
        return jnp.logical_and(p < 72, jnp.min(done) < 0.5)

    def body(st):
        p, lo, hi, clo, chi, done = st
        mid = (lo >> 1) + (hi >> 1) + (lo & hi & 1)
        flo, fhi = _from_key(lo), _from_key(hi)
        frac = (jnp.log(clo) - math.log(kk)) / (jnp.log(clo) - jnp.log(jnp.maximum(chi, 0.5)))
        cand = jnp.where(p % 2 == 1, mid, _to_key(flo + frac * (fhi - flo)))
        cand = jnp.minimum(jnp.maximum(cand, lo + 1), hi - 1)
        active = done < 0.5
        cand = jnp.where(active, cand, lo)
        cnt = count_ge(cand)
        up = active & (cnt >= kk)
        dn = active & (cnt < kk)
        lo, clo = jnp.where(up, cand, lo), jnp.where(up, cnt, clo)
        hi, chi = jnp.where(dn, cand, hi), jnp.where(dn, cnt, chi)
        done = jnp.where((clo == kk) | (hi - 1 <= lo), 1.0, done)
        return p + 1, lo, hi, clo, chi, done

    st = lax.while_loop(cond, body, (jnp.int32(0), lo0, hi0, c_min, jnp.zeros_like(c_min), done0))
    return st[1], st[3]


def _kth_largest(vals, kk, axis):
    shape = list(vals.shape)
    shape[axis] = 1

    def body(_, st):
        work, taken, thr = st
        mx = jnp.max(work, axis=axis, keepdims=True)
        hit = work == mx
        thr = jnp.where(taken < kk, mx, thr)
        taken = taken + jnp.sum(jnp.where(hit, 1.0, 0.0), axis=axis, keepdims=True)
        return jnp.where(hit, -jnp.inf, work), taken, thr

    st = (vals, jnp.zeros(shape, F32), jnp.full(shape, -jnp.inf, F32))
    return lax.fori_loop(0, kk, body, st)[2]


def _kth_key_val(keys, kk, axis=-1):
    def count_ge(cand):
        return jnp.sum(jnp.where(keys >= cand, 1.0, 0.0), axis=axis, keepdims=True)
    shape = list(keys.shape)
    shape[axis] = 1
    return _kth_key(count_ge, tuple(shape), float(kk))


def _gelu(x):
    return 0.5 * x * (1.0 + jnp.tanh(math.sqrt(2.0 / math.pi) * (x + 0.044715 * (x * x * x))))


def _dot_nt(a, b):
    return lax.dot_general(a, b, (((1,), (1,)), ((), ())), preferred_element_type=F32)


def _head_norm(x, g, scale):
    ms = jnp.mean(x * x, axis=-1, keepdims=True)
    return x * lax.rsqrt(ms + EPS) * g * scale


def _mm_kernel(a_ref, b_ref, o_ref):
    o_ref[...] = jnp.dot(a_ref[...], b_ref[...], preferred_element_type=F32).astype(o_ref.dtype)


def _matmul(a, b, out_dtype=F32, tm=1024, tn=512, name="matmul"):
    m, k = a.shape
    n = b.shape[1]
    tm, tn = _pick(m, tm), _pick(n, tn)
    return pl.pallas_call(
        _mm_kernel, grid=(m // tm, n // tn),
        in_specs=[pl.BlockSpec((tm, k), lambda i, j: (i, 0)), pl.BlockSpec((k, tn), lambda i, j: (0, j))],
        out_specs=pl.BlockSpec((tm, tn), lambda i, j: (i, j)),
        out_shape=jax.ShapeDtypeStruct((m, n), out_dtype),
        compiler_params=_cparams(("parallel", "arbitrary")), name=name)(a, b)


def _mm_resid_kernel(a_ref, b_ref, x_ref, gt_ref, o_ref):
    k = pl.program_id(2)
    part = jnp.dot(a_ref[...], b_ref[...], preferred_element_type=F32)

    @pl.when(k == 0)
    def _():
        o_ref[...] = part

    @pl.when(k > 0)
    def _():
        o_ref[...] += part

    @pl.when(k == pl.num_programs(2) - 1)
    def _():
        o_ref[...] = x_ref[...] + gt_ref[...] * o_ref[...]


def _matmul_resid(a, b, x, gt, name):
    m, kd = a.shape
    n = b.shape[1]
    tm, tn, tk = _pick(m, 1024), _pick(n, 1024), _pick(kd, 4096)
    return pl.pallas_call(
        _mm_resid_kernel, grid=(m // tm, n // tn, kd // tk),
        in_specs=[pl.BlockSpec((tm, tk), lambda i, j, k: (i, k)), pl.BlockSpec((tk, tn), lambda i, j, k: (k, j)),
                  pl.BlockSpec((tm, tn), lambda i, j, k: (i, j), pipeline_mode=pl.Buffered(1)),
                  pl.BlockSpec((1, tn), lambda i, j, k: (0, j))],
        out_specs=pl.BlockSpec((tm, tn), lambda i, j, k: (i, j)),
        out_shape=jax.ShapeDtypeStruct((m, n), F32),
        compiler_params=_cparams(("parallel", "parallel", "arbitrary")), name=name)(a, b, x, gt)


def _ada_kernel(c_ref, w_ref, b_ref, o_ref):
    o_ref[...] = jnp.dot(c_ref[...], w_ref[...].astype(BF16), preferred_element_type=F32) + b_ref[...]


def _ada(c, w, b):
    d, n = w.shape
    tn = _pick(n, 512)
    c8 = jnp.broadcast_to(c.astype(BF16), (8, d))
    out = pl.pallas_call(
        _ada_kernel, grid=(n // tn,),
        in_specs=[pl.BlockSpec((8, d), lambda j: (0, 0)), pl.BlockSpec((d, tn), lambda j: (0, j)),
                  pl.BlockSpec((1, tn), lambda j: (0, j))],
        out_specs=pl.BlockSpec((8, tn), lambda j: (0, j)),
        out_shape=jax.ShapeDtypeStruct((8, n), F32),
        compiler_params=_cparams(("arbitrary",)), name="ada")(c8, w, b.reshape(1, n))
    return out[0:1]


def _normmod_kernel(x_ref, g_ref, sc_ref, sh_ref, o_ref):
    x = x_ref[...]
    y = x * lax.rsqrt(jnp.mean(x * x, axis=-1, keepdims=True) + EPS) * g_ref[...]
    o_ref[...] = (y * (1.0 + sc_ref[...]) + sh_ref[...]).astype(o_ref.dtype)


def _normmod(x, g, sc, sh):
    t, d = x.shape
    tm = _pick(t, 256)
    row = pl.BlockSpec((1, d), lambda i: (0, 0))
    return pl.pallas_call(
        _normmod_kernel, grid=(t // tm,),
        in_specs=[pl.BlockSpec((tm, d), lambda i: (i, 0)), row, row, row],
        out_specs=pl.BlockSpec((tm, d), lambda i: (i, 0)),
        out_shape=jax.ShapeDtypeStruct((t, d), BF16),
        compiler_params=_cparams(("parallel",)), name="normmod")(x, g.reshape(1, d), sc, sh)


def _prep_kernel(qa_ref, qi_ref, qb_ref, kvb_ref, ka_ref, va_ref, tail_ref,
                 gqa_ref, gka_ref, gqb_ref, gslc_ref, gwin_ref,
                 qa_o, qi_o, qb_o, kvb_o, ka_o, va_o, ki_o, wi_o, gs_o):
    qscale = HEAD_DIM ** -0.5 * LOG2E
    for h in range(DSA_HEADS):
        qa_o[h] = _head_norm(qa_ref[:, h * 128:(h + 1) * 128], gqa_ref[...], qscale).astype(BF16)
    for h in range(NSA_HEADS):
        qb_o[h] = _head_norm(qb_ref[:, h * 128:(h + 1) * 128], gqb_ref[...], qscale).astype(BF16)
    for h in range(IDX_HEADS):
        qi_o[h] = (qi_ref[:, h * IDX_DIM:(h + 1) * IDX_DIM] * IDX_DIM ** -0.5).astype(BF16)
    for g in range(DSA_KV_HEADS):
        ka_o[g] = _head_norm(ka_ref[:, g * 128:(g + 1) * 128], gka_ref[...], 1.0).astype(BF16)
        va_o[g] = va_ref[:, g * 128:(g + 1) * 128].astype(BF16)
    for s in range(6 * NSA_KV_HEADS):
        xs = kvb_ref[:, s * 128:(s + 1) * 128]
        part = s // NSA_KV_HEADS
        if part == 2:
            xs = _head_norm(xs, gslc_ref[...], 1.0)
        elif part == 4:
            xs = _head_norm(xs, gwin_ref[...], 1.0)
        kvb_o[s] = xs.astype(BF16)
    tail = tail_ref[...]
    ki_o[...] = tail[:, 0:IDX_DIM].astype(BF16)
    wi_o[...] = tail[:, 0:128] * IDX_HEADS ** -0.5
    gs_o[...] = jax.nn.sigmoid(tail[:, 128:256])


def _prep(pm, pt, gq_a, gk_a, gq_b, gk_slc, gk_win):
    t = pm.shape[0]
    tm = _pick(t, 128)
    g128 = pl.BlockSpec((1, 128), lambda i: (0, 0))

    def col(w, off):
        assert off % w == 0
        return pl.BlockSpec((tm, w), lambda i: (i, off // w))

    def hm(n, w):
        return pl.BlockSpec((n, tm, w), lambda i: (0, i, 0))

    row = lambda w: pl.BlockSpec((tm, w), lambda i: (i, 0))
    outs = pl.pallas_call(
        _prep_kernel, grid=(t // tm,),
        in_specs=[col(2048, OFF_QA), col(2048, OFF_QI), col(2048, OFF_QB), col(3072, OFF_KVB),
                  col(512, OFF_KA), col(512, OFF_VA), row(256), g128, g128, g128, g128, g128],
        out_specs=[hm(DSA_HEADS, 128), hm(IDX_HEADS, IDX_DIM), hm(NSA_HEADS, 128), hm(24, 128),
                   hm(DSA_KV_HEADS, 128), hm(DSA_KV_HEADS, 128), row(IDX_DIM), row(128), row(128)],
        out_shape=[jax.ShapeDtypeStruct((DSA_HEADS, t, 128), BF16),
                   jax.ShapeDtypeStruct((IDX_HEADS, t, IDX_DIM), BF16),
                   jax.ShapeDtypeStruct((NSA_HEADS, t, 128), BF16),
                   jax.ShapeDtypeStruct((24, t, 128), BF16),
                   jax.ShapeDtypeStruct((DSA_KV_HEADS, t, 128), BF16),
                   jax.ShapeDtypeStruct((DSA_KV_HEADS, t, 128), BF16),
                   jax.ShapeDtypeStruct((t, IDX_DIM), BF16),
                   jax.ShapeDtypeStruct((t, 128), F32),
                   jax.ShapeDtypeStruct((t, 128), F32)],
        compiler_params=_cparams(("parallel",)), name="prep")(
            pm, pm, pm, pm, pm, pm, pt,
            gq_a.reshape(1, 128), gk_a.reshape(1, 128), gq_b.reshape(1, 128),
            gk_slc.reshape(1, 128), gk_win.reshape(1, 128))
    return outs


def _bucket_of(dist):
    bucket = jnp.zeros(dist.shape, jnp.int32)
    for b in range(1, REL_BUCKETS):
        bucket = jnp.where(dist >= BUCKET_STARTS[b], b, bucket)
    return bucket


def _bias_row(tbl_ref, h, rows):
    trow = (tbl_ref[h:h + 1, :] - tbl_ref[h:h + 1, REL_BUCKETS - 1:REL_BUCKETS]) * LOG2E
    return jnp.broadcast_to(trow, (rows, LANE))


def _bias_tiles_kernel(tbl_ref, o_ref, *, tq, tk, nheads):
    d = pl.program_id(0)
    dist = d * tq + lax.broadcasted_iota(jnp.int32, (tq, tk), 0) - lax.broadcasted_iota(jnp.int32, (tq, tk), 1)
    bucket = _bucket_of(dist)
    for h in range(nheads):
        trow = _bias_row(tbl_ref, h, tq)
        o_ref[h, 0] = jnp.concatenate(
            [jnp.take_along_axis(trow, bucket[:, c * LANE:(c + 1) * LANE], axis=1) for c in range(tk // LANE)], axis=1)


def _n_near(tq, tk):
    return -(-(FAR_DIST + tk - 1) // tq)


def _bias_tiles(table, tq, tk):
    nheads = table.shape[1]
    nd = _n_near(tq, tk)
    return pl.pallas_call(
        functools.partial(_bias_tiles_kernel, tq=tq, tk=tk, nheads=nheads), grid=(nd,),
        in_specs=[pl.BlockSpec((nheads, LANE), lambda d: (0, 0))],
        out_specs=pl.BlockSpec((nheads, 1, tq, tk), lambda d: (0, d, 0, 0)),
        out_shape=jax.ShapeDtypeStruct((nheads, nd, tq, tk), F32),
        compiler_params=_cparams(("parallel",)), name="bias_tiles")(_table_rows(table))


def _table_rows(table):
    return jnp.pad(table.T, ((0, 0), (0, LANE - REL_BUCKETS)))


def _idx_kernel(q_ref, kt_ref, w_ref, key_ref, thr_ref, s_ref, wb_ref, *, tq, tk, ts, t, kk):
    i = pl.program_id(0)
    nkt = (i * tq + tq + tk - 1) // tk
    w = w_ref[...]
    for h in range(IDX_HEADS):
        wb_ref[h] = jnp.broadcast_to(w[:, IDX_DIM + h:IDX_DIM + h + 1], (tq, LANE))
    qall = q_ref[...].reshape(IDX_HEADS * tq, IDX_DIM)
    rows = lax.broadcasted_iota(jnp.int32, (tq, ts), 0)
    cols = lax.broadcasted_iota(jnp.int32, (tq, ts), 1)

    def tile(j, carry):
        offs = [pl.multiple_of(j * tk + u * ts, ts) for u in range(tk // ts)]
        for u, off in enumerate(offs):
            s_ref[u] = jnp.dot(qall, kt_ref[:, pl.ds(off, ts)], preferred_element_type=F32)
        for u, off in enumerate(offs):
            acc = jnp.zeros((tq, ts), F32)
            for h in range(IDX_HEADS):
                wbh = jnp.concatenate([wb_ref[h]] * (ts // LANE), axis=1)
                acc = acc + jnp.maximum(s_ref[u, h * tq:(h + 1) * tq, :], 0.0) * wbh
            acc = jnp.where(off + cols <= i * tq + rows, acc, -jnp.inf)
            key_ref[:, pl.ds(off, ts)] = _to_key(acc)
        return carry

    lax.fori_loop(0, nkt, tile, 0)

    def fill(j, carry):
        key_ref[:, pl.ds(pl.multiple_of(j * tk, tk), tk)] = jnp.full((tq, tk), KEY_NEG_INF, jnp.int32)
        return carry

    lax.fori_loop(nkt, t // tk, fill, 0)

    def count_ge(cand):
        def body(j, c):
            kt = key_ref[:, pl.ds(pl.multiple_of(j * tk, tk), tk)]
            ge = jnp.where(kt >= cand, 1.0, 0.0)
            for a in range(tk // LANE):
                c = c + ge[:, a * LANE:(a + 1) * LANE]
            return c
        c = lax.fori_loop(0, nkt, body, jnp.zeros((tq, LANE), F32))
        return jnp.sum(c, axis=-1, keepdims=True)

    def score_range(j, c):
        smin, smax = c
        st = _from_key(key_ref[:, pl.ds(pl.multiple_of(j * tk, tk), tk)])
        lo_t = jnp.where(st == -jnp.inf, jnp.inf, st)
        for a in range(tk // LANE):
            smin = jnp.minimum(smin, lo_t[:, a * LANE:(a + 1) * LANE])
            smax = jnp.maximum(smax, st[:, a * LANE:(a + 1) * LANE])
        return smin, smax

    smin, smax = lax.fori_loop(0, nkt, score_range, (jnp.full((tq, LANE), jnp.inf, F32),
                                                     jnp.full((tq, LANE), -jnp.inf, F32)))
    kmin = _to_key(jnp.min(smin, axis=-1, keepdims=True))
    kmax = _to_key(jnp.max(smax, axis=-1, keepdims=True))
    thr, reach = _kth_key_bracket(count_ge, kmin, kmax, float(kk))
    thr_ref[...] = thr
    tied = reach > kk

    @pl.when(jnp.max(jnp.where(tied, 1.0, 0.0)) > 0.5)
    def _():
        need = kk - count_ge(thr + 1)

        def count_tied_before(cut):
            def body(j, c):
                off = pl.multiple_of(j * tk, tk)
                kt = key_ref[:, pl.ds(off, tk)]
                pos = off + lax.broadcasted_iota(jnp.int32, (tq, tk), 1)
                m = jnp.where(kt == thr, jnp.where(pos < cut, 1.0, 0.0), 0.0)
                for a in range(tk // LANE):
                    c = c + m[:, a * LANE:(a + 1) * LANE]
                return c
            c = lax.fori_loop(0, nkt, body, jnp.zeros((tq, LANE), F32))
            return jnp.sum(c, axis=-1, keepdims=True)

        def halve(_, st):
            lo_c, hi_c = st
            mid = (lo_c + hi_c) >> 1
            enough = count_tied_before(mid) >= need
            return jnp.where(enough, lo_c, mid), jnp.where(enough, mid, hi_c)

        steps = max(t, 2).bit_length()
        _, cut = lax.fori_loop(0, steps, halve, (jnp.zeros((tq, 1), jnp.int32), jnp.full((tq, 1), t, jnp.int32)))

        def demote(j, carry):
            off = pl.multiple_of(j * tk, tk)
            kt = key_ref[:, pl.ds(off, tk)]
            pos = off + lax.broadcasted_iota(jnp.int32, (tq, tk), 1)
            late = jnp.where(tied, jnp.where(kt == thr, jnp.where(pos >= cut, 1.0, 0.0), 0.0), 0.0)
            key_ref[:, pl.ds(off, tk)] = jnp.where(late > 0.5, thr - 1, kt)
            return carry

        lax.fori_loop(0, nkt, demote, 0)


def _dsa_index(qi_h, ki_t, wi, kk):
    t = ki_t.shape[1]
    tq, tk = _pick(t, 128), _pick(t, 512)
    ts = tk // 2
    return pl.pallas_call(
        functools.partial(_idx_kernel, tq=tq, tk=tk, ts=ts, t=t, kk=kk), grid=(t // tq,),
        in_specs=[pl.BlockSpec((IDX_HEADS, tq, IDX_DIM), lambda i: (0, i, 0)),
                  pl.BlockSpec((IDX_DIM, t), lambda i: (0, 0)),
                  pl.BlockSpec((tq, 128), lambda i: (i, 0))],
        out_specs=[pl.BlockSpec((tq, t), lambda i: (i, 0)), pl.BlockSpec((tq, 1), lambda i: (i, 0))],
        out_shape=[jax.ShapeDtypeStruct((t, t), jnp.int32), jax.ShapeDtypeStruct((t, 1), jnp.int32)],
        scratch_shapes=[pltpu.VMEM((tk // ts, IDX_HEADS * tq, ts), F32), pltpu.VMEM((IDX_HEADS, tq, LANE), F32)],
        compiler_params=_cparams(("parallel",)), name="dsa_index")(qi_h, ki_t, wi)


def _first_tile(mode, i, tq, tk):
    if mode != "win":
        return 0
    start = i * tq - (WIN - 1)
    return (max(start, 0) if isinstance(i, int) else jnp.maximum(start, 0)) // tk


def _last_tile(i, tq, tk):
    return (i * tq + tq - 1) // tk


def _flash_kernel(qi_ref, kj_ref, *refs, mode, tq, tk, nheads, ngroups, nnear):
    refs, (m_ref, l_ref, acc_ref, s_ref, p_ref, madd_ref) = refs[:-6], refs[-6:]
    if mode == "dsa":
        q_ref, k_ref, v_ref, b_ref, key_ref, thr_ref, o_ref = refs
    elif mode == "slc":
        q_ref, k_ref, v_ref, b_ref, sel_ref, e_ref, o_ref = refs
    else:
        q_ref, k_ref, v_ref, b_ref, oc_ref, os_ref, g_ref, o_ref = refs
    rep = nheads // ngroups
    nch = tk // LANE
    rblk = min(tq, 64)
    s = pl.program_id(0)
    i, j = qi_ref[s], kj_ref[s]
    jlast = _last_tile(i, tq, tk)
    d = i - (tk // tq) * j

    @pl.when(j == _first_tile(mode, i, tq, tk))
    def _():
        m_ref[...] = jnp.full(m_ref.shape, NEG, F32)
        l_ref[...] = jnp.zeros(l_ref.shape, F32)
        acc_ref[...] = jnp.zeros(acc_ref.shape, F32)

    def step(near):
        if near:
            dist = ((i * tq - j * tk) + lax.broadcasted_iota(jnp.int32, (tq, tk), 0)
                    - lax.broadcasted_iota(jnp.int32, (tq, tk), 1))
            causal = dist >= 0
            if mode == "win":
                causal = causal & (dist < WIN)
        if mode == "dsa":
            hit = key_ref[...] >= thr_ref[...]
            madd_ref[...] = jnp.where((hit & causal) if near else hit, 0.0, NEG)
        elif mode == "win":
            madd_ref[...] = jnp.where(causal, 0.0, NEG)
        else:
            hit = jnp.dot(sel_ref[...].reshape(ngroups * tq, sel_ref.shape[2]), e_ref[...],
                          preferred_element_type=F32) > 0.5
            for g in range(ngroups):
                hg = hit[g * tq:(g + 1) * tq]
                madd_ref[g * tq:(g + 1) * tq, :] = jnp.where((hg & causal) if near else hg, 0.0, NEG)
        for g in range(ngroups):
            s_ref[...] = _dot_nt(q_ref[g * rep:(g + 1) * rep].reshape(rep * tq, HEAD_DIM), k_ref[g])
            mrow = g * tq if mode == "slc" else 0
            for r in range(rep):
                h = g * rep + r
                for rb in range(tq // rblk):
                    rows = slice(rb * rblk, (rb + 1) * rblk)
                    srows = slice(r * tq + rb * rblk, r * tq + (rb + 1) * rblk)
                    sh = s_ref[srows, :] + madd_ref[mrow + rb * rblk:mrow + (rb + 1) * rblk, :]
                    if near:
                        sh = sh + b_ref[h, 0, rows, :]
                    chunks = [sh[:, c * LANE:(c + 1) * LANE] for c in range(nch)]
                    m_old = m_ref[h, rows, :]
                    tile_max = jnp.max(functools.reduce(jnp.maximum, chunks), axis=-1, keepdims=True)
                    m_new = jnp.maximum(m_old, tile_max)
                    alpha = jnp.exp2(m_old - m_new)
                    pcs = [jnp.exp2(c - m_new) for c in chunks]
                    l_ref[h, rows, :] = alpha * l_ref[h, rows, :] + functools.reduce(jnp.add, pcs)
                    m_ref[h, rows, :] = m_new
                    acc_ref[h, rows, :] = acc_ref[h, rows, :] * alpha
                    p_ref[srows, :] = jnp.concatenate(pcs, axis=1).astype(BF16)
            pv = jnp.dot(p_ref[...], v_ref[g], preferred_element_type=F32)
            for r in range(rep):
                h = g * rep + r
                acc_ref[h] = acc_ref[h] + pv[r * tq:(r + 1) * tq]

    if mode == "win":
        step(True)
    else:
        pl.when(d < nnear)(lambda: step(True))
        pl.when(d >= nnear)(lambda: step(False))

    @pl.when(j == jlast)
    def _():
        for h in range(nheads):
            cols = slice(h * HEAD_DIM, (h + 1) * HEAD_DIM)
            out = acc_ref[h] / jnp.sum(l_ref[h], axis=-1, keepdims=True)
            if mode == "win":
                g = g_ref[...]
                out = (g[:, 3 * h:3 * h + 1] * oc_ref[:, cols] + g[:, 3 * h + 1:3 * h + 2] * os_ref[:, cols]
                       + g[:, 3 * h + 2:3 * h + 3] * out)
            o_ref[:, cols] = out.astype(o_ref.dtype)


def _flash(mode, q_h, k_h, v_h, bias, extra, out_dtype):
    nheads, t, _ = q_h.shape
    ngroups = k_h.shape[0]
    rep = nheads // ngroups
    _, nnear, tq, tk = bias.shape
    nq = t // tq
    pairs = [(i, j) for i in range(nq) for j in range(_first_tile(mode, i, tq, tk), _last_tile(i, tq, tk) + 1)]
    qi = jnp.asarray([p[0] for p in pairs], jnp.int32)
    kj = jnp.asarray([p[1] for p in pairs], jnp.int32)

    def bmap(s, qi, kj):
        return jnp.minimum(qi[s] - (tk // tq) * kj[s], nnear - 1)

    in_specs = [pl.BlockSpec((nheads, tq, HEAD_DIM), lambda s, qi, kj: (0, qi[s], 0)),
                pl.BlockSpec((ngroups, tk, HEAD_DIM), lambda s, qi, kj: (0, kj[s], 0)),
                pl.BlockSpec((ngroups, tk, HEAD_DIM), lambda s, qi, kj: (0, kj[s], 0)),
                pl.BlockSpec((nheads, 1, tq, tk), lambda s, qi, kj: (0, bmap(s, qi, kj), 0, 0))]
    if mode == "dsa":
        in_specs += [pl.BlockSpec((tq, tk), lambda s, qi, kj: (qi[s], kj[s])),
                     pl.BlockSpec((tq, 1), lambda s, qi, kj: (qi[s], 0))]
    elif mode == "slc":
        ns = extra[0].shape[2]
        in_specs += [pl.BlockSpec((ngroups, tq, ns), lambda s, qi, kj: (0, qi[s], 0)),
                     pl.BlockSpec((ns, tk), lambda s, qi, kj: (0, kj[s]))]
    else:
        row = lambda w: pl.BlockSpec((tq, w), lambda s, qi, kj: (qi[s], 0))
        in_specs += [row(nheads * HEAD_DIM), row(nheads * HEAD_DIM), row(LANE)]
    grid_spec = pltpu.PrefetchScalarGridSpec(
        num_scalar_prefetch=2, grid=(len(pairs),), in_specs=in_specs,
        out_specs=pl.BlockSpec((tq, nheads * HEAD_DIM), lambda s, qi, kj: (qi[s], 0)),
        scratch_shapes=[pltpu.VMEM((nheads, tq, LANE), F32), pltpu.VMEM((nheads, tq, LANE), F32),
                        pltpu.VMEM((nheads, tq, HEAD_DIM), F32),
                        pltpu.VMEM((rep * tq, tk), F32), pltpu.VMEM((rep * tq, tk), BF16),
                        pltpu.VMEM(((ngroups if mode == "slc" else 1) * tq, tk), F32)])
    return pl.pallas_call(
        functools.partial(_flash_kernel, mode=mode, tq=tq, tk=tk, nheads=nheads, ngroups=ngroups, nnear=nnear),
        grid_spec=grid_spec, out_shape=jax.ShapeDtypeStruct((t, nheads * HEAD_DIM), out_dtype),
        compiler_params=_cparams(("arbitrary",)), name="flash_" + mode)(qi, kj, q_h, k_h, v_h, bias, *extra)


def _compress_kernel(c_ref, w1_ref, pe_ref, w2_ref, g_ref, o_ref, *, norm):
    c = c_ref[0]
    half = CMP_STRIDE * HEAD_DIM
    ncp = c.shape[0]
    a = jnp.dot(c, w1_ref[0:half], preferred_element_type=F32)
    b = jnp.dot(c, w1_ref[half:2 * half], preferred_element_type=F32)
    pet = jnp.dot(pe_ref[...], w1_ref[...], preferred_element_type=F32)[0:1]
    hid = _gelu(a + pltpu.roll(b, ncp - 1, 0) + pet)
    y = jnp.dot(hid.astype(BF16), w2_ref[...], preferred_element_type=F32)
    if norm:
        y = _head_norm(y, g_ref[...], 1.0)
    o_ref[0] = y.astype(BF16)


def _compress(slabs, base, pe, w1, w2, gain, norm):
    _, t, _ = slabs.shape
    ncp = t // CMP_STRIDE
    width = CMP_STRIDE * HEAD_DIM
    chunks = slabs[base:base + NSA_KV_HEADS].reshape(NSA_KV_HEADS, ncp, width)
    w1f = w1.reshape(CMP_LEN * HEAD_DIM, CMP_HIDDEN).astype(BF16)
    pe8 = jnp.broadcast_to(pe.reshape(1, CMP_LEN * HEAD_DIM).astype(BF16), (8, CMP_LEN * HEAD_DIM))
    full = lambda shp: pl.BlockSpec(shp, lambda g: (0,) * len(shp))
    return pl.pallas_call(
        functools.partial(_compress_kernel, norm=norm), grid=(NSA_KV_HEADS,),
        in_specs=[pl.BlockSpec((1, ncp, width), lambda g: (g, 0, 0)),
                  full((CMP_LEN * HEAD_DIM, CMP_HIDDEN)), full((8, CMP_LEN * HEAD_DIM)),
                  full((CMP_HIDDEN, HEAD_DIM)), full((1, HEAD_DIM))],
        out_specs=pl.BlockSpec((1, ncp, HEAD_DIM), lambda g: (g, 0, 0)),
        out_shape=jax.ShapeDtypeStruct((NSA_KV_HEADS, ncp, HEAD_DIM), BF16),
        compiler_params=_cparams(("parallel",)), name="compress")(
            chunks, w1f, pe8, w2.astype(BF16), gain.reshape(1, HEAD_DIM))


def _cmp_attn_kernel(tbl_ref, q_ref, kc_ref, vc_ref, ov_ref, oc_ref, sel_ref, s_ref, key_ref,
                     *, tq, ncp, ns, nsel, ww):
    i = pl.program_id(0)
    rep = NSA_HEADS // NSA_KV_HEADS
    qpos = i * tq + lax.broadcasted_iota(jnp.int32, (tq, ncp), 0)
    dist = qpos - (lax.broadcasted_iota(jnp.int32, (tq, ncp), 1) * CMP_STRIDE + CMP_LEN - 1)
    mask = dist >= 0
    w0 = jnp.maximum(i * tq - (CMP_LEN - 1) - FAR_DIST + CMP_STRIDE, 0) // (CMP_STRIDE * LANE) * LANE
    w0 = pl.multiple_of(jnp.minimum(w0, ncp - ww), LANE)
    dist_w = (i * tq + lax.broadcasted_iota(jnp.int32, (tq, ww), 0)
              - ((w0 + lax.broadcasted_iota(jnp.int32, (tq, ww), 1)) * CMP_STRIDE + CMP_LEN - 1))
    bucket = _bucket_of(dist_w)
    blk = lax.broadcasted_iota(jnp.int32, (tq, ns), 1)
    qp = i * tq + lax.broadcasted_iota(jnp.int32, (tq, ns), 0)
    cur = qp // SLC_LEN
    forced = (blk == 0) | (blk == cur) | (blk == cur - 1)
    admissible = blk * SLC_LEN <= qp
    for g in range(NSA_KV_HEADS):
        s_ref[...] = _dot_nt(q_ref[g * rep:(g + 1) * rep].reshape(rep * tq, HEAD_DIM), kc_ref[g])
        imp = jnp.zeros((tq, ncp), F32)
        ps = []
        for r in range(rep):
            h = g * rep + r
            trow = _bias_row(tbl_ref, h, tq)
            bias = [jnp.take_along_axis(trow, bucket[:, c * LANE:(c + 1) * LANE], axis=1)
                    for c in range(ww // LANE)]
            s_ref[r * tq:(r + 1) * tq, pl.ds(w0, ww)] += jnp.concatenate(bias, axis=1)
            sh = jnp.where(mask, s_ref[r * tq:(r + 1) * tq, :], NEG)
            m = jnp.max(sh, axis=-1, keepdims=True)
            p = jnp.where(mask, jnp.exp2(sh - m), 0.0)
            pc = p / jnp.maximum(jnp.sum(p, axis=-1, keepdims=True), 1e-30)
            imp = imp + pc
            ps.append(pc.astype(BF16))
        o = jnp.dot(jnp.concatenate(ps, axis=0), vc_ref[g], preferred_element_type=F32)
        for r in range(rep):
            h = g * rep + r
            oc_ref[:, h * HEAD_DIM:(h + 1) * HEAD_DIM] = o[r * tq:(r + 1) * tq]
        hi = imp.astype(BF16)
        lo = (imp - hi.astype(F32)).astype(BF16)
        impb = (jnp.dot(hi, ov_ref[...], preferred_element_type=F32)
                + jnp.dot(lo, ov_ref[...], preferred_element_type=F32))
        impb = jnp.where(forced, FORCE_SCORE, impb)
        key_ref[:, g * tq:(g + 1) * tq] = _to_key(jnp.where(admissible, impb, -jnp.inf).T)
    keys = key_ref[...]
    thr = _kth_key_val(keys, nsel, axis=0)
    above = jnp.where(keys > thr, 1.0, 0.0)
    at = jnp.where(keys == thr, 1.0, 0.0)
    need = nsel - jnp.sum(above, axis=0, keepdims=True)
    blk_t = lax.broadcasted_iota(jnp.int32, keys.shape, 0)

    def halve(_, st):
        lo_c, hi_c = st
        mid = (lo_c + hi_c) >> 1
        enough = jnp.sum(jnp.where(blk_t < mid, at, 0.0), axis=0, keepdims=True) >= need
        return jnp.where(enough, lo_c, mid), jnp.where(enough, mid, hi_c)

    row = (1, keys.shape[1])
    surplus = jnp.max(jnp.sum(at, axis=0, keepdims=True) - need)
    steps = jnp.where(surplus > 0.5, max(ns, 2).bit_length(), 0)
    _, cut = lax.fori_loop(0, steps, halve, (jnp.zeros(row, jnp.int32), jnp.full(row, ns, jnp.int32)))
    sel_t = above + jnp.where(blk_t < cut, at, 0.0)
    for g in range(NSA_KV_HEADS):
        sel_ref[g] = sel_t[:, g * tq:(g + 1) * tq].T.astype(BF16)


def _cmp_attn(table, q_h, kc, vc):
    nheads, t, _ = q_h.shape
    ncp = kc.shape[1]
    ns = t // SLC_LEN
    nsel = min(SLC_TOP, ns)
    tq = _pick(t, 128)
    ww = min(ncp, LANE * (-(-(FAR_DIST + tq + CMP_STRIDE * LANE) // (CMP_STRIDE * LANE))))
    rep = nheads // NSA_KV_HEADS
    c_start = np.arange(ncp)[:, None] * CMP_STRIDE
    s_start = np.arange(ns)[None, :] * SLC_LEN
    overlap = jnp.asarray((c_start < s_start + SLC_LEN) & (c_start + CMP_LEN > s_start), BF16)
    full = lambda shp: pl.BlockSpec(shp, lambda i: (0,) * len(shp))
    return pl.pallas_call(
        functools.partial(_cmp_attn_kernel, tq=tq, ncp=ncp, ns=ns, nsel=nsel, ww=ww), grid=(t // tq,),
        in_specs=[full((nheads, LANE)),
                  pl.BlockSpec((nheads, tq, HEAD_DIM), lambda i: (0, i, 0)),
                  full((NSA_KV_HEADS, ncp, HEAD_DIM)), full((NSA_KV_HEADS, ncp, HEAD_DIM)), full((ncp, ns))],
        out_specs=[pl.BlockSpec((tq, nheads * HEAD_DIM), lambda i: (i, 0)),
                   pl.BlockSpec((NSA_KV_HEADS, tq, ns), lambda i: (0, i, 0))],
        out_shape=[jax.ShapeDtypeStruct((t, nheads * HEAD_DIM), F32),
                   jax.ShapeDtypeStruct((NSA_KV_HEADS, t, ns), BF16)],
        scratch_shapes=[pltpu.VMEM((rep * tq, ncp), F32), pltpu.VMEM((ns, NSA_KV_HEADS * tq), jnp.int32)],
        compiler_params=_cparams(("parallel",)), name="cmp_attn")(_table_rows(table), q_h, kc, vc, overlap)


def _merge_kernel(ya_ref, yb_ref, wa_ref, wb_ref, ga_ref, gb_ref, o_ref):
    za = jnp.dot(ya_ref[...], wa_ref[...], preferred_element_type=F32)
    zb = jnp.dot(yb_ref[...], wb_ref[...], preferred_element_type=F32)
    o_ref[...] = (jax.nn.sigmoid(ga_ref[...]) * za + jax.nn.sigmoid(gb_ref[...]) * zb).astype(o_ref.dtype)


def _merge(y_a, y_b, w_a, w_b, pm):
    t, k = y_a.shape
    d = w_a.shape[1]
    tm, tn = _pick(t, 1024), _pick(d, 512)
    assert OFF_GM % tn == 0
    lhs = pl.BlockSpec((tm, k), lambda i, j: (i, 0))
    rhs = pl.BlockSpec((k, tn), lambda i, j: (0, j))
    return pl.pallas_call(
        _merge_kernel, grid=(t // tm, d // tn),
        in_specs=[lhs, lhs, rhs, rhs, pl.BlockSpec((tm, tn), lambda i, j: (i, OFF_GM // tn + j)),
                  pl.BlockSpec((tm, tn), lambda i, j: (i, (OFF_GM + d) // tn + j))],
        out_specs=pl.BlockSpec((tm, tn), lambda i, j: (i, j)), out_shape=jax.ShapeDtypeStruct((t, d), BF16),
        compiler_params=_cparams(("parallel", "arbitrary")), name="merge")(y_a, y_b, w_a, w_b, pm, pm)


_CAND_FULL = PEER_TOPK // 2


def _peer_cands(a1, a2, op):
    pieces = [op(a1[0:1], a2)]
    pieces += [op(a1[k:k + 1], a2[0:_CAND_FULL]) for k in range(1, _CAND_FULL)]
    pieces.append(op(a1[_CAND_FULL:], a2[0:1]))
    return jnp.concatenate(pieces, axis=0)


def _peer_score_kernel(q_ref, sk_ref, s1_o, e1_o, s2_o, e2_o, thr_o, *, tm):
    row = lax.broadcasted_iota(jnp.int32, (PEER_NKEYS, tm), 0).astype(F32)
    for h in range(PEER_HEADS):
        svals, tops = [], []
        for c in range(2):
            hc = 2 * h + c
            s = _dot_nt(sk_ref[hc], q_ref[:, hc * 128:(hc + 1) * 128].astype(BF16))
            svals.append(s)
            work, top = s, []
            for _ in range(PEER_TOPK):
                mx = jnp.max(work, axis=0, keepdims=True)
                first = jnp.min(jnp.where(work == mx, row, float(PEER_NKEYS)), axis=0, keepdims=True)
                work = jnp.where(row == first, -jnp.inf, work)
                top.append(mx)
            tops.append(jnp.concatenate(top, axis=0))
        a1, a2 = tops
        cand = _peer_cands(a1, a2, jnp.add)
        thr = _kth_largest(cand, PEER_TOPK, 0)
        m1, m2 = a1[0:1], a2[0:1]
        ec = _peer_cands(jnp.exp(a1 - m1), jnp.exp(a2 - m2), jnp.multiply)
        z = jnp.sum(jnp.where(cand >= thr, ec, 0.0), axis=0, keepdims=True)
        s1_o[h] = svals[0]
        s2_o[h] = svals[1]
        e1_o[h] = jnp.exp(svals[0] - m1)
        e2_o[h] = jnp.exp(svals[1] - m2) / z
        thr_o[h:h + 1, :] = thr


def _peer_score(qp, sub_keys):
    t = qp.shape[0]
    tm = _pick(t, 256)
    sk = sub_keys.reshape(2 * PEER_HEADS, PEER_NKEYS, PEER_QDIM // 2).astype(BF16)
    tr = pl.BlockSpec((PEER_HEADS, PEER_NKEYS, tm), lambda i: (0, 0, i))
    shp = jax.ShapeDtypeStruct((PEER_HEADS, PEER_NKEYS, t), F32)
    return pl.pallas_call(
        functools.partial(_peer_score_kernel, tm=tm), grid=(t // tm,),
        in_specs=[pl.BlockSpec((tm, 2 * PEER_HEADS * 128), lambda i: (i, 0)),
                  pl.BlockSpec((2 * PEER_HEADS, PEER_NKEYS, PEER_QDIM // 2), lambda i: (0, 0, 0))],
        out_specs=[tr, tr, tr, tr, pl.BlockSpec((PEER_HEADS, tm), lambda i: (0, i))],
        out_shape=[shp, shp, shp, shp, jax.ShapeDtypeStruct((PEER_HEADS, t), F32)],
        compiler_params=_cparams(("parallel",)), name="peer_score")(qp, sk)


def _peer_act_kernel(h_ref, u_ref, s1_ref, e1_ref, s2_ref, e2_ref, thr_ref, o_ref, w_ref, a_ref, *, tm, te, sub):
    j = pl.program_id(1)
    n1 = te // PEER_NKEYS
    a_ref[...] = _dot_nt(h_ref[...], u_ref[...])
    for ai in range(n1):
        i1 = j * n1 + ai
        s1rows = [s1_ref[h, pl.ds(i1, 1), :] for h in range(PEER_HEADS)]
        e1rows = [e1_ref[h, pl.ds(i1, 1), :] for h in range(PEER_HEADS)]
        for ts in range(tm // LANE):
            tok = slice(ts * LANE, (ts + 1) * LANE)
            wt = jnp.zeros((PEER_NKEYS, LANE), F32)
            for h in range(PEER_HEADS):
                hit = (s1rows[h][:, tok] + s2_ref[h, :, tok]) >= thr_ref[h:h + 1, tok]
                wt = wt + jnp.where(hit, e1rows[h][:, tok] * e2_ref[h, :, tok], 0.0)
            w_ref[tok, ai * PEER_NKEYS:(ai + 1) * PEER_NKEYS] = wt.T
    for r in range(tm // sub):
        rows = slice(r * sub, (r + 1) * sub)
        o_ref[rows, :] = (_gelu(a_ref[rows, :]) * w_ref[rows, :]).astype(o_ref.dtype)


def _peer_act(h2, u, s1, e1, s2, e2, thr):
    t, d = h2.shape
    ne = u.shape[0]
    tm, te, sub = _pick(t, 1024), 512, 256
    once = dict(pipeline_mode=pl.Buffered(1))
    tok = pl.BlockSpec((PEER_HEADS, PEER_NKEYS, tm), lambda i, j: (0, 0, i), **once)
    return pl.pallas_call(
        functools.partial(_peer_act_kernel, tm=tm, te=te, sub=sub), grid=(t // tm, ne // te),
        in_specs=[pl.BlockSpec((tm, d), lambda i, j: (i, 0), **once),
                  pl.BlockSpec((te, d), lambda i, j: (j, 0)),
                  tok, tok, tok, tok, pl.BlockSpec((PEER_HEADS, tm), lambda i, j: (0, i), **once)],
        out_specs=pl.BlockSpec((tm, te), lambda i, j: (i, j)),
        out_shape=jax.ShapeDtypeStruct((t, ne), BF16),
        scratch_shapes=[pltpu.VMEM((tm, te), F32), pltpu.VMEM((tm, te), F32)],
        compiler_params=_cparams(("parallel", "arbitrary")), name="peer_act")(h2, u, s1, e1, s2, e2, thr)


def _pack_w_in(w_in, d):
    offs = np.cumsum([0, DSA_HEADS * 128, DSA_KV_HEADS * 128, DSA_KV_HEADS * 128, IDX_HEADS * IDX_DIM, IDX_DIM,
                      IDX_HEADS, NSA_HEADS * 128, 6 * NSA_KV_HEADS * 128, 3 * NSA_HEADS, 2 * d])
    qa, ka, va, qi, ki, wi, qb, kvb, gb, gm = [w_in[:, int(offs[n]):int(offs[n + 1])] for n in range(10)]
    main = jnp.concatenate([qa, qi, qb, kvb, ka, va, gm], axis=1).astype(BF16)
    pad = lambda n: jnp.zeros((d, n), w_in.dtype)
    tail = jnp.concatenate([ki, wi, pad(128 - IDX_DIM - IDX_HEADS), gb, pad(128 - 3 * NSA_HEADS)], axis=1).astype(BF16)
    return main, tail


def _token_mixer(h, table, w_in, gq_a, gk_a, gq_b, gk_cmp, gk_slc, gk_win, cmp_pe_k, cmp_w1_k, cmp_w2_k,
                 cmp_pe_v, cmp_w1_v, cmp_w2_v, w_branch_a, w_branch_b):
    t, d = h.shape
    w_main, w_tail = _pack_w_in(w_in, d)
    pm = _matmul(h, w_main, tn=1024, name="proj_main")
    pt = _matmul(h, w_tail, name="proj_tail")
    qa_h, qi_h, qb_h, kvb_h, ka_h, va_h, ki, wi, gsig = _prep(pm, pt, gq_a, gk_a, gq_b, gk_slc, gk_win)
    tq, tk = _pick(t, 256), _pick(t, 512)
    keys, thr = _dsa_index(qi_h, ki.T, wi, min(DSA_TOPK, t // 4))
    bias_a = _bias_tiles(table[:, :DSA_HEADS], tq, tk)
    y_a = _flash("dsa", qa_h, ka_h, va_h, bias_a, (keys, thr), BF16)
    g = NSA_KV_HEADS
    kc = _compress(kvb_h, 0, cmp_pe_k, cmp_w1_k, cmp_w2_k, gk_cmp, True)
    vc = _compress(kvb_h, g, cmp_pe_v, cmp_w1_v, cmp_w2_v, gk_cmp, False)
    o_c, sel = _cmp_attn(table[:, DSA_HEADS:], qb_h, kc, vc)
    bias_b = _bias_tiles(table[:, DSA_HEADS:], tq, tk)
    ns = t // SLC_LEN
    expand = jnp.asarray(np.arange(ns)[:, None] == (np.arange(t)[None, :] // SLC_LEN), BF16)
    o_s = _flash("slc", qb_h, kvb_h[2 * g:3 * g], kvb_h[3 * g:4 * g], bias_b, (sel, expand), F32)
    y_b = _flash("win", qb_h, kvb_h[4 * g:5 * g], kvb_h[5 * g:6 * g], bias_b, (o_c, o_s, gsig), BF16)
    return _merge(y_a, y_b, w_branch_a.astype(BF16), w_branch_b.astype(BF16), pm)


def _peer_ffn(h2, x1, gt2, w_q, sub_keys, u, v):
    qp = _matmul(h2, w_q.astype(BF16), name="peer_q")
    s1, e1, s2, e2, thr = _peer_score(qp, sub_keys)
    act = _peer_act(h2, u.astype(BF16), s1, e1, s2, e2, thr)
    return _matmul_resid(act, v.astype(BF16), x1, gt2, name="peer_out")


def kernel(x, c, rel_bias, w_ada, b_ada, g_mix, w_in, gq_a, gk_a, gq_b, gk_cmp, gk_slc, gk_win, cmp_pe_k, cmp_w1_k,
           cmp_w2_k, cmp_pe_v, cmp_w1_v, cmp_w2_v, w_branch_a, w_branch_b, w_out, g_ffn, w_peer_q, peer_sub_keys,
           peer_u, peer_v):
    bsz, t, d = x.shape
    assert bsz == 1 and t % 512 == 0 and d % 128 == 0
    xs = x[0]
    for i in range(w_ada.shape[0]):
        mod = _ada(c, w_ada[i], b_ada[i])
        sh1, sc1, gt1, sh2, sc2, gt2 = [mod[:, n * d:(n + 1) * d] for n in range(6)]
        h = _normmod(xs, g_mix[i], sc1, sh1)
        merged = _token_mixer(h, rel_bias, w_in[i], gq_a[i], gk_a[i], gq_b[i], gk_cmp[i], gk_slc[i], gk_win[i],
                              cmp_pe_k[i], cmp_w1_k[i], cmp_w2_k[i], cmp_pe_v[i], cmp_w1_v[i], cmp_w2_v[i],
                              w_branch_a[i], w_branch_b[i])
        x1 = _matmul_resid(merged, w_out[i].astype(BF16), xs, gt1, name="w_out")
        h2 = _normmod(x1, g_ffn[i], sc2, sh2)
        xs = _peer_ffn(h2, x1, gt2, w_peer_q[i], peer_sub_keys[i], peer_u[i], peer_v[i])
    return xs[None]
```

```python
import functools
import math

import numpy as np
import jax
import jax.numpy as jnp
from jax import lax
from jax.experimental import pallas as pl
from jax.experimental.pallas import tpu as pltpu

HEAD_DIM = 128
DSA_HEADS = 16
DSA_KV_HEADS = 4
IDX_HEADS = 32
IDX_DIM = 64
DSA_TOPK = 256
NSA_HEADS = 16
NSA_KV_HEADS = 4
CMP_LEN = 32
CMP_STRIDE = 16
CMP_HIDDEN = 256
SLC_LEN = 64
SLC_TOP = 16
WIN = 512
FORCE_SCORE = 1e9
PEER_HEADS = 8
PEER_NKEYS = 128
PEER_QDIM = 256
PEER_TOPK = 16
REL_BUCKETS = 32
REL_MAX_DIST = 2048
EPS = 1e-6

LANE = 128
VMEM_LIMIT = 56 * 1024 * 1024
NEG = -1e30
LOG2E = math.log2(math.e)
INT_MIN = -2 ** 31
KEY_NEG_INF = -2 ** 31 + 0x007FFFFF

OFF_QA, OFF_QI, OFF_QB, OFF_KVB, OFF_KA, OFF_VA, OFF_GM = 0, 2048, 4096, 6144, 9216, 9728, 10240

F32 = jnp.float32
BF16 = jnp.bfloat16


def _bucket_starts():
    n = np.arange(2 * REL_MAX_DIST)
    exact = REL_BUCKETS // 2
    nf = np.maximum(n, 1).astype(np.float32)
    lb = exact + (np.log(nf / np.float32(exact)) / np.float32(math.log(REL_MAX_DIST / exact))
                  * np.float32(REL_BUCKETS - exact)).astype(np.int32)
    bucket = np.where(n < exact, n, np.minimum(lb, REL_BUCKETS - 1))
    return [int(np.argmax(bucket >= b)) for b in range(REL_BUCKETS)]


BUCKET_STARTS = _bucket_starts()
FAR_DIST = BUCKET_STARTS[REL_BUCKETS - 1]


def _cparams(sem):
    return pltpu.CompilerParams(dimension_semantics=sem, vmem_limit_bytes=VMEM_LIMIT)


def _pick(n, pref):
    t = pref
    while n % t:
        t //= 2
    return t


def _to_key(x):
    b = lax.bitcast_convert_type(x, jnp.int32)
    return b ^ ((b >> 31) & 0x7FFFFFFF)


def _from_key(k):
    return lax.bitcast_convert_type(k ^ ((k >> 31) & 0x7FFFFFFF), F32)


def _kth_key(count_ge, shape, kk):
    def bit_pass(b, x):
        cand = x + lax.shift_left(jnp.int32(1), 31 - b)
        return jnp.where(count_ge(cand) >= kk, cand, x)
    x = lax.fori_loop(0, 32, bit_pass, jnp.full(shape, INT_MIN, jnp.int32))
    return jnp.maximum(x, KEY_NEG_INF)


def _kth_key_bracket(count_ge, kmin, kmax, kk):
    c_min = count_ge(kmin)
    few = c_min < kk
    lo0 = jnp.where(few, KEY_NEG_INF, kmin)
    hi0 = kmax + 1
    done0 = jnp.where(few | (c_min == kk) | (hi0 - 1 <= lo0), 1.0, 0.0)

    def cond(st):
        p, _, _, _, _, done = st
        return jnp.logical_and(p < 72, jnp.min(done) < 0.5)

    def body(st):
        p, lo, hi, clo, chi, done = st
        mid = (lo >> 1) + (hi >> 1) + (lo & hi & 1)
        flo, fhi = _from_key(lo), _from_key(hi)
        frac = (jnp.log(clo) - math.log(kk)) / (jnp.log(clo) - jnp.log(jnp.maximum(chi, 0.5)))
        cand = jnp.where(p % 2 == 1, mid, _to_key(flo + frac * (fhi - flo)))
        cand = jnp.minimum(jnp.maximum(cand, lo + 1), hi - 1)
        active = done < 0.5
        cand = jnp.where(active, cand, lo)
        cnt = count_ge(cand)
        up = active & (cnt >= kk)
        dn = active & (cnt < kk)
        lo, clo = jnp.where(up, cand, lo), jnp.where(up, cnt, clo)
        hi, chi = jnp.where(dn, cand, hi), jnp.where(dn, cnt, chi)
        done = jnp.where((clo == kk) | (hi - 1 <= lo), 1.0, done)
        return p + 1, lo, hi, clo, chi, done

    st = lax.while_loop(cond, body, (jnp.int32(0), lo0, hi0, c_min, jnp.zeros_like(c_min), done0))
    return st[1], st[3]


def _kth_largest(vals, kk, axis):
    shape = list(vals.shape)
    shape[axis] = 1

    def body(_, st):
        work, taken, thr = st
        mx = jnp.max(work, axis=axis, keepdims=True)
        hit = work == mx
        thr = jnp.where(taken < kk, mx, thr)
        taken = taken + jnp.sum(jnp.where(hit, 1.0, 0.0), axis=axis, keepdims=True)
        return jnp.where(hit, -jnp.inf, work), taken, thr

    st = (vals, jnp.zeros(shape, F32), jnp.full(shape, -jnp.inf, F32))
    return lax.fori_loop(0, kk, body, st)[2]


def _kth_key_val(keys, kk, axis=-1):
    def count_ge(cand):
        return jnp.sum(jnp.where(keys >= cand, 1.0, 0.0), axis=axis, keepdims=True)
    shape = list(keys.shape)
    shape[axis] = 1
    return _kth_key(count_ge, tuple(shape), float(kk))


def _gelu(x):
    return 0.5 * x * (1.0 + jnp.tanh(math.sqrt(2.0 / math.pi) * (x + 0.044715 * (x * x * x))))


def _dot_nt(a, b):
    return lax.dot_general(a, b, (((1,), (1,)), ((), ())), preferred_element_type=F32)


def _head_norm(x, g, scale):
    ms = jnp.mean(x * x, axis=-1, keepdims=True)
    return x * lax.rsqrt(ms + EPS) * g * scale


def _mm_kernel(a_ref, b_ref, o_ref):
    o_ref[...] = jnp.dot(a_ref[...], b_ref[...], preferred_element_type=F32).astype(o_ref.dtype)


def _matmul(a, b, out_dtype=F32, tm=1024, tn=512, name="matmul"):
    m, k = a.shape
    n = b.shape[1]
    tm, tn = _pick(m, tm), _pick(n, tn)
    return pl.pallas_call(
        _mm_kernel, grid=(m // tm, n // tn),
        in_specs=[pl.BlockSpec((tm, k), lambda i, j: (i, 0)), pl.BlockSpec((k, tn), lambda i, j: (0, j))],
        out_specs=pl.BlockSpec((tm, tn), lambda i, j: (i, j)),
        out_shape=jax.ShapeDtypeStruct((m, n), out_dtype),
        compiler_params=_cparams(("parallel", "arbitrary")), name=name)(a, b)


def _mm_resid_kernel(a_ref, b_ref, x_ref, gt_ref, o_ref):
    k = pl.program_id(2)
    part = jnp.dot(a_ref[...], b_ref[...], preferred_element_type=F32)

    @pl.when(k == 0)
    def _():
        o_ref[...] = part

    @pl.when(k > 0)
    def _():
        o_ref[...] += part

    @pl.when(k == pl.num_programs(2) - 1)
    def _():
        o_ref[...] = x_ref[...] + gt_ref[...] * o_ref[...]


def _matmul_resid(a, b, x, gt, name):
    m, kd = a.shape
    n = b.shape[1]
    tm, tn, tk = _pick(m, 1024), _pick(n, 1024), _pick(kd, 2048)
    if kd // tk <= 2:
        tn, tk = _pick(n, 512), kd
    return pl.pallas_call(
        _mm_resid_kernel, grid=(m // tm, n // tn, kd // tk),
        in_specs=[pl.BlockSpec((tm, tk), lambda i, j, k: (i, k)), pl.BlockSpec((tk, tn), lambda i, j, k: (k, j)),
                  pl.BlockSpec((tm, tn), lambda i, j, k: (i, j)), pl.BlockSpec((1, tn), lambda i, j, k: (0, j))],
        out_specs=pl.BlockSpec((tm, tn), lambda i, j, k: (i, j)),
        out_shape=jax.ShapeDtypeStruct((m, n), F32),
        compiler_params=_cparams(("parallel", "parallel", "arbitrary")), name=name)(a, b, x, gt)


def _ada_kernel(c_ref, w_ref, b_ref, o_ref):
    o_ref[...] = jnp.dot(c_ref[...], w_ref[...].astype(BF16), preferred_element_type=F32) + b_ref[...]


def _ada(c, w, b):
    d, n = w.shape
    tn = _pick(n, 512)
    c8 = jnp.broadcast_to(c.astype(BF16), (8, d))
    out = pl.pallas_call(
        _ada_kernel, grid=(n // tn,),
        in_specs=[pl.BlockSpec((8, d), lambda j: (0, 0)), pl.BlockSpec((d, tn), lambda j: (0, j)),
                  pl.BlockSpec((1, tn), lambda j: (0, j))],
        out_specs=pl.BlockSpec((8, tn), lambda j: (0, j)),
        out_shape=jax.ShapeDtypeStruct((8, n), F32),
        compiler_params=_cparams(("arbitrary",)), name="ada")(c8, w, b.reshape(1, n))
    return out[0:1]


def _normmod_kernel(x_ref, g_ref, sc_ref, sh_ref, o_ref):
    x = x_ref[...]
    y = x * lax.rsqrt(jnp.mean(x * x, axis=-1, keepdims=True) + EPS) * g_ref[...]
    o_ref[...] = (y * (1.0 + sc_ref[...]) + sh_ref[...]).astype(o_ref.dtype)


def _normmod(x, g, sc, sh):
    t, d = x.shape
    tm = _pick(t, 256)
    row = pl.BlockSpec((1, d), lambda i: (0, 0))
    return pl.pallas_call(
        _normmod_kernel, grid=(t // tm,),
        in_specs=[pl.BlockSpec((tm, d), lambda i: (i, 0)), row, row, row],
        out_specs=pl.BlockSpec((tm, d), lambda i: (i, 0)),
        out_shape=jax.ShapeDtypeStruct((t, d), BF16),
        compiler_params=_cparams(("parallel",)), name="normmod")(x, g.reshape(1, d), sc, sh)


def _prep_kernel(qa_ref, qi_ref, qb_ref, kvb_ref, ka_ref, va_ref, tail_ref,
                 gqa_ref, gka_ref, gqb_ref, gslc_ref, gwin_ref,
                 qa_o, qi_o, qb_o, kvb_o, ka_o, va_o, ki_o, wi_o, gs_o):
    qscale = HEAD_DIM ** -0.5 * LOG2E
    for h in range(DSA_HEADS):
        qa_o[h] = _head_norm(qa_ref[:, h * 128:(h + 1) * 128], gqa_ref[...], qscale).astype(BF16)
    for h in range(NSA_HEADS):
        qb_o[h] = _head_norm(qb_ref[:, h * 128:(h + 1) * 128], gqb_ref[...], qscale).astype(BF16)
    for h in range(IDX_HEADS):
        qi_o[h] = (qi_ref[:, h * IDX_DIM:(h + 1) * IDX_DIM] * IDX_DIM ** -0.5).astype(BF16)
    for g in range(DSA_KV_HEADS):
        ka_o[g] = _head_norm(ka_ref[:, g * 128:(g + 1) * 128], gka_ref[...], 1.0).astype(BF16)
        va_o[g] = va_ref[:, g * 128:(g + 1) * 128].astype(BF16)
    for s in range(6 * NSA_KV_HEADS):
        xs = kvb_ref[:, s * 128:(s + 1) * 128]
        part = s // NSA_KV_HEADS
        if part == 2:
            xs = _head_norm(xs, gslc_ref[...], 1.0)
        elif part == 4:
            xs = _head_norm(xs, gwin_ref[...], 1.0)
        kvb_o[s] = xs.astype(BF16)
    tail = tail_ref[...]
    ki_o[...] = tail[:, 0:IDX_DIM].astype(BF16)
    wi_o[...] = tail[:, 0:128] * IDX_HEADS ** -0.5
    gs_o[...] = jax.nn.sigmoid(tail[:, 128:256])


def _prep(pm, pt, gq_a, gk_a, gq_b, gk_slc, gk_win):
    t = pm.shape[0]
    tm = _pick(t, 128)
    g128 = pl.BlockSpec((1, 128), lambda i: (0, 0))

    def col(w, off):
        assert off % w == 0
        return pl.BlockSpec((tm, w), lambda i: (i, off // w))

    def hm(n, w):
        return pl.BlockSpec((n, tm, w), lambda i: (0, i, 0))

    row = lambda w: pl.BlockSpec((tm, w), lambda i: (i, 0))
    outs = pl.pallas_call(
        _prep_kernel, grid=(t // tm,),
        in_specs=[col(2048, OFF_QA), col(2048, OFF_QI), col(2048, OFF_QB), col(3072, OFF_KVB),
                  col(512, OFF_KA), col(512, OFF_VA), row(256), g128, g128, g128, g128, g128],
        out_specs=[hm(DSA_HEADS, 128), hm(IDX_HEADS, IDX_DIM), hm(NSA_HEADS, 128), hm(24, 128),
                   hm(DSA_KV_HEADS, 128), hm(DSA_KV_HEADS, 128), row(IDX_DIM), row(128), row(128)],
        out_shape=[jax.ShapeDtypeStruct((DSA_HEADS, t, 128), BF16),
                   jax.ShapeDtypeStruct((IDX_HEADS, t, IDX_DIM), BF16),
                   jax.ShapeDtypeStruct((NSA_HEADS, t, 128), BF16),
                   jax.ShapeDtypeStruct((24, t, 128), BF16),
                   jax.ShapeDtypeStruct((DSA_KV_HEADS, t, 128), BF16),
                   jax.ShapeDtypeStruct((DSA_KV_HEADS, t, 128), BF16),
                   jax.ShapeDtypeStruct((t, IDX_DIM), BF16),
                   jax.ShapeDtypeStruct((t, 128), F32),
                   jax.ShapeDtypeStruct((t, 128), F32)],
        compiler_params=_cparams(("parallel",)), name="prep")(
            pm, pm, pm, pm, pm, pm, pt,
            gq_a.reshape(1, 128), gk_a.reshape(1, 128), gq_b.reshape(1, 128),
            gk_slc.reshape(1, 128), gk_win.reshape(1, 128))
    return outs


def _bucket_of(dist):
    bucket = jnp.zeros(dist.shape, jnp.int32)
    for b in range(1, REL_BUCKETS):
        bucket = jnp.where(dist >= BUCKET_STARTS[b], b, bucket)
    return bucket


def _bias_row(tbl_ref, h, rows):
    trow = (tbl_ref[h:h + 1, :] - tbl_ref[h:h + 1, REL_BUCKETS - 1:REL_BUCKETS]) * LOG2E
    return jnp.broadcast_to(trow, (rows, LANE))


def _bias_tiles_kernel(tbl_ref, o_ref, *, tq, tk, nheads):
    d = pl.program_id(0)
    dist = d * tq + lax.broadcasted_iota(jnp.int32, (tq, tk), 0) - lax.broadcasted_iota(jnp.int32, (tq, tk), 1)
    bucket = _bucket_of(dist)
    for h in range(nheads):
        trow = _bias_row(tbl_ref, h, tq)
        o_ref[h, 0] = jnp.concatenate(
            [jnp.take_along_axis(trow, bucket[:, c * LANE:(c + 1) * LANE], axis=1) for c in range(tk // LANE)], axis=1)


def _n_near(tq, tk):
    return -(-(FAR_DIST + tk - 1) // tq)


def _bias_tiles(table, tq, tk):
    nheads = table.shape[1]
    nd = _n_near(tq, tk)
    return pl.pallas_call(
        functools.partial(_bias_tiles_kernel, tq=tq, tk=tk, nheads=nheads), grid=(nd,),
        in_specs=[pl.BlockSpec((nheads, LANE), lambda d: (0, 0))],
        out_specs=pl.BlockSpec((nheads, 1, tq, tk), lambda d: (0, d, 0, 0)),
        out_shape=jax.ShapeDtypeStruct((nheads, nd, tq, tk), F32),
        compiler_params=_cparams(("parallel",)), name="bias_tiles")(_table_rows(table))


def _table_rows(table):
    return jnp.pad(table.T, ((0, 0), (0, LANE - REL_BUCKETS)))


def _idx_kernel(q_ref, kt_ref, w_ref, key_ref, thr_ref, s_ref, wb_ref, *, tq, tk, ts, t, kk):
    i = pl.program_id(0)
    nkt = (i * tq + tq + tk - 1) // tk
    w = w_ref[...]
    for h in range(IDX_HEADS):
        wb_ref[h] = jnp.broadcast_to(w[:, IDX_DIM + h:IDX_DIM + h + 1], (tq, LANE))
    qall = q_ref[...].reshape(IDX_HEADS * tq, IDX_DIM)
    rows = lax.broadcasted_iota(jnp.int32, (tq, ts), 0)
    cols = lax.broadcasted_iota(jnp.int32, (tq, ts), 1)

    def tile(j, carry):
        offs = [pl.multiple_of(j * tk + u * ts, ts) for u in range(tk // ts)]
        for u, off in enumerate(offs):
            s_ref[u] = jnp.dot(qall, kt_ref[:, pl.ds(off, ts)], preferred_element_type=F32)
        for u, off in enumerate(offs):
            acc = jnp.zeros((tq, ts), F32)
            for h in range(IDX_HEADS):
                wbh = jnp.concatenate([wb_ref[h]] * (ts // LANE), axis=1)
                acc = acc + jnp.maximum(s_ref[u, h * tq:(h + 1) * tq, :], 0.0) * wbh
            acc = jnp.where(off + cols <= i * tq + rows, acc, -jnp.inf)
            key_ref[:, pl.ds(off, ts)] = _to_key(acc)
        return carry

    lax.fori_loop(0, nkt, tile, 0)

    def fill(j, carry):
        key_ref[:, pl.ds(pl.multiple_of(j * tk, tk), tk)] = jnp.full((tq, tk), KEY_NEG_INF, jnp.int32)
        return carry

    lax.fori_loop(nkt, t // tk, fill, 0)

    def count_ge(cand):
        def body(j, c):
            kt = key_ref[:, pl.ds(pl.multiple_of(j * tk, tk), tk)]
            ge = jnp.where(kt >= cand, 1.0, 0.0)
            for a in range(tk // LANE):
                c = c + ge[:, a * LANE:(a + 1) * LANE]
            return c
        c = lax.fori_loop(0, nkt, body, jnp.zeros((tq, LANE), F32))
        return jnp.sum(c, axis=-1, keepdims=True)

    def score_range(j, c):
        smin, smax = c
        st = _from_key(key_ref[:, pl.ds(pl.multiple_of(j * tk, tk), tk)])
        lo_t = jnp.where(st == -jnp.inf, jnp.inf, st)
        for a in range(tk // LANE):
            smin = jnp.minimum(smin, lo_t[:, a * LANE:(a + 1) * LANE])
            smax = jnp.maximum(smax, st[:, a * LANE:(a + 1) * LANE])
        return smin, smax

    smin, smax = lax.fori_loop(0, nkt, score_range, (jnp.full((tq, LANE), jnp.inf, F32),
                                                     jnp.full((tq, LANE), -jnp.inf, F32)))
    kmin = _to_key(jnp.min(smin, axis=-1, keepdims=True))
    kmax = _to_key(jnp.max(smax, axis=-1, keepdims=True))
    thr, reach = _kth_key_bracket(count_ge, kmin, kmax, float(kk))
    thr_ref[...] = thr
    tied = reach > kk

    @pl.when(jnp.max(jnp.where(tied, 1.0, 0.0)) > 0.5)
    def _():
        need = kk - count_ge(thr + 1)

        def count_tied_before(cut):
            def body(j, c):
                off = pl.multiple_of(j * tk, tk)
                kt = key_ref[:, pl.ds(off, tk)]
                pos = off + lax.broadcasted_iota(jnp.int32, (tq, tk), 1)
                m = jnp.where(kt == thr, jnp.where(pos < cut, 1.0, 0.0), 0.0)
                for a in range(tk // LANE):
                    c = c + m[:, a * LANE:(a + 1) * LANE]
                return c
            c = lax.fori_loop(0, nkt, body, jnp.zeros((tq, LANE), F32))
            return jnp.sum(c, axis=-1, keepdims=True)

        def halve(_, st):
            lo_c, hi_c = st
            mid = (lo_c + hi_c) >> 1
            enough = count_tied_before(mid) >= need
            return jnp.where(enough, lo_c, mid), jnp.where(enough, mid, hi_c)

        steps = max(t, 2).bit_length()
        _, cut = lax.fori_loop(0, steps, halve, (jnp.zeros((tq, 1), jnp.int32), jnp.full((tq, 1), t, jnp.int32)))

        def demote(j, carry):
            off = pl.multiple_of(j * tk, tk)
            kt = key_ref[:, pl.ds(off, tk)]
            pos = off + lax.broadcasted_iota(jnp.int32, (tq, tk), 1)
            late = jnp.where(tied, jnp.where(kt == thr, jnp.where(pos >= cut, 1.0, 0.0), 0.0), 0.0)
            key_ref[:, pl.ds(off, tk)] = jnp.where(late > 0.5, thr - 1, kt)
            return carry

        lax.fori_loop(0, nkt, demote, 0)


def _dsa_index(qi_h, ki_t, wi, kk):
    t = ki_t.shape[1]
    tq, tk = _pick(t, 128), _pick(t, 512)
    ts = tk // 2
    return pl.pallas_call(
        functools.partial(_idx_kernel, tq=tq, tk=tk, ts=ts, t=t, kk=kk), grid=(t // tq,),
        in_specs=[pl.BlockSpec((IDX_HEADS, tq, IDX_DIM), lambda i: (0, i, 0)),
                  pl.BlockSpec((IDX_DIM, t), lambda i: (0, 0)),
                  pl.BlockSpec((tq, 128), lambda i: (i, 0))],
        out_specs=[pl.BlockSpec((tq, t), lambda i: (i, 0)), pl.BlockSpec((tq, 1), lambda i: (i, 0))],
        out_shape=[jax.ShapeDtypeStruct((t, t), jnp.int32), jax.ShapeDtypeStruct((t, 1), jnp.int32)],
        scratch_shapes=[pltpu.VMEM((tk // ts, IDX_HEADS * tq, ts), F32), pltpu.VMEM((IDX_HEADS, tq, LANE), F32)],
        compiler_params=_cparams(("parallel",)), name="dsa_index")(qi_h, ki_t, wi)


def _first_tile(mode, i, tq, tk):
    if mode != "win":
        return 0
    start = i * tq - (WIN - 1)
    return (max(start, 0) if isinstance(i, int) else jnp.maximum(start, 0)) // tk


def _last_tile(i, tq, tk):
    return (i * tq + tq - 1) // tk


def _flash_kernel(qi_ref, kj_ref, *refs, mode, tq, tk, nheads, ngroups, nnear):
    refs, (m_ref, l_ref, acc_ref, s_ref, p_ref, madd_ref) = refs[:-6], refs[-6:]
    if mode == "dsa":
        q_ref, k_ref, v_ref, b_ref, key_ref, thr_ref, o_ref = refs
    elif mode == "slc":
        q_ref, k_ref, v_ref, b_ref, sel_ref, e_ref, o_ref = refs
    else:
        q_ref, k_ref, v_ref, b_ref, oc_ref, os_ref, g_ref, o_ref = refs
    rep = nheads // ngroups
    nch = tk // LANE
    rblk = min(tq, 64)
    s = pl.program_id(0)
    i, j = qi_ref[s], kj_ref[s]
    jlast = _last_tile(i, tq, tk)
    d = i - (tk // tq) * j

    @pl.when(j == _first_tile(mode, i, tq, tk))
    def _():
        m_ref[...] = jnp.full(m_ref.shape, NEG, F32)
        l_ref[...] = jnp.zeros(l_ref.shape, F32)
        acc_ref[...] = jnp.zeros(acc_ref.shape, F32)

    def step(near):
        if near:
            dist = ((i * tq - j * tk) + lax.broadcasted_iota(jnp.int32, (tq, tk), 0)
                    - lax.broadcasted_iota(jnp.int32, (tq, tk), 1))
            causal = dist >= 0
            if mode == "win":
                causal = causal & (dist < WIN)
        if mode == "dsa":
            hit = key_ref[...] >= thr_ref[...]
            madd_ref[...] = jnp.where((hit & causal) if near else hit, 0.0, NEG)
        elif mode == "win":
            madd_ref[...] = jnp.where(causal, 0.0, NEG)
        else:
            hit = jnp.dot(sel_ref[...].reshape(ngroups * tq, sel_ref.shape[2]), e_ref[...],
                          preferred_element_type=F32) > 0.5
            for g in range(ngroups):
                hg = hit[g * tq:(g + 1) * tq]
                madd_ref[g * tq:(g + 1) * tq, :] = jnp.where((hg & causal) if near else hg, 0.0, NEG)
        for g in range(ngroups):
            s_ref[...] = _dot_nt(q_ref[g * rep:(g + 1) * rep].reshape(rep * tq, HEAD_DIM), k_ref[g])
            mrow = g * tq if mode == "slc" else 0
            for r in range(rep):
                h = g * rep + r
                for rb in range(tq // rblk):
                    rows = slice(rb * rblk, (rb + 1) * rblk)
                    srows = slice(r * tq + rb * rblk, r * tq + (rb + 1) * rblk)
                    sh = s_ref[srows, :] + madd_ref[mrow + rb * rblk:mrow + (rb + 1) * rblk, :]
                    if near:
                        sh = sh + b_ref[h, 0, rows, :]
                    chunks = [sh[:, c * LANE:(c + 1) * LANE] for c in range(nch)]
                    m_old = m_ref[h, rows, :]
                    tile_max = jnp.max(functools.reduce(jnp.maximum, chunks), axis=-1, keepdims=True)
                    m_new = jnp.maximum(m_old, tile_max)
                    alpha = jnp.exp2(m_old - m_new)
                    pcs = [jnp.exp2(c - m_new) for c in chunks]
                    l_ref[h, rows, :] = alpha * l_ref[h, rows, :] + functools.reduce(jnp.add, pcs)
                    m_ref[h, rows, :] = m_new
                    acc_ref[h, rows, :] = acc_ref[h, rows, :] * alpha
                    p_ref[srows, :] = jnp.concatenate(pcs, axis=1).astype(BF16)
            pv = jnp.dot(p_ref[...], v_ref[g], preferred_element_type=F32)
            for r in range(rep):
                h = g * rep + r
                acc_ref[h] = acc_ref[h] + pv[r * tq:(r + 1) * tq]

    if mode == "win":
        step(True)
    else:
        pl.when(d < nnear)(lambda: step(True))
        pl.when(d >= nnear)(lambda: step(False))

    @pl.when(j == jlast)
    def _():
        for h in range(nheads):
            cols = slice(h * HEAD_DIM, (h + 1) * HEAD_DIM)
            out = acc_ref[h] / jnp.sum(l_ref[h], axis=-1, keepdims=True)
            if mode == "win":
                g = g_ref[...]
                out = (g[:, 3 * h:3 * h + 1] * oc_ref[:, cols] + g[:, 3 * h + 1:3 * h + 2] * os_ref[:, cols]
                       + g[:, 3 * h + 2:3 * h + 3] * out)
            o_ref[:, cols] = out.astype(o_ref.dtype)


def _flash(mode, q_h, k_h, v_h, bias, extra, out_dtype):
    nheads, t, _ = q_h.shape
    ngroups = k_h.shape[0]
    rep = nheads // ngroups
    _, nnear, tq, tk = bias.shape
    nq = t // tq
    pairs = [(i, j) for i in range(nq) for j in range(_first_tile(mode, i, tq, tk), _last_tile(i, tq, tk) + 1)]
    qi = jnp.asarray([p[0] for p in pairs], jnp.int32)
    kj = jnp.asarray([p[1] for p in pairs], jnp.int32)

    def bmap(s, qi, kj):
        return jnp.minimum(qi[s] - (tk // tq) * kj[s], nnear - 1)

    in_specs = [pl.BlockSpec((nheads, tq, HEAD_DIM), lambda s, qi, kj: (0, qi[s], 0)),
                pl.BlockSpec((ngroups, tk, HEAD_DIM), lambda s, qi, kj: (0, kj[s], 0)),
                pl.BlockSpec((ngroups, tk, HEAD_DIM), lambda s, qi, kj: (0, kj[s], 0)),
                pl.BlockSpec((nheads, 1, tq, tk), lambda s, qi, kj: (0, bmap(s, qi, kj), 0, 0))]
    if mode == "dsa":
        in_specs += [pl.BlockSpec((tq, tk), lambda s, qi, kj: (qi[s], kj[s])),
                     pl.BlockSpec((tq, 1), lambda s, qi, kj: (qi[s], 0))]
    elif mode == "slc":
        ns = extra[0].shape[2]
        in_specs += [pl.BlockSpec((ngroups, tq, ns), lambda s, qi, kj: (0, qi[s], 0)),
                     pl.BlockSpec((ns, tk), lambda s, qi, kj: (0, kj[s]))]
    else:
        row = lambda w: pl.BlockSpec((tq, w), lambda s, qi, kj: (qi[s], 0))
        in_specs += [row(nheads * HEAD_DIM), row(nheads * HEAD_DIM), row(LANE)]
    grid_spec = pltpu.PrefetchScalarGridSpec(
        num_scalar_prefetch=2, grid=(len(pairs),), in_specs=in_specs,
        out_specs=pl.BlockSpec((tq, nheads * HEAD_DIM), lambda s, qi, kj: (qi[s], 0)),
        scratch_shapes=[pltpu.VMEM((nheads, tq, LANE), F32), pltpu.VMEM((nheads, tq, LANE), F32),
                        pltpu.VMEM((nheads, tq, HEAD_DIM), F32),
                        pltpu.VMEM((rep * tq, tk), F32), pltpu.VMEM((rep * tq, tk), BF16),
                        pltpu.VMEM(((ngroups if mode == "slc" else 1) * tq, tk), F32)])
    return pl.pallas_call(
        functools.partial(_flash_kernel, mode=mode, tq=tq, tk=tk, nheads=nheads, ngroups=ngroups, nnear=nnear),
        grid_spec=grid_spec, out_shape=jax.ShapeDtypeStruct((t, nheads * HEAD_DIM), out_dtype),
        compiler_params=_cparams(("arbitrary",)), name="flash_" + mode)(qi, kj, q_h, k_h, v_h, bias, *extra)


def _compress_kernel(c_ref, w1_ref, pe_ref, w2_ref, g_ref, o_ref, *, norm):
    c = c_ref[0]
    half = CMP_STRIDE * HEAD_DIM
    ncp = c.shape[0]
    a = jnp.dot(c, w1_ref[0:half], preferred_element_type=F32)
    b = jnp.dot(c, w1_ref[half:2 * half], preferred_element_type=F32)
    pet = jnp.dot(pe_ref[...], w1_ref[...], preferred_element_type=F32)[0:1]
    hid = _gelu(a + pltpu.roll(b, ncp - 1, 0) + pet)
    y = jnp.dot(hid.astype(BF16), w2_ref[...], preferred_element_type=F32)
    if norm:
        y = _head_norm(y, g_ref[...], 1.0)
    o_ref[0] = y.astype(BF16)


def _compress(slabs, base, pe, w1, w2, gain, norm):
    _, t, _ = slabs.shape
    ncp = t // CMP_STRIDE
    width = CMP_STRIDE * HEAD_DIM
    chunks = slabs[base:base + NSA_KV_HEADS].reshape(NSA_KV_HEADS, ncp, width)
    w1f = w1.reshape(CMP_LEN * HEAD_DIM, CMP_HIDDEN).astype(BF16)
    pe8 = jnp.broadcast_to(pe.reshape(1, CMP_LEN * HEAD_DIM).astype(BF16), (8, CMP_LEN * HEAD_DIM))
    full = lambda shp: pl.BlockSpec(shp, lambda g: (0,) * len(shp))
    return pl.pallas_call(
        functools.partial(_compress_kernel, norm=norm), grid=(NSA_KV_HEADS,),
        in_specs=[pl.BlockSpec((1, ncp, width), lambda g: (g, 0, 0)),
                  full((CMP_LEN * HEAD_DIM, CMP_HIDDEN)), full((8, CMP_LEN * HEAD_DIM)),
                  full((CMP_HIDDEN, HEAD_DIM)), full((1, HEAD_DIM))],
        out_specs=pl.BlockSpec((1, ncp, HEAD_DIM), lambda g: (g, 0, 0)),
        out_shape=jax.ShapeDtypeStruct((NSA_KV_HEADS, ncp, HEAD_DIM), BF16),
        compiler_params=_cparams(("parallel",)), name="compress")(
            chunks, w1f, pe8, w2.astype(BF16), gain.reshape(1, HEAD_DIM))


def _cmp_attn_kernel(tbl_ref, q_ref, kc_ref, vc_ref, ov_ref, oc_ref, sel_ref, s_ref, key_ref,
                     *, tq, ncp, ns, nsel, ww):
    i = pl.program_id(0)
    rep = NSA_HEADS // NSA_KV_HEADS
    qpos = i * tq + lax.broadcasted_iota(jnp.int32, (tq, ncp), 0)
    dist = qpos - (lax.broadcasted_iota(jnp.int32, (tq, ncp), 1) * CMP_STRIDE + CMP_LEN - 1)
    mask = dist >= 0
    w0 = jnp.maximum(i * tq - (CMP_LEN - 1) - FAR_DIST + CMP_STRIDE, 0) // (CMP_STRIDE * LANE) * LANE
    w0 = pl.multiple_of(jnp.minimum(w0, ncp - ww), LANE)
    dist_w = (i * tq + lax.broadcasted_iota(jnp.int32, (tq, ww), 0)
              - ((w0 + lax.broadcasted_iota(jnp.int32, (tq, ww), 1)) * CMP_STRIDE + CMP_LEN - 1))
    bucket = _bucket_of(dist_w)
    blk = lax.broadcasted_iota(jnp.int32, (tq, ns), 1)
    qp = i * tq + lax.broadcasted_iota(jnp.int32, (tq, ns), 0)
    cur = qp // SLC_LEN
    forced = (blk == 0) | (blk == cur) | (blk == cur - 1)
    admissible = blk * SLC_LEN <= qp
    for g in range(NSA_KV_HEADS):
        s_ref[...] = _dot_nt(q_ref[g * rep:(g + 1) * rep].reshape(rep * tq, HEAD_DIM), kc_ref[g])
        imp = jnp.zeros((tq, ncp), F32)
        ps = []
        for r in range(rep):
            h = g * rep + r
            trow = _bias_row(tbl_ref, h, tq)
            bias = [jnp.take_along_axis(trow, bucket[:, c * LANE:(c + 1) * LANE], axis=1)
                    for c in range(ww // LANE)]
            s_ref[r * tq:(r + 1) * tq, pl.ds(w0, ww)] += jnp.concatenate(bias, axis=1)
            sh = jnp.where(mask, s_ref[r * tq:(r + 1) * tq, :], NEG)
            m = jnp.max(sh, axis=-1, keepdims=True)
            p = jnp.where(mask, jnp.exp2(sh - m), 0.0)
            pc = p / jnp.maximum(jnp.sum(p, axis=-1, keepdims=True), 1e-30)
            imp = imp + pc
            ps.append(pc.astype(BF16))
        o = jnp.dot(jnp.concatenate(ps, axis=0), vc_ref[g], preferred_element_type=F32)
        for r in range(rep):
            h = g * rep + r
            oc_ref[:, h * HEAD_DIM:(h + 1) * HEAD_DIM] = o[r * tq:(r + 1) * tq]
        hi = imp.astype(BF16)
        lo = (imp - hi.astype(F32)).astype(BF16)
        impb = (jnp.dot(hi, ov_ref[...], preferred_element_type=F32)
                + jnp.dot(lo, ov_ref[...], preferred_element_type=F32))
        impb = jnp.where(forced, FORCE_SCORE, impb)
        key_ref[:, g * tq:(g + 1) * tq] = _to_key(jnp.where(admissible, impb, -jnp.inf).T)
    keys = key_ref[...]
    thr = _kth_key_val(keys, nsel, axis=0)
    above = jnp.where(keys > thr, 1.0, 0.0)
    at = jnp.where(keys == thr, 1.0, 0.0)
    need = nsel - jnp.sum(above, axis=0, keepdims=True)
    blk_t = lax.broadcasted_iota(jnp.int32, keys.shape, 0)

    def halve(_, st):
        lo_c, hi_c = st
        mid = (lo_c + hi_c) >> 1
        enough = jnp.sum(jnp.where(blk_t < mid, at, 0.0), axis=0, keepdims=True) >= need
        return jnp.where(enough, lo_c, mid), jnp.where(enough, mid, hi_c)

    row = (1, keys.shape[1])
    surplus = jnp.max(jnp.sum(at, axis=0, keepdims=True) - need)
    steps = jnp.where(surplus > 0.5, max(ns, 2).bit_length(), 0)
    _, cut = lax.fori_loop(0, steps, halve, (jnp.zeros(row, jnp.int32), jnp.full(row, ns, jnp.int32)))
    sel_t = above + jnp.where(blk_t < cut, at, 0.0)
    for g in range(NSA_KV_HEADS):
        sel_ref[g] = sel_t[:, g * tq:(g + 1) * tq].T.astype(BF16)


def _cmp_attn(table, q_h, kc, vc):
    nheads, t, _ = q_h.shape
    ncp = kc.shape[1]
    ns = t // SLC_LEN
    nsel = min(SLC_TOP, ns)
    tq = _pick(t, 256)
    ww = min(ncp, LANE * (-(-(FAR_DIST + tq + CMP_STRIDE * LANE) // (CMP_STRIDE * LANE))))
    rep = nheads // NSA_KV_HEADS
    c_start = np.arange(ncp)[:, None] * CMP_STRIDE
    s_start = np.arange(ns)[None, :] * SLC_LEN
    overlap = jnp.asarray((c_start < s_start + SLC_LEN) & (c_start + CMP_LEN > s_start), BF16)
    full = lambda shp: pl.BlockSpec(shp, lambda i: (0,) * len(shp))
    return pl.pallas_call(
        functools.partial(_cmp_attn_kernel, tq=tq, ncp=ncp, ns=ns, nsel=nsel, ww=ww), grid=(t // tq,),
        in_specs=[full((nheads, LANE)),
                  pl.BlockSpec((nheads, tq, HEAD_DIM), lambda i: (0, i, 0)),
                  full((NSA_KV_HEADS, ncp, HEAD_DIM)), full((NSA_KV_HEADS, ncp, HEAD_DIM)), full((ncp, ns))],
        out_specs=[pl.BlockSpec((tq, nheads * HEAD_DIM), lambda i: (i, 0)),
                   pl.BlockSpec((NSA_KV_HEADS, tq, ns), lambda i: (0, i, 0))],
        out_shape=[jax.ShapeDtypeStruct((t, nheads * HEAD_DIM), F32),
                   jax.ShapeDtypeStruct((NSA_KV_HEADS, t, ns), BF16)],
        scratch_shapes=[pltpu.VMEM((rep * tq, ncp), F32), pltpu.VMEM((ns, NSA_KV_HEADS * tq), jnp.int32)],
        compiler_params=_cparams(("parallel",)), name="cmp_attn")(_table_rows(table), q_h, kc, vc, overlap)


def _merge_kernel(ya_ref, yb_ref, wa_ref, wb_ref, ga_ref, gb_ref, o_ref):
    za = jnp.dot(ya_ref[...], wa_ref[...], preferred_element_type=F32)
    zb = jnp.dot(yb_ref[...], wb_ref[...], preferred_element_type=F32)
    o_ref[...] = (jax.nn.sigmoid(ga_ref[...]) * za + jax.nn.sigmoid(gb_ref[...]) * zb).astype(o_ref.dtype)


def _merge(y_a, y_b, w_a, w_b, pm):
    t, k = y_a.shape
    d = w_a.shape[1]
    tm, tn = _pick(t, 1024), _pick(d, 512)
    assert OFF_GM % tn == 0
    lhs = pl.BlockSpec((tm, k), lambda i, j: (i, 0))
    rhs = pl.BlockSpec((k, tn), lambda i, j: (0, j))
    return pl.pallas_call(
        _merge_kernel, grid=(t // tm, d // tn),
        in_specs=[lhs, lhs, rhs, rhs, pl.BlockSpec((tm, tn), lambda i, j: (i, OFF_GM // tn + j)),
                  pl.BlockSpec((tm, tn), lambda i, j: (i, (OFF_GM + d) // tn + j))],
        out_specs=pl.BlockSpec((tm, tn), lambda i, j: (i, j)), out_shape=jax.ShapeDtypeStruct((t, d), BF16),
        compiler_params=_cparams(("parallel", "arbitrary")), name="merge")(y_a, y_b, w_a, w_b, pm, pm)


_CAND_FULL = PEER_TOPK // 2


def _peer_cands(a1, a2, op):
    pieces = [op(a1[0:1], a2)]
    pieces += [op(a1[k:k + 1], a2[0:_CAND_FULL]) for k in range(1, _CAND_FULL)]
    pieces.append(op(a1[_CAND_FULL:], a2[0:1]))
    return jnp.concatenate(pieces, axis=0)


def _peer_score_kernel(q_ref, sk_ref, s1_o, e1_o, s2_o, e2_o, thr_o, *, tm):
    row = lax.broadcasted_iota(jnp.int32, (PEER_NKEYS, tm), 0).astype(F32)
    for h in range(PEER_HEADS):
        svals, tops = [], []
        for c in range(2):
            hc = 2 * h + c
            s = _dot_nt(sk_ref[hc], q_ref[:, hc * 128:(hc + 1) * 128].astype(BF16))
            svals.append(s)
            work, top = s, []
            for _ in range(PEER_TOPK):
                mx = jnp.max(work, axis=0, keepdims=True)
                first = jnp.min(jnp.where(work == mx, row, float(PEER_NKEYS)), axis=0, keepdims=True)
                work = jnp.where(row == first, -jnp.inf, work)
                top.append(mx)
            tops.append(jnp.concatenate(top, axis=0))
        a1, a2 = tops
        cand = _peer_cands(a1, a2, jnp.add)
        thr = _kth_largest(cand, PEER_TOPK, 0)
        m1, m2 = a1[0:1], a2[0:1]
        ec = _peer_cands(jnp.exp(a1 - m1), jnp.exp(a2 - m2), jnp.multiply)
        z = jnp.sum(jnp.where(cand >= thr, ec, 0.0), axis=0, keepdims=True)
        s1_o[h] = svals[0]
        s2_o[h] = svals[1]
        e1_o[h] = jnp.exp(svals[0] - m1)
        e2_o[h] = jnp.exp(svals[1] - m2) / z
        thr_o[h:h + 1, :] = thr


def _peer_score(qp, sub_keys):
    t = qp.shape[0]
    tm = _pick(t, 256)
    sk = sub_keys.reshape(2 * PEER_HEADS, PEER_NKEYS, PEER_QDIM // 2).astype(BF16)
    tr = pl.BlockSpec((PEER_HEADS, PEER_NKEYS, tm), lambda i: (0, 0, i))
    shp = jax.ShapeDtypeStruct((PEER_HEADS, PEER_NKEYS, t), F32)
    return pl.pallas_call(
        functools.partial(_peer_score_kernel, tm=tm), grid=(t // tm,),
        in_specs=[pl.BlockSpec((tm, 2 * PEER_HEADS * 128), lambda i: (i, 0)),
                  pl.BlockSpec((2 * PEER_HEADS, PEER_NKEYS, PEER_QDIM // 2), lambda i: (0, 0, 0))],
        out_specs=[tr, tr, tr, tr, pl.BlockSpec((PEER_HEADS, tm), lambda i: (0, i))],
        out_shape=[shp, shp, shp, shp, jax.ShapeDtypeStruct((PEER_HEADS, t), F32)],
        compiler_params=_cparams(("parallel",)), name="peer_score")(qp, sk)


def _peer_act_kernel(h_ref, u_ref, s1_ref, e1_ref, s2_ref, e2_ref, thr_ref, o_ref, w_ref, a_ref, *, tm, te, sub):
    j = pl.program_id(1)
    n1 = te // PEER_NKEYS
    a_ref[...] = _dot_nt(h_ref[...], u_ref[...])
    for ai in range(n1):
        i1 = j * n1 + ai
        s1rows = [s1_ref[h, pl.ds(i1, 1), :] for h in range(PEER_HEADS)]
        e1rows = [e1_ref[h, pl.ds(i1, 1), :] for h in range(PEER_HEADS)]
        for ts in range(tm // LANE):
            tok = slice(ts * LANE, (ts + 1) * LANE)
            wt = jnp.zeros((PEER_NKEYS, LANE), F32)
            for h in range(PEER_HEADS):
                hit = (s1rows[h][:, tok] + s2_ref[h, :, tok]) >= thr_ref[h:h + 1, tok]
                wt = wt + jnp.where(hit, e1rows[h][:, tok] * e2_ref[h, :, tok], 0.0)
            w_ref[tok, ai * PEER_NKEYS:(ai + 1) * PEER_NKEYS] = wt.T
    for r in range(tm // sub):
        rows = slice(r * sub, (r + 1) * sub)
        o_ref[rows, :] = (_gelu(a_ref[rows, :]) * w_ref[rows, :]).astype(o_ref.dtype)


def _peer_act(h2, u, s1, e1, s2, e2, thr):
    t, d = h2.shape
    ne = u.shape[0]
    tm, te, sub = _pick(t, 1024), 512, 256
    once = dict(pipeline_mode=pl.Buffered(1))
    tok = pl.BlockSpec((PEER_HEADS, PEER_NKEYS, tm), lambda i, j: (0, 0, i), **once)
    return pl.pallas_call(
        functools.partial(_peer_act_kernel, tm=tm, te=te, sub=sub), grid=(t // tm, ne // te),
        in_specs=[pl.BlockSpec((tm, d), lambda i, j: (i, 0), **once),
                  pl.BlockSpec((te, d), lambda i, j: (j, 0)),
                  tok, tok, tok, tok, pl.BlockSpec((PEER_HEADS, tm), lambda i, j: (0, i), **once)],
        out_specs=pl.BlockSpec((tm, te), lambda i, j: (i, j)),
        out_shape=jax.ShapeDtypeStruct((t, ne), BF16),
        scratch_shapes=[pltpu.VMEM((tm, te), F32), pltpu.VMEM((tm, te), F32)],
        compiler_params=_cparams(("parallel", "arbitrary")), name="peer_act")(h2, u, s1, e1, s2, e2, thr)


def _pack_w_in(w_in, d):
    offs = np.cumsum([0, DSA_HEADS * 128, DSA_KV_HEADS * 128, DSA_KV_HEADS * 128, IDX_HEADS * IDX_DIM, IDX_DIM,
                      IDX_HEADS, NSA_HEADS * 128, 6 * NSA_KV_HEADS * 128, 3 * NSA_HEADS, 2 * d])
    qa, ka, va, qi, ki, wi, qb, kvb, gb, gm = [w_in[:, int(offs[n]):int(offs[n + 1])] for n in range(10)]
    main = jnp.concatenate([qa, qi, qb, kvb, ka, va, gm], axis=1).astype(BF16)
    pad = lambda n: jnp.zeros((d, n), w_in.dtype)
    tail = jnp.concatenate([ki, wi, pad(128 - IDX_DIM - IDX_HEADS), gb, pad(128 - 3 * NSA_HEADS)], axis=1).astype(BF16)
    return main, tail


def _token_mixer(h, table, w_in, gq_a, gk_a, gq_b, gk_cmp, gk_slc, gk_win, cmp_pe_k, cmp_w1_k, cmp_w2_k,
                 cmp_pe_v, cmp_w1_v, cmp_w2_v, w_branch_a, w_branch_b):
    t, d = h.shape
    w_main, w_tail = _pack_w_in(w_in, d)
    pm = _matmul(h, w_main, tn=1024, name="proj_main")
    pt = _matmul(h, w_tail, name="proj_tail")
    qa_h, qi_h, qb_h, kvb_h, ka_h, va_h, ki, wi, gsig = _prep(pm, pt, gq_a, gk_a, gq_b, gk_slc, gk_win)
    tq, tk = _pick(t, 256), _pick(t, 512)
    keys, thr = _dsa_index(qi_h, ki.T, wi, min(DSA_TOPK, t // 4))
    bias_a = _bias_tiles(table[:, :DSA_HEADS], tq, tk)
    y_a = _flash("dsa", qa_h, ka_h, va_h, bias_a, (keys, thr), BF16)
    g = NSA_KV_HEADS
    kc = _compress(kvb_h, 0, cmp_pe_k, cmp_w1_k, cmp_w2_k, gk_cmp, True)
    vc = _compress(kvb_h, g, cmp_pe_v, cmp_w1_v, cmp_w2_v, gk_cmp, False)
    o_c, sel = _cmp_attn(table[:, DSA_HEADS:], qb_h, kc, vc)
    bias_b = _bias_tiles(table[:, DSA_HEADS:], tq, tk)
    ns = t // SLC_LEN
    expand = jnp.asarray(np.arange(ns)[:, None] == (np.arange(t)[None, :] // SLC_LEN), BF16)
    o_s = _flash("slc", qb_h, kvb_h[2 * g:3 * g], kvb_h[3 * g:4 * g], bias_b, (sel, expand), F32)
    y_b = _flash("win", qb_h, kvb_h[4 * g:5 * g], kvb_h[5 * g:6 * g], bias_b, (o_c, o_s, gsig), BF16)
    return _merge(y_a, y_b, w_branch_a.astype(BF16), w_branch_b.astype(BF16), pm)


def _peer_ffn(h2, x1, gt2, w_q, sub_keys, u, v):
    qp = _matmul(h2, w_q.astype(BF16), tn=1024, name="peer_q")
    s1, e1, s2, e2, thr = _peer_score(qp, sub_keys)
    act = _peer_act(h2, u.astype(BF16), s1, e1, s2, e2, thr)
    return _matmul_resid(act, v.astype(BF16), x1, gt2, name="peer_out")


def kernel(x, c, rel_bias, w_ada, b_ada, g_mix, w_in, gq_a, gk_a, gq_b, gk_cmp, gk_slc, gk_win, cmp_pe_k, cmp_w1_k,
           cmp_w2_k, cmp_pe_v, cmp_w1_v, cmp_w2_v, w_branch_a, w_branch_b, w_out, g_ffn, w_peer_q, peer_sub_keys,
           peer_u, peer_v):
    bsz, t, d = x.shape
    assert bsz == 1 and t % 512 == 0 and d % 128 == 0
    xs = x[0]
    for i in range(w_ada.shape[0]):
        mod = _ada(c, w_ada[i], b_ada[i])
        sh1, sc1, gt1, sh2, sc2, gt2 = [mod[:, n * d:(n + 1) * d] for n in range(6)]
        h = _normmod(xs, g_mix[i], sc1, sh1)
        merged = _token_mixer(h, rel_bias, w_in[i], gq_a[i], gk_a[i], gq_b[i], gk_cmp[i], gk_slc[i], gk_win[i],
                              cmp_pe_k[i], cmp_w1_k[i], cmp_w2_k[i], cmp_pe_v[i], cmp_w1_v[i], cmp_w2_v[i],
                              w_branch_a[i], w_branch_b[i])
        x1 = _matmul_resid(merged, w_out[i].astype(BF16), xs, gt1, name="w_out")
        h2 = _normmod(x1, g_ffn[i], sc2, sh2)
        xs = _peer_ffn(h2, x1, gt2, w_peer_q[i], peer_sub_keys[i], peer_u[i], peer_v[i])
    return xs[None]
```
